```python
import math
import jax, jax.numpy as jnp
from jax import lax
import numpy as np

D_MODEL = 1024
BATCH = 16
SEQ = 4096
DEPTH = 2
DEC_BATCH = 8
DEC_SEQ = 64
PAST_LEN = 4096

CHUNK = 64
Q_BLOCK = 128
A_HEADS = 6
A_QK = 32
A_V = 2 * A_QK
B_GROUPS = 4
B_CH = 64
POOL_WINDOWS = (2, 4, 8, 16)
POOL_HIST = max(POOL_WINDOWS) - 1
C_HEADS = 6
C_NOPE = 64
C_ROPE = 32
C_V = 64
C_Q_RANK = 256
C_KV_RANK = 128
ROPE_BASE = 10000.0
REL_BUCKETS = 32
REL_MAX_DIST = 128
FFN_DIM = 2816
CONV_W = 3
EPS = 1e-6

A_WIDTH = A_HEADS * A_V
B_WIDTH = B_GROUPS * B_CH
C_WIDTH = C_HEADS * C_V
MIX_WIDTH = A_WIDTH + B_WIDTH + C_WIDTH
OFF_AQ = 0
OFF_AK = OFF_AQ + A_HEADS * 2 * A_QK
OFF_AV = OFF_AK + A_HEADS * 2 * A_QK
OFF_B = OFF_AV + A_WIDTH
OFF_CQ = OFF_B + B_WIDTH
OFF_CKV = OFF_CQ + C_Q_RANK
OFF_CKR = OFF_CKV + C_KV_RANK
IN_COLS = OFF_CKR + C_ROPE

kernel_name = "hybrid_streaming_encoder_step"


def rmsnorm(x, g):
    xf = x.astype(jnp.float32)
    y = xf * lax.rsqrt(jnp.mean(xf * xf, axis=-1, keepdims=True) + EPS)
    return (y * g.astype(jnp.float32)).astype(x.dtype)


def apply_rope(x, pos):
    half = C_ROPE // 2
    inv = 1.0 / (ROPE_BASE ** (jnp.arange(half, dtype=jnp.float32) / half))
    ang = pos.astype(jnp.float32)[:, None] * inv[None, :]
    cos, sin = jnp.cos(ang), jnp.sin(ang)
    if x.ndim == 4:
        cos, sin = cos[:, None], sin[:, None]
    xf = x.astype(jnp.float32)
    x1, x2 = xf[..., :half], xf[..., half:]
    return jnp.concatenate([x1 * cos - x2 * sin, x2 * cos + x1 * sin], axis=-1).astype(x.dtype)


def t5_bucket(rel):
    half = REL_BUCKETS // 2
    exact = half // 2
    ret = jnp.where(rel > 0, half, 0)
    n = jnp.abs(rel)
    nf = jnp.maximum(n, 1).astype(jnp.float32)
    large = exact + (jnp.log(nf / exact) / math.log(REL_MAX_DIST / exact) * (half - exact)).astype(jnp.int32)
    large = jnp.minimum(large, half - 1)
    return ret + jnp.where(n < exact, n, large)


def chunk_mask(q_pos, k_pos):
    return (k_pos[None, :] // CHUNK) <= (q_pos[:, None] // CHUNK)


def map_query_blocks(fn, q, q_pos):
    b, t = q.shape[0], q.shape[1]
    if t <= Q_BLOCK or t % Q_BLOCK:
        return fn(q, q_pos)
    nb = t // Q_BLOCK
    qb = q.reshape(b, nb, Q_BLOCK, *q.shape[2:]).swapaxes(0, 1)
    pb = q_pos.reshape(nb, Q_BLOCK)
    out = lax.map(lambda a: fn(a[0], a[1]), (qb, pb))
    out = out.swapaxes(0, 1)
    return out.reshape(b, t, *out.shape[3:])


def diff_attention(q_in, k_in, v_in, past_k, past_v, q_pos, rel_bias, lq1, lk1, lq2, lk2, subln_g, layer_idx):
    b, t, _ = q_in.shape
    q = q_in.reshape(b, t, A_HEADS, 2 * A_QK)
    k = k_in.reshape(b, t, A_HEADS, 2 * A_QK)
    v = v_in.reshape(b, t, A_HEADS, A_V)
    k_all = k if past_k is None else jnp.concatenate([past_k, k], axis=1)
    v_all = v if past_v is None else jnp.concatenate([past_v, v], axis=1)
    k_pos = jnp.arange(k_all.shape[1], dtype=jnp.int32)
    lam_init = 0.8 - 0.6 * math.exp(-0.3 * layer_idx)
    f32 = jnp.float32
    lam = (jnp.exp(jnp.sum(lq1.astype(f32) * lk1.astype(f32)))
           - jnp.exp(jnp.sum(lq2.astype(f32) * lk2.astype(f32))) + lam_init)
    k1 = k_all[..., :A_QK].astype(f32)
    k2 = k_all[..., A_QK:].astype(f32)
    vf = v_all.astype(f32)
    scale = A_QK ** -0.5

    def block(qb, qpb):
        qf = qb.astype(f32)
        bias = rel_bias.astype(f32)[t5_bucket(k_pos[None, :] - qpb[:, None])]
        bias = bias.transpose(2, 0, 1)[None]
        mask = chunk_mask(qpb, k_pos)[None, None]
        s1 = jnp.einsum('bqhd,bkhd->bhqk', qf[..., :A_QK], k1) * scale + bias
        s2 = jnp.einsum('bqhd,bkhd->bhqk', qf[..., A_QK:], k2) * scale + bias
        p = (jax.nn.softmax(jnp.where(mask, s1, -jnp.inf), axis=-1)
             - lam * jax.nn.softmax(jnp.where(mask, s2, -jnp.inf), axis=-1))
        return jnp.einsum('bhqk,bkhd->bqhd', p, vf)

    o = map_query_blocks(block, q, q_pos)
    o = rmsnorm(o, subln_g) * (1.0 - lam_init)
    return o.reshape(b, t, A_WIDTH).astype(v_in.dtype), k, v


def multiscale_pool(u, hist, q_pos, pool_w, pool_scale):
    b, t, _ = u.shape
    if hist is None:
        hist = jnp.zeros((b, POOL_HIST, B_WIDTH), u.dtype)
    ext = jnp.concatenate([hist, u], axis=1)
    cs = jnp.concatenate([jnp.zeros((b, 1, B_WIDTH), jnp.float32),
                          jnp.cumsum(ext.astype(jnp.float32), axis=1)], axis=1)
    uf = u.astype(jnp.float32)
    outs = []
    for g, w in enumerate(POOL_WINDOWS):
        sl = slice(g * B_CH, (g + 1) * B_CH)
        tot = cs[:, POOL_HIST + 1:POOL_HIST + 1 + t, sl] - cs[:, POOL_HIST + 1 - w:POOL_HIST + 1 - w + t, sl]
        cnt = jnp.minimum(q_pos + 1, w).astype(jnp.float32)[None, :, None]
        outs.append(tot / cnt - uf[..., sl])
    m = jnp.stack(outs, axis=2)
    y = jnp.einsum('btgc,gcd->btgd', m, pool_w.astype(jnp.float32)).reshape(b, t, B_WIDTH)
    y = y * pool_scale.astype(jnp.float32)
    return y.astype(u.dtype), ext[:, -POOL_HIST:]


def latent_attention(cq_in, ckv_in, kr_in, past_lat, past_kr, q_pos, q_norm_g, w_q_up, kv_norm_g, w_kv_up):
    b, t, _ = cq_in.shape
    f32 = jnp.float32
    q = (rmsnorm(cq_in, q_norm_g) @ w_q_up).reshape(b, t, C_HEADS, C_NOPE + C_ROPE)
    q = jnp.concatenate([q[..., :C_NOPE], apply_rope(q[..., C_NOPE:], q_pos)], axis=-1)
    lat = rmsnorm(ckv_in, kv_norm_g)
    kr = apply_rope(kr_in, q_pos)
    lat_all = lat if past_lat is None else jnp.concatenate([past_lat, lat], axis=1)
    kr_all = kr if past_kr is None else jnp.concatenate([past_kr, kr], axis=1)
    tk = lat_all.shape[1]
    kv = (lat_all @ w_kv_up).reshape(b, tk, C_HEADS, C_NOPE + C_V)
    k_nope = kv[..., :C_NOPE].astype(f32)
    vf = kv[..., C_NOPE:].astype(f32)
    krf = kr_all.astype(f32)
    k_pos = jnp.arange(tk, dtype=jnp.int32)
    scale = (C_NOPE + C_ROPE) ** -0.5

    def block(qb, qpb):
        qf = qb.astype(f32)
        s = (jnp.einsum('bqhd,bkhd->bhqk', qf[..., :C_NOPE], k_nope)
             + jnp.einsum('bqhd,bkd->bhqk', qf[..., C_NOPE:], krf)) * scale
        mask = chunk_mask(qpb, k_pos)[None, None]
        p = jax.nn.softmax(jnp.where(mask, s, -jnp.inf), axis=-1)
        return jnp.einsum('bhqk,bkhd->bqhd', p, vf)

    o = map_query_blocks(block, q, q_pos)
    return o.reshape(b, t, C_WIDTH).astype(cq_in.dtype), lat, kr


def conv_ffn(h, hist, w_up, conv_w, conv_b, w_down):
    b, t, _ = h.shape
    up = h @ w_up
    g, val = up[..., :FFN_DIM], up[..., FFN_DIM:]
    if hist is None:
        hist = jnp.zeros((b, CONV_W - 1, FFN_DIM), g.dtype)
    ext = jnp.concatenate([hist, g], axis=1)
    gc = sum(conv_w[j] * ext[:, j:j + t] for j in range(CONV_W)) + conv_b
    out = (jax.nn.silu(gc) * val) @ w_down
    return out, ext[:, -(CONV_W - 1):]


def setup_inputs(seed: int = 0) -> dict:
    key = jax.random.key(seed)
    ks = iter(jax.random.split(key, 40))
    D = D_MODEL

    def nrm(shape, s=1.0):
        return jax.random.normal(next(ks), shape, jnp.float32) * s

    def gain(shape):
        return 1.0 + nrm(shape, 0.02)

    return {
        "x_prompt": nrm((BATCH, SEQ, D)),
        "x_sample": nrm((DEC_BATCH, DEC_SEQ, D)),
        "c_prompt": nrm((BATCH, D)),
        "c_sample": nrm((DEC_BATCH, D)),
        "cache_a_k": nrm((DEPTH, DEC_BATCH, PAST_LEN, A_HEADS, 2 * A_QK)),
        "cache_a_v": nrm((DEPTH, DEC_BATCH, PAST_LEN, A_HEADS, A_V)),
        "cache_c_latent": nrm((DEPTH, DEC_BATCH, PAST_LEN, C_KV_RANK)),
        "cache_c_krope": nrm((DEPTH, DEC_BATCH, PAST_LEN, C_ROPE)),
        "state_b_pool": nrm((DEPTH, DEC_BATCH, POOL_HIST, B_WIDTH)),
        "state_ffn_conv": nrm((DEPTH, DEC_BATCH, CONV_W - 1, FFN_DIM)),
        "w_ada": nrm((DEPTH, D, 6 * D), D ** -0.5),
        "b_ada": nrm((DEPTH, 6 * D), 0.02),
        "g_mix": gain((DEPTH, D)),
        "w_in": nrm((DEPTH, D, IN_COLS), D ** -0.5),
        "lam_q1": nrm((DEPTH, A_QK), 0.1),
        "lam_k1": nrm((DEPTH, A_QK), 0.1),
        "lam_q2": nrm((DEPTH, A_QK), 0.1),
        "lam_k2": nrm((DEPTH, A_QK), 0.1),
        "a_subln_g": gain((DEPTH, A_V)),
        "rel_bias": nrm((REL_BUCKETS, A_HEADS), 0.5),
        "pool_w": nrm((DEPTH, B_GROUPS, B_CH, B_CH), B_CH ** -0.5),
        "pool_scale": gain((DEPTH, B_WIDTH)),
        "c_q_norm_g": gain((DEPTH, C_Q_RANK)),
        "w_q_up": nrm((DEPTH, C_Q_RANK, C_HEADS * (C_NOPE + C_ROPE)), C_Q_RANK ** -0.5),
        "c_kv_norm_g": gain((DEPTH, C_KV_RANK)),
        "w_kv_up": nrm((DEPTH, C_KV_RANK, C_HEADS * (C_NOPE + C_V)), C_KV_RANK ** -0.5),
        "w_out": nrm((DEPTH, MIX_WIDTH, D), MIX_WIDTH ** -0.5),
        "g_ffn": gain((DEPTH, D)),
        "w_up": nrm((DEPTH, D, 2 * FFN_DIM), D ** -0.5),
        "conv_w": nrm((DEPTH, CONV_W, FFN_DIM), CONV_W ** -0.5),
        "conv_b": nrm((DEPTH, FFN_DIM), 0.01),
        "w_down": nrm((DEPTH, FFN_DIM, D), FFN_DIM ** -0.5),
        "g_final": gain((D,)),
    }


def reference(x_prompt, x_sample, c_prompt, c_sample, cache_a_k, cache_a_v, cache_c_latent, cache_c_krope,
              state_b_pool, state_ffn_conv, w_ada, b_ada, g_mix, w_in, lam_q1, lam_k1, lam_q2, lam_k2,
              a_subln_g, rel_bias, pool_w, pool_scale, c_q_norm_g, w_q_up, c_kv_norm_g, w_kv_up, w_out,
              g_ffn, w_up, conv_w, conv_b, w_down, g_final):

    def run_layer(l, x, c, q_pos, past):
        pa_k, pa_v, p_lat, p_kr, p_pool, p_conv = past
        mod = jax.nn.silu(c) @ w_ada[l] + b_ada[l]
        sh1, sc1, gt1, sh2, sc2, gt2 = jnp.split(mod[:, None, :], 6, axis=-1)
        h = rmsnorm(x, g_mix[l]) * (1 + sc1) + sh1
        u = h @ w_in[l]
        a_out, a_k, a_v = diff_attention(
            u[..., OFF_AQ:OFF_AK], u[..., OFF_AK:OFF_AV], u[..., OFF_AV:OFF_B], pa_k, pa_v, q_pos,
            rel_bias, lam_q1[l], lam_k1[l], lam_q2[l], lam_k2[l], a_subln_g[l], l)
        b_out, b_hist = multiscale_pool(u[..., OFF_B:OFF_CQ], p_pool, q_pos, pool_w[l], pool_scale[l])
        c_out, c_lat, c_kr = latent_attention(
            u[..., OFF_CQ:OFF_CKV], u[..., OFF_CKV:OFF_CKR], u[..., OFF_CKR:IN_COLS], p_lat, p_kr, q_pos,
            c_q_norm_g[l], w_q_up[l], c_kv_norm_g[l], w_kv_up[l])
        mix = jnp.concatenate([a_out, b_out, c_out], axis=-1) @ w_out[l]
        x = x + gt1 * mix
        h = rmsnorm(x, g_ffn[l]) * (1 + sc2) + sh2
        f, conv_hist = conv_ffn(h, p_conv, w_up[l], conv_w[l], conv_b[l], w_down[l])
        x = x + gt2 * f
        return x, (a_k, a_v, c_lat, c_kr, b_hist, conv_hist)

    pos_p = jnp.arange(x_prompt.shape[1], dtype=jnp.int32)
    pos_s = cache_a_k.shape[2] + jnp.arange(x_sample.shape[1], dtype=jnp.int32)
    hp, hs = x_prompt, x_sample
    new_p, new_s = [], []
    for l in range(DEPTH):
        hp, st_p = run_layer(l, hp, c_prompt, pos_p, (None, None, None, None, None, None))
        new_p.append(st_p)
        hs, st_s = run_layer(l, hs, c_sample, pos_s,
                             (cache_a_k[l], cache_a_v[l], cache_c_latent[l], cache_c_krope[l],
                              state_b_pool[l], state_ffn_conv[l]))
        new_s.append(st_s)
    y_prompt = rmsnorm(hp, g_final)
    y_sample = rmsnorm(hs, g_final)
    a_k_p, a_v_p, lat_p, kr_p, pool_p, conv_p = [jnp.stack(z, axis=0) for z in zip(*new_p)]
    a_k_s, a_v_s, lat_s, kr_s, pool_s, conv_s = [jnp.stack(z, axis=0) for z in zip(*new_s)]
    return (y_prompt, y_sample, a_k_p, a_v_p, lat_p, kr_p, pool_p, conv_p,
            a_k_s, a_v_s, lat_s, kr_s, pool_s, conv_s)
```

```python
import functools
import math

import numpy as np
import jax
import jax.numpy as jnp
from jax import lax
from jax.experimental import pallas as pl
from jax.experimental.pallas import tpu as pltpu

F32 = jnp.float32
BF16 = jnp.bfloat16

CHUNK = 64
A_HEADS = 6
A_QK = 32
A_V = 64
B_GROUPS = 4
B_CH = 64
POOL_WINDOWS = (2, 4, 8, 16)
POOL_HIST = 15
C_HEADS = 6
C_NOPE = 64
C_ROPE = 32
C_V = 64
C_Q_RANK = 256
C_KV_RANK = 128
ROPE_BASE = 10000.0
REL_BUCKETS = 32
REL_MAX_DIST = 128
CONV_W = 3
EPS = 1e-6

A_WIDTH = A_HEADS * A_V
B_WIDTH = B_GROUPS * B_CH
C_WIDTH = C_HEADS * C_V
OFF_B = 3 * A_WIDTH
OFF_CQ = OFF_B + B_WIDTH
OFF_CKV = OFF_CQ + C_Q_RANK
OFF_CKR = OFF_CKV + C_KV_RANK

LANES = 128
ATT_BLOCK = 256
LOG2E = math.log2(math.e)
NEG_BIG = -1e30
VMEM_LIMIT = 56 * 1024 * 1024

N1 = OFF_CKR + 2 * LANES


def _cparams(sem):
    return pltpu.CompilerParams(dimension_semantics=sem, vmem_limit_bytes=VMEM_LIMIT)


def _rms(x):
    return x * lax.rsqrt(jnp.mean(x * x, axis=-1, keepdims=True) + EPS)


def _ada_kernel(c_ref, w_ref, b_ref, o_ref):
    c = c_ref[...]
    s = (c * jax.nn.sigmoid(c)).astype(BF16)
    o_ref[0] = jnp.dot(s, w_ref[0].astype(BF16), preferred_element_type=F32) + b_ref[0]


def _ada(c_all, w_ada, b_ada):
    depth, d, n = w_ada.shape
    nb = c_all.shape[0]
    tn = 1024
    return pl.pallas_call(
        _ada_kernel,
        grid=(depth, n // tn),
        in_specs=[pl.BlockSpec((nb, d), lambda l, j: (0, 0)),
                  pl.BlockSpec((1, d, tn), lambda l, j: (l, 0, j)),
                  pl.BlockSpec((1, 1, tn), lambda l, j: (l, 0, j))],
        out_specs=pl.BlockSpec((1, nb, tn), lambda l, j: (l, 0, j)),
        out_shape=jax.ShapeDtypeStruct((depth, nb, n), F32),
        compiler_params=_cparams(("arbitrary", "arbitrary")),
        name="ada_mod",
    )(c_all, w_ada, b_ada.reshape(depth, 1, n))


def _bias_kernel(rb_ref, bucket_ref, madd_ref, o_ref):
    h = pl.program_id(0)
    bucket = bucket_ref[...]
    far = rb_ref[REL_BUCKETS // 2 - 1, h]
    val = jnp.zeros(bucket.shape, F32)
    for b in range(REL_BUCKETS):
        val = jnp.where(bucket == b, rb_ref[b, h] - far, val)
    o_ref[0] = val * LOG2E + madd_ref[...]


def _bias_tiles(rel_bias, bucket, madd):
    r, c = bucket.shape
    return pl.pallas_call(
        _bias_kernel,
        grid=(A_HEADS,),
        in_specs=[pl.BlockSpec(memory_space=pltpu.SMEM),
                  pl.BlockSpec((r, c), lambda h: (0, 0)),
                  pl.BlockSpec((r, c), lambda h: (0, 0))],
        out_specs=pl.BlockSpec((1, r, c), lambda h: (h, 0, 0)),
        out_shape=jax.ShapeDtypeStruct((A_HEADS, r, c), F32),
        compiler_params=_cparams(("arbitrary",)),
        name="rel_bias_tiles",
    )(rel_bias, bucket, madd)


def _t5_bucket(rel):
    half = REL_BUCKETS // 2
    exact = half // 2
    ret = jnp.where(rel > 0, half, 0)
    n = jnp.abs(rel)
    nf = jnp.maximum(n, 1).astype(jnp.float32)
    large = exact + (jnp.log(nf / exact) / math.log(REL_MAX_DIST / exact) * (half - exact)).astype(jnp.int32)
    large = jnp.minimum(large, half - 1)
    return ret + jnp.where(n < exact, n, large)


def _tile_bucket_mask(q_pos, k_pos):
    q_pos = jnp.asarray(q_pos, jnp.int32)
    k_pos = jnp.asarray(k_pos, jnp.int32)
    bucket = _t5_bucket(k_pos[None, :] - q_pos[:, None]).astype(jnp.int32)
    visible = (k_pos[None, :] // CHUNK) <= (q_pos[:, None] // CHUNK)
    return bucket, jnp.where(visible, 0.0, NEG_BIG).astype(F32)


def _premix_kernel(x_ref, mod_ref, gmix_ref, w1_ref, cs_ref, sn_ref, csq_ref, gq_ref, wq_ref,
                   gkv_ref, wkv_ref, wp_ref, ps_ref, hist_ref,
                   aq_ref, ak_ref, av_ref, akb_ref, avb_ref, ub_ref, bo_ref, lat_ref, kr_ref,
                   cq_ref, ck_ref, cv_ref, carry_ref, *, tm, pos0):
    t = pl.program_id(1)
    x = x_ref[0]
    sh1 = mod_ref[0, 0:1, :]
    sc1 = mod_ref[0, 1:2, :]
    h = (_rms(x) * gmix_ref[...]) * (1.0 + sc1) + sh1
    u = jnp.dot(h.astype(BF16), w1_ref[...], preferred_element_type=F32)

    aq_ref[0] = (u[:, 0:A_WIDTH] * (A_QK ** -0.5 * LOG2E)).astype(BF16)
    ak = u[:, A_WIDTH:2 * A_WIDTH]
    av = u[:, 2 * A_WIDTH:3 * A_WIDTH]
    ak_ref[0] = ak
    av_ref[0] = av
    akb_ref[0] = ak.astype(BF16)
    avb_ref[0] = av.astype(BF16)

    ub = u[:, OFF_B:OFF_CQ]
    ub_ref[0] = ub

    @pl.when(t == 0)
    def _():
        carry_ref[...] = hist_ref[0]

    ext = jnp.concatenate([carry_ref[...], ub], axis=0)
    carry_ref[...] = ub[tm - 16:, :]
    s2 = ext + pltpu.roll(ext, 1, 0)
    s4 = s2 + pltpu.roll(s2, 2, 0)
    s8 = s4 + pltpu.roll(s4, 4, 0)
    s16 = s8 + pltpu.roll(s8, 8, 0)
    lane = lax.broadcasted_iota(jnp.int32, (tm, B_WIDTH), 1)
    grp = lane // B_CH
    tot = jnp.where(grp == 0, s2[16:], jnp.where(grp == 1, s4[16:], jnp.where(grp == 2, s8[16:], s16[16:])))
    win = jnp.where(grp == 0, 2, jnp.where(grp == 1, 4, jnp.where(grp == 2, 8, 16)))
    pos = pos0 + t * tm + lax.broadcasted_iota(jnp.int32, (tm, B_WIDTH), 0)
    cnt = jnp.minimum(pos + 1, win).astype(F32)
    m = tot / cnt - ub
    y = jnp.dot(m.astype(BF16), wp_ref[...], preferred_element_type=F32) * ps_ref[...]
    bo_ref[0] = y.astype(BF16)

    cs = cs_ref[...]
    sn = sn_ref[...]
    csq = csq_ref[...]
    qn = (_rms(u[:, OFF_CQ:OFF_CKV]) * gq_ref[...]).astype(BF16)
    q2 = jnp.dot(qn, wq_ref[...], preferred_element_type=F32)
    qscale = (C_NOPE + C_ROPE) ** -0.5 * LOG2E
    lat = _rms(u[:, OFF_CKV:OFF_CKR]) * gkv_ref[...]
    lat_ref[0] = lat
    krp = u[:, OFF_CKR:OFF_CKR + LANES] * cs + u[:, OFF_CKR + LANES:N1] * sn
    kr_ref[0] = krp[:, 0:C_ROPE]
    kv = jnp.dot(lat.astype(BF16), wkv_ref[...], preferred_element_type=F32)
    for hh in range(C_HEADS):
        sl = slice(hh * LANES, (hh + 1) * LANES)
        sr = slice((C_HEADS + hh) * LANES, (C_HEADS + hh + 1) * LANES)
        cq_ref[0, :, sl] = ((q2[:, sl] * csq + q2[:, sr] * sn) * qscale).astype(BF16)
        ck_ref[0, :, sl] = (kv[:, sl] + krp).astype(BF16)
    cv_ref[0] = kv[:, C_HEADS * LANES:].astype(BF16)


def _premix(x, mod, gmix, w1, cs, sn, csq, gq, wq, gkv, wkv, wp, ps, hist16, pos0):
    b, t, d = x.shape
    tm = min(512, t)
    assert t % tm == 0 and tm >= 16
    full = lambda shape: pl.BlockSpec(shape, lambda i, j: (0,) * len(shape))
    tile = lambda n: pl.BlockSpec((1, tm, n), lambda i, j: (i, j, 0))
    tab = pl.BlockSpec((tm, LANES), lambda i, j: (j, 0))
    outs = [(A_WIDTH, BF16), (A_WIDTH, F32), (A_WIDTH, F32), (A_WIDTH, BF16), (A_WIDTH, BF16),
            (B_WIDTH, F32), (B_WIDTH, BF16), (C_KV_RANK, F32), (C_ROPE, F32),
            (C_HEADS * LANES, BF16), (C_HEADS * LANES, BF16), (C_WIDTH, BF16)]
    return pl.pallas_call(
        functools.partial(_premix_kernel, tm=tm, pos0=pos0),
        grid=(b, t // tm),
        in_specs=[tile(d),
                  pl.BlockSpec((1, 6, d), lambda i, j: (i, 0, 0)),
                  full((1, d)), full(w1.shape), tab, tab, tab,
                  full((1, C_Q_RANK)), full(wq.shape), full((1, C_KV_RANK)), full(wkv.shape),
                  full(wp.shape), full((1, B_WIDTH)),
                  pl.BlockSpec((1, 16, B_WIDTH), lambda i, j: (i, 0, 0))],
        out_specs=[tile(n) for n, _ in outs],
        out_shape=[jax.ShapeDtypeStruct((b, t, n), dt) for n, dt in outs],
        scratch_shapes=[pltpu.VMEM((16, B_WIDTH), F32)],
        compiler_params=_cparams(("arbitrary", "arbitrary")),
        name="premix",
    )(x, mod, gmix, w1, cs, sn, csq, gq, wq, gkv, wkv, wp, ps, hist16)


def _kvpast_kernel(lat_ref, kr_ref, wkv_ref, e_ref, ck_ref, cv_ref):
    kv = jnp.dot(lat_ref[0].astype(BF16), wkv_ref[...], preferred_element_type=F32)
    krp = jnp.dot(kr_ref[0].astype(BF16), e_ref[...], preferred_element_type=F32)
    for hh in range(C_HEADS):
        sl = slice(hh * LANES, (hh + 1) * LANES)
        ck_ref[0, :, sl] = (kv[:, sl] + krp).astype(BF16)
    cv_ref[0] = kv[:, C_HEADS * LANES:].astype(BF16)


def _kvpast(lat, kr, wkv):
    b, p, _ = lat.shape
    tm = min(512, p)
    assert p % tm == 0
    e = jnp.eye(C_ROPE, LANES, dtype=BF16)
    return pl.pallas_call(
        _kvpast_kernel,
        grid=(b, p // tm),
        in_specs=[pl.BlockSpec((1, tm, C_KV_RANK), lambda i, j: (i, j, 0)),
                  pl.BlockSpec((1, tm, C_ROPE), lambda i, j: (i, j, 0)),
                  pl.BlockSpec(wkv.shape, lambda i, j: (0, 0)),
                  pl.BlockSpec(e.shape, lambda i, j: (0, 0))],
        out_specs=[pl.BlockSpec((1, tm, C_HEADS * LANES), lambda i, j: (i, j, 0)),
                   pl.BlockSpec((1, tm, C_WIDTH), lambda i, j: (i, j, 0))],
        out_shape=[jax.ShapeDtypeStruct((b, p, C_HEADS * LANES), BF16),
                   jax.ShapeDtypeStruct((b, p, C_WIDTH), BF16)],
        compiler_params=_cparams(("arbitrary", "arbitrary")),
        name="mla_kv_past",
    )(lat, kr, wkv, e)


def _online_update(carry, s, v):
    m, l, acc = carry
    m_new = jnp.maximum(m, jnp.max(s, axis=-1, keepdims=True))
    alpha = jnp.exp2(m - m_new)
    p = jnp.exp2(s - m_new)
    l = alpha * l + jnp.sum(p, axis=-1, keepdims=True)
    acc = alpha * acc + jnp.dot(p.astype(BF16), v, preferred_element_type=F32)
    return m_new, l, acc


def _qk(q, k):
    return lax.dot_general(q, k, (((1,), (1,)), ((), ())), preferred_element_type=F32)


def _attn_kernel(*refs, diff, sub_bias, diag_bias, bq, bk, n_all, lam_init):
    it = iter(refs)
    lamp_ref = next(it) if diff else None
    g_ref = next(it) if diff else None
    q_ref, ka_ref, va_ref, kd_ref, vd_ref = (next(it) for _ in range(5))
    bs_ref = next(it) if sub_bias else None
    bd_ref = next(it) if diag_bias else None
    o_ref = next(it)

    qi = pl.program_id(2)
    na = qi if n_all is None else n_all
    lane = lax.broadcasted_iota(jnp.int32, (bq, LANES), 1)
    q = q_ref[0]
    zero = jnp.zeros_like(q)

    if diff:
        qs = jnp.concatenate(
            [jnp.where((lane >= A_QK * v) & (lane < A_QK * (v + 1)), q, zero) for v in range(4)], axis=0)
        groups = [(qs, 0, [0, 0, 1, 1])]
    else:
        groups = [(q[:, 0:LANES], 0, [0]), (q[:, LANES:2 * LANES], 1, [0])]

    def stack_bias(ref, bias_idx):
        tiles = [ref[0] if ref.shape[0] == 1 else ref[i] for i in bias_idx]
        return tiles[0] if len(tiles) == 1 else jnp.concatenate(tiles, axis=0)

    def step(carries, k, v, biases):
        out = []
        for (qg, kcol, _), c, bias in zip(groups, carries, biases):
            s = _qk(qg, k[:, kcol * LANES:(kcol + 1) * LANES])
            out.append(_online_update(c, s if bias is None else s + bias, v))
        return tuple(out)

    carries = tuple((jnp.full((len(bi) * bq, 1), NEG_BIG, F32), jnp.zeros((len(bi) * bq, 1), F32),
                     jnp.zeros((len(bi) * bq, LANES), F32)) for _, _, bi in groups)
    biases = [stack_bias(bd_ref, bi) if diag_bias else None for _, _, bi in groups]
    carries = step(carries, kd_ref[0], vd_ref[0], biases)
    if sub_bias:
        off = pl.multiple_of(jnp.maximum(na - 1, 0) * bk, bk)
        gate = jnp.where(na >= 1, 0.0, NEG_BIG)
        biases = [stack_bias(bs_ref, bi) + gate for _, _, bi in groups]
        carries = step(carries, ka_ref[0, pl.ds(off, bk), :], va_ref[0, pl.ds(off, bk), :], biases)
        n_far = jnp.maximum(na - 1, 0)
    else:
        n_far = na

    def body(j, cs):
        off = pl.multiple_of(j * bk, bk)
        return step(cs, ka_ref[0, pl.ds(off, bk), :], va_ref[0, pl.ds(off, bk), :], [None] * len(groups))

    carries = lax.fori_loop(0, n_far, body, carries)
    results = []
    for (_, _, bi), (m, l, acc) in zip(groups, carries):
        o = acc / l
        results.extend(o[i * bq:(i + 1) * bq] for i in range(len(bi)))

    if diff:
        lp = lamp_ref[...]
        lam = (jnp.exp(jnp.sum(lp[0:1] * lp[1:2], axis=-1, keepdims=True))
               - jnp.exp(jnp.sum(lp[2:3] * lp[3:4], axis=-1, keepdims=True)) + lam_init)
        outs = []
        for e in range(2):
            d = results[2 * e] - lam * results[2 * e + 1]
            own = (lane >= A_V * e) & (lane < A_V * (e + 1))
            ms = jnp.sum(jnp.where(own, d * d, 0.0), axis=-1, keepdims=True) * (1.0 / A_V)
            outs.append(d * lax.rsqrt(ms + EPS) * g_ref[...] * (1.0 - lam_init))
        out = jnp.where(lane < A_V, outs[0], outs[1])
    else:
        out = jnp.where(lane < C_V, results[0], results[1])
    o_ref[0] = out.astype(o_ref.dtype)


def _attention(q, k_all, v_all, k_new, v_new, bias_sub, bias_diag, *, diff, causal_blocks,
               lamp=None, g2=None, lam_init=0.0):
    b, tq, _ = q.shape
    t_all = k_all.shape[1]
    bk = ATT_BLOCK
    bq = min(ATT_BLOCK, tq)
    assert tq % bq == 0 and t_all % bk == 0
    if causal_blocks:
        assert bq == bk and t_all == tq
    else:
        assert tq == bq
    qw = LANES if diff else 2 * LANES
    n_pairs = A_HEADS // 2
    in_specs, args = [], []
    if diff:
        in_specs += [pl.BlockSpec(lamp.shape, lambda i, p, j: (0, 0)),
                     pl.BlockSpec(g2.shape, lambda i, p, j: (0, 0))]
        args += [lamp, g2]
    in_specs += [pl.BlockSpec((1, bq, qw), lambda i, p, j: (i, j, p)),
                 pl.BlockSpec((1, t_all, qw), lambda i, p, j: (i, 0, p)),
                 pl.BlockSpec((1, t_all, LANES), lambda i, p, j: (i, 0, p)),
                 pl.BlockSpec((1, bq, qw), lambda i, p, j: (i, j, p)),
                 pl.BlockSpec((1, bq, LANES), lambda i, p, j: (i, j, p))]
    args += [q, k_all, v_all, k_new, v_new]
    for bias in (bias_sub, bias_diag):
        if bias is not None:
            if bias.shape[0] == 1:
                in_specs.append(pl.BlockSpec(bias.shape, lambda i, p, j: (0, 0, 0)))
            else:
                in_specs.append(pl.BlockSpec((2,) + bias.shape[1:], lambda i, p, j: (p, 0, 0)))
            args.append(bias)
    kern = functools.partial(
        _attn_kernel, diff=diff, sub_bias=bias_sub is not None, diag_bias=bias_diag is not None,
        bq=bq, bk=bk, n_all=None if causal_blocks else t_all // bk, lam_init=lam_init)
    return pl.pallas_call(
        kern,
        grid=(b, n_pairs, tq // bq),
        in_specs=in_specs,
        out_specs=pl.BlockSpec((1, bq, LANES), lambda i, p, j: (i, j, p)),
        out_shape=jax.ShapeDtypeStruct((b, tq, n_pairs * LANES), BF16),
        compiler_params=_cparams(("arbitrary", "arbitrary", "arbitrary")),
        name="diff_attn" if diff else "mla_attn",
    )(*args)


def _postffn_kernel(a_ref, b_ref, c_ref, x_ref, mod_ref, wo_ref, gffn_ref, wg_ref, wv_ref, cw_ref,
                    cb_ref, wd_ref, hist_ref, gfin_ref, y_ref, conv_ref, carry_ref, *, tm, final):
    t = pl.program_id(1)
    cat = jnp.concatenate([a_ref[0], b_ref[0], c_ref[0]], axis=-1)
    mix = jnp.dot(cat, wo_ref[...], preferred_element_type=F32)
    gt1 = mod_ref[0, 2:3, :]
    sh2 = mod_ref[0, 3:4, :]
    sc2 = mod_ref[0, 4:5, :]
    gt2 = mod_ref[0, 5:6, :]
    x1 = x_ref[0] + gt1 * mix
    h2 = ((_rms(x1) * gffn_ref[...]) * (1.0 + sc2) + sh2).astype(BF16)
    g = jnp.dot(h2, wg_ref[...], preferred_element_type=F32)
    val = jnp.dot(h2, wv_ref[...], preferred_element_type=F32)

    @pl.when(t == 0)
    def _():
        carry_ref[...] = hist_ref[0]

    ext = jnp.concatenate([carry_ref[...], g], axis=0)
    tail = g[tm - 8:, :]
    carry_ref[...] = tail
    conv_ref[0] = tail
    gc = (cw_ref[0:1, :] * pltpu.roll(ext, 2, 0)[8:] + cw_ref[1:2, :] * pltpu.roll(ext, 1, 0)[8:]
          + cw_ref[2:3, :] * g + cb_ref[...])
    act = (gc * jax.nn.sigmoid(gc) * val).astype(BF16)
    f = jnp.dot(act, wd_ref[...], preferred_element_type=F32)
    x2 = x1 + gt2 * f
    if final:
        x2 = _rms(x2) * gfin_ref[...]
    y_ref[0] = x2


def _postffn(a, bo, c, x, mod, wo, gffn, wg, wv, cw, cb, wd, hist8, gfin, final):
    b, t, d = x.shape
    f = wg.shape[1]
    tm = min(256, t)
    assert t % tm == 0 and tm >= 8
    tile = lambda n: pl.BlockSpec((1, tm, n), lambda i, j: (i, j, 0))

    def resident(shape):
        return pl.BlockSpec(shape, lambda i, j: (0,) * len(shape), pipeline_mode=pl.Buffered(1))

    return pl.pallas_call(
        functools.partial(_postffn_kernel, tm=tm, final=final),
        grid=(b, t // tm),
        in_specs=[tile(A_WIDTH), tile(B_WIDTH), tile(C_WIDTH), tile(d),
                  pl.BlockSpec((1, 6, d), lambda i, j: (i, 0, 0)),
                  resident(wo.shape), resident((1, d)), resident(wg.shape), resident(wv.shape),
                  resident((CONV_W, f)), resident((1, f)), resident(wd.shape),
                  pl.BlockSpec((1, 8, f), lambda i, j: (i, 0, 0)),
                  resident((1, d))],
        out_specs=[tile(d), pl.BlockSpec((1, 8, f), lambda i, j: (i, 0, 0))],
        out_shape=[jax.ShapeDtypeStruct((b, t, d), F32), jax.ShapeDtypeStruct((b, 8, f), F32)],
        scratch_shapes=[pltpu.VMEM((8, f), F32)],
        compiler_params=_cparams(("arbitrary", "arbitrary")),
        name="postmix_ffn",
    )(a, bo, c, x, mod, wo, gffn, wg, wv, cw, cb, wd, hist8, gfin)


def _rot_cols(w):
    half = C_ROPE // 2
    return jnp.concatenate([-w[..., half:], w[..., :half]], axis=-1)


def _layer_weights(l, w_in, w_q_up, w_kv_up, pool_w, w_out, w_up):
    d = w_in.shape[1]
    wkr = w_in[l][:, OFF_CKR:]
    zpad = jnp.zeros((d, LANES - C_ROPE), F32)
    w1 = jnp.concatenate([w_in[l][:, :OFF_CKR], wkr, zpad, _rot_cols(wkr), zpad], axis=1).astype(BF16)

    wq = w_q_up[l].reshape(C_Q_RANK, C_HEADS, C_NOPE + C_ROPE)
    nope, rope = wq[..., :C_NOPE], wq[..., C_NOPE:]
    zq = jnp.zeros((C_Q_RANK, C_HEADS, LANES - C_NOPE - C_ROPE), F32)
    main = jnp.concatenate([rope, nope, zq], axis=-1).reshape(C_Q_RANK, C_HEADS * LANES)
    rot = jnp.concatenate([_rot_cols(rope), jnp.zeros((C_Q_RANK, C_HEADS, LANES - C_ROPE), F32)],
                          axis=-1).reshape(C_Q_RANK, C_HEADS * LANES)
    wq2 = jnp.concatenate([main, rot], axis=1).astype(BF16)

    wkv = w_kv_up[l].reshape(C_KV_RANK, C_HEADS, C_NOPE + C_V)
    zk = jnp.zeros((C_KV_RANK, C_HEADS, C_ROPE), F32)
    kpart = jnp.concatenate([zk, wkv[..., :C_NOPE], zk], axis=-1).reshape(C_KV_RANK, C_HEADS * LANES)
    vpart = wkv[..., C_NOPE:].reshape(C_KV_RANK, C_WIDTH)
    wkv2 = jnp.concatenate([kpart, vpart], axis=1).astype(BF16)

    wp = jnp.zeros((B_WIDTH, B_WIDTH), F32)
    for g in range(B_GROUPS):
        wp = wp.at[g * B_CH:(g + 1) * B_CH, g * B_CH:(g + 1) * B_CH].set(pool_w[l, g])
    ffn = w_up.shape[2] // 2
    return dict(w1=w1, wq=wq2, wkv=wkv2, wp=wp.astype(BF16), wo=w_out[l].astype(BF16),
                wg=w_up[l][:, :ffn].astype(BF16), wv=w_up[l][:, ffn:].astype(BF16))


def _rope_tables(pos):
    half = C_ROPE // 2
    inv = 1.0 / (ROPE_BASE ** (jnp.arange(half, dtype=jnp.float32) / half))
    ang = pos.astype(jnp.float32)[:, None] * inv[None, :]
    cos, sin = jnp.cos(ang), jnp.sin(ang)
    n = pos.shape[0]
    z = jnp.zeros((n, LANES - C_ROPE), F32)
    cs = jnp.concatenate([cos, cos, z], axis=1)
    sn = jnp.concatenate([sin, sin, z], axis=1)
    csq = jnp.concatenate([cos, cos, jnp.ones((n, C_NOPE), F32), jnp.zeros((n, LANES - C_ROPE - C_NOPE), F32)], axis=1)
    return cs, sn, csq


def kernel(x_prompt, x_sample, c_prompt, c_sample, cache_a_k, cache_a_v, cache_c_latent, cache_c_krope,
           state_b_pool, state_ffn_conv, w_ada, b_ada, g_mix, w_in, lam_q1, lam_k1, lam_q2, lam_k2,
           a_subln_g, rel_bias, pool_w, pool_scale, c_q_norm_g, w_q_up, c_kv_norm_g, w_kv_up, w_out,
           g_ffn, w_up, conv_w, conv_b, w_down, g_final):
    depth = w_in.shape[0]
    bp, tp, d = x_prompt.shape
    bs, ts, _ = x_sample.shape
    past = cache_a_k.shape[2]
    ffn = conv_w.shape[2]
    blk = ATT_BLOCK

    mod_all = _ada(jnp.concatenate([c_prompt, c_sample], axis=0), w_ada, b_ada)
    mod_all = mod_all.reshape(depth, bp + bs, 6, d)

    r = np.arange(blk)
    bkt, madd = _tile_bucket_mask(blk + r, r)
    pb_sub = _bias_tiles(rel_bias, bkt, madd)
    bkt, madd = _tile_bucket_mask(r, r)
    pb_diag = _bias_tiles(rel_bias, bkt, madd)
    mla_diag = madd[None]
    rs = np.arange(ts)
    bkt, madd = _tile_bucket_mask(past + rs, past - blk + r)
    sb_sub = _bias_tiles(rel_bias, bkt, madd)
    bkt, madd = _tile_bucket_mask(past + rs, past + rs)
    sb_diag = _bias_tiles(rel_bias, bkt, madd)
    mla_sdiag = madd[None]

    tabs_p = _rope_tables(jnp.arange(tp, dtype=jnp.int32))
    tabs_s = _rope_tables(past + jnp.arange(ts, dtype=jnp.int32))
    zero_pool = jnp.zeros((bp, 16, B_WIDTH), F32)
    zero_conv = jnp.zeros((bp, 8, ffn), F32)

    hp, hs = x_prompt, x_sample
    new_p, new_s = [], []
    for l in range(depth):
        w = _layer_weights(l, w_in, w_q_up, w_kv_up, pool_w, w_out, w_up)
        lam_init = 0.8 - 0.6 * math.exp(-0.3 * l)
        lamp = jnp.stack([lam_q1[l], lam_k1[l], lam_q2[l], lam_k2[l]], axis=0)
        g2 = jnp.concatenate([a_subln_g[l], a_subln_g[l]])[None]
        wd = w_down[l].astype(BF16)
        last = l == depth - 1

        def run(x, mod, tabs, pos0, hist16, hist8, pasts):
            (aq, ak, av, akb, avb, ub, bo, lat, kr, cq, ck, cv) = _premix(
                x, mod, g_mix[l][None], w["w1"], *tabs, c_q_norm_g[l][None], w["wq"],
                c_kv_norm_g[l][None], w["wkv"], w["wp"], pool_scale[l][None], hist16, pos0)
            if pasts is None:
                a_out = _attention(aq, akb, avb, akb, avb, pb_sub, pb_diag, diff=True, causal_blocks=True,
                                   lamp=lamp, g2=g2, lam_init=lam_init)
                c_out = _attention(cq, ck, cv, ck, cv, None, mla_diag, diff=False, causal_blocks=True)
            else:
                pk, pv, plat, pkr = pasts
                pkb = pk.reshape(pk.shape[0], past, A_WIDTH).astype(BF16)
                pvb = pv.reshape(pv.shape[0], past, A_WIDTH).astype(BF16)
                a_out = _attention(aq, pkb, pvb, akb, avb, sb_sub, sb_diag, diff=True, causal_blocks=False,
                                   lamp=lamp, g2=g2, lam_init=lam_init)
                ckp, cvp = _kvpast(plat, pkr, w["wkv"])
                c_out = _attention(cq, ckp, cvp, ck, cv, None, mla_sdiag, diff=False, causal_blocks=False)
            y, conv8 = _postffn(a_out, bo, c_out, x, mod, w["wo"], g_ffn[l][None], w["wg"], w["wv"],
                                conv_w[l], conv_b[l][None], wd, hist8, g_final[None], last)
            bsz, tt = x.shape[0], x.shape[1]
            state = (ak.reshape(bsz, tt, A_HEADS, 2 * A_QK), av.reshape(bsz, tt, A_HEADS, A_V), lat, kr,
                     ub[:, tt - POOL_HIST:], conv8[:, 8 - (CONV_W - 1):])
            return y, state

        hp, st = run(hp, mod_all[l, :bp], tabs_p, 0, zero_pool, zero_conv, None)
        new_p.append(st)
        hist16 = jnp.pad(state_b_pool[l], ((0, 0), (1, 0), (0, 0)))
        hist8 = jnp.pad(state_ffn_conv[l], ((0, 0), (8 - (CONV_W - 1), 0), (0, 0)))
        hs, st = run(hs, mod_all[l, bp:], tabs_s, past, hist16, hist8,
                     (cache_a_k[l], cache_a_v[l], cache_c_latent[l], cache_c_krope[l]))
        new_s.append(st)

    outs_p = [jnp.stack(z, axis=0) for z in zip(*new_p)]
    outs_s = [jnp.stack(z, axis=0) for z in zip(*new_s)]
    return (hp, hs, *outs_p, *outs_s)
```

```python
import functools
import math

import numpy as np
import jax
import jax.numpy as jnp
from jax import lax
from jax.experimental import pallas as pl
from jax.experimental.pallas import tpu as pltpu

F32 = jnp.float32
BF16 = jnp.bfloat16

CHUNK = 64
A_HEADS = 6
A_QK = 32
A_V = 64
B_GROUPS = 4
B_CH = 64
POOL_WINDOWS = (2, 4, 8, 16)
POOL_HIST = 15
C_HEADS = 6
C_NOPE = 64
C_ROPE = 32
C_V = 64
C_Q_RANK = 256
C_KV_RANK = 128
ROPE_BASE = 10000.0
REL_BUCKETS = 32
REL_MAX_DIST = 128
CONV_W = 3
EPS = 1e-6

A_WIDTH = A_HEADS * A_V
B_WIDTH = B_GROUPS * B_CH
C_WIDTH = C_HEADS * C_V
OFF_B = 3 * A_WIDTH
OFF_CQ = OFF_B + B_WIDTH
OFF_CKV = OFF_CQ + C_Q_RANK
OFF_CKR = OFF_CKV + C_KV_RANK

LANES = 128
ATT_BLOCK = 256
LOG2E = math.log2(math.e)
NEG_BIG = -1e30
VMEM_LIMIT = 56 * 1024 * 1024

N1 = OFF_CKR + 2 * LANES


def _cparams(sem):
    return pltpu.CompilerParams(dimension_semantics=sem, vmem_limit_bytes=VMEM_LIMIT)


def _rms(x):
    return x * lax.rsqrt(jnp.mean(x * x, axis=-1, keepdims=True) + EPS)


def _ada_kernel(c_ref, w_ref, b_ref, o_ref):
    c = c_ref[...]
    s = (c * jax.nn.sigmoid(c)).astype(BF16)
    o_ref[0] = jnp.dot(s, w_ref[0].astype(BF16), preferred_element_type=F32) + b_ref[0]


def _ada(c_all, w_ada, b_ada):
    depth, d, n = w_ada.shape
    nb = c_all.shape[0]
    tn = 1024
    return pl.pallas_call(
        _ada_kernel,
        grid=(depth, n // tn),
        in_specs=[pl.BlockSpec((nb, d), lambda l, j: (0, 0)),
                  pl.BlockSpec((1, d, tn), lambda l, j: (l, 0, j)),
                  pl.BlockSpec((1, 1, tn), lambda l, j: (l, 0, j))],
        out_specs=pl.BlockSpec((1, nb, tn), lambda l, j: (l, 0, j)),
        out_shape=jax.ShapeDtypeStruct((depth, nb, n), F32),
        compiler_params=_cparams(("arbitrary", "arbitrary")),
        name="ada_mod",
    )(c_all, w_ada, b_ada.reshape(depth, 1, n))


def _bias_kernel(rb_ref, bucket_ref, madd_ref, o_ref, mx_ref):
    h = pl.program_id(0)
    bucket = bucket_ref[...]
    far = rb_ref[REL_BUCKETS // 2 - 1, h]
    val = jnp.zeros(bucket.shape, F32)
    top = far - far
    for b in range(REL_BUCKETS):
        val = jnp.where(bucket == b, rb_ref[b, h] - far, val)
        top = jnp.maximum(top, rb_ref[b, h] - far)
    o_ref[0] = val * LOG2E + madd_ref[...]
    mx_ref[0] = jnp.full(mx_ref.shape[1:], top * LOG2E, F32)


def _bias_tiles(rel_bias, bucket, madd):
    r, c = bucket.shape
    return pl.pallas_call(
        _bias_kernel,
        grid=(A_HEADS,),
        in_specs=[pl.BlockSpec(memory_space=pltpu.SMEM),
                  pl.BlockSpec((r, c), lambda h: (0, 0)),
                  pl.BlockSpec((r, c), lambda h: (0, 0))],
        out_specs=[pl.BlockSpec((1, r, c), lambda h: (h, 0, 0)),
                   pl.BlockSpec((1, 8, LANES), lambda h: (h, 0, 0))],
        out_shape=[jax.ShapeDtypeStruct((A_HEADS, r, c), F32),
                   jax.ShapeDtypeStruct((A_HEADS, 8, LANES), F32)],
        compiler_params=_cparams(("arbitrary",)),
        name="rel_bias_tiles",
    )(rel_bias, bucket, madd)


def _t5_bucket(rel):
    half = REL_BUCKETS // 2
    exact = half // 2
    ret = jnp.where(rel > 0, half, 0)
    n = jnp.abs(rel)
    nf = jnp.maximum(n, 1).astype(jnp.float32)
    large = exact + (jnp.log(nf / exact) / math.log(REL_MAX_DIST / exact) * (half - exact)).astype(jnp.int32)
    large = jnp.minimum(large, half - 1)
    return ret + jnp.where(n < exact, n, large)


def _tile_bucket_mask(q_pos, k_pos):
    q_pos = jnp.asarray(q_pos, jnp.int32)
    k_pos = jnp.asarray(k_pos, jnp.int32)
    bucket = _t5_bucket(k_pos[None, :] - q_pos[:, None]).astype(jnp.int32)
    visible = (k_pos[None, :] // CHUNK) <= (q_pos[:, None] // CHUNK)
    return bucket, jnp.where(visible, 0.0, NEG_BIG).astype(F32)


def _premix_kernel(x_ref, mod_ref, gmix_ref, w1_ref, cs_ref, sn_ref, csq_ref, gq_ref, wq_ref,
                   gkv_ref, wkv_ref, wp_ref, ps_ref, hist_ref,
                   aq_ref, ak_ref, av_ref, akb_ref, avb_ref, ub_ref, bo_ref, lat_ref, kr_ref,
                   cq_ref, ck_ref, cv_ref, carry_ref, *, tm, pos0):
    t = pl.program_id(1)
    x = x_ref[0]
    sh1 = mod_ref[0, 0:1, :]
    sc1 = mod_ref[0, 1:2, :]
    h = (_rms(x) * gmix_ref[...]) * (1.0 + sc1) + sh1
    u = jnp.dot(h.astype(BF16), w1_ref[...], preferred_element_type=F32)

    aq_ref[0] = (u[:, 0:A_WIDTH] * (A_QK ** -0.5 * LOG2E)).astype(BF16)
    ak = u[:, A_WIDTH:2 * A_WIDTH]
    av = u[:, 2 * A_WIDTH:3 * A_WIDTH]
    ak_ref[0] = ak
    av_ref[0] = av
    akb_ref[0] = ak.astype(BF16)
    avb_ref[0] = av.astype(BF16)

    ub = u[:, OFF_B:OFF_CQ]
    ub_ref[0] = ub

    @pl.when(t == 0)
    def _():
        carry_ref[...] = hist_ref[0]

    ext = jnp.concatenate([carry_ref[...], ub], axis=0)
    carry_ref[...] = ub[tm - 16:, :]
    s2 = ext + pltpu.roll(ext, 1, 0)
    s4 = s2 + pltpu.roll(s2, 2, 0)
    s8 = s4 + pltpu.roll(s4, 4, 0)
    s16 = s8 + pltpu.roll(s8, 8, 0)
    lane = lax.broadcasted_iota(jnp.int32, (tm, B_WIDTH), 1)
    grp = lane // B_CH
    tot = jnp.where(grp == 0, s2[16:], jnp.where(grp == 1, s4[16:], jnp.where(grp == 2, s8[16:], s16[16:])))
    win = jnp.where(grp == 0, 2, jnp.where(grp == 1, 4, jnp.where(grp == 2, 8, 16)))
    pos = pos0 + t * tm + lax.broadcasted_iota(jnp.int32, (tm, B_WIDTH), 0)
    cnt = jnp.minimum(pos + 1, win).astype(F32)
    m = tot / cnt - ub
    y = jnp.dot(m.astype(BF16), wp_ref[...], preferred_element_type=F32) * ps_ref[...]
    bo_ref[0] = y.astype(BF16)

    cs = cs_ref[...]
    sn = sn_ref[...]
    csq = csq_ref[...]
    qn = (_rms(u[:, OFF_CQ:OFF_CKV]) * gq_ref[...]).astype(BF16)
    q2 = jnp.dot(qn, wq_ref[...], preferred_element_type=F32)
    qscale = (C_NOPE + C_ROPE) ** -0.5 * LOG2E
    lat = _rms(u[:, OFF_CKV:OFF_CKR]) * gkv_ref[...]
    lat_ref[0] = lat
    krp = u[:, OFF_CKR:OFF_CKR + LANES] * cs + u[:, OFF_CKR + LANES:N1] * sn
    kr_ref[0] = krp[:, 0:C_ROPE]
    kv = jnp.dot(lat.astype(BF16), wkv_ref[...], preferred_element_type=F32)
    for hh in range(C_HEADS):
        sl = slice(hh * LANES, (hh + 1) * LANES)
        sr = slice((C_HEADS + hh) * LANES, (C_HEADS + hh + 1) * LANES)
        cq_ref[0, :, sl] = ((q2[:, sl] * csq + q2[:, sr] * sn) * qscale).astype(BF16)
        ck_ref[0, :, sl] = (kv[:, sl] + krp).astype(BF16)
    cv_ref[0] = kv[:, C_HEADS * LANES:].astype(BF16)


def _premix(x, mod, gmix, w1, cs, sn, csq, gq, wq, gkv, wkv, wp, ps, hist16, pos0):
    b, t, d = x.shape
    tm = min(512, t)
    assert t % tm == 0 and tm >= 16
    full = lambda shape: pl.BlockSpec(shape, lambda i, j: (0,) * len(shape))
    tile = lambda n: pl.BlockSpec((1, tm, n), lambda i, j: (i, j, 0))
    tab = pl.BlockSpec((tm, LANES), lambda i, j: (j, 0))
    outs = [(A_WIDTH, BF16), (A_WIDTH, F32), (A_WIDTH, F32), (A_WIDTH, BF16), (A_WIDTH, BF16),
            (B_WIDTH, F32), (B_WIDTH, BF16), (C_KV_RANK, F32), (C_ROPE, F32),
            (C_HEADS * LANES, BF16), (C_HEADS * LANES, BF16), (C_WIDTH, BF16)]
    return pl.pallas_call(
        functools.partial(_premix_kernel, tm=tm, pos0=pos0),
        grid=(b, t // tm),
        in_specs=[tile(d),
                  pl.BlockSpec((1, 6, d), lambda i, j: (i, 0, 0)),
                  full((1, d)), full(w1.shape), tab, tab, tab,
                  full((1, C_Q_RANK)), full(wq.shape), full((1, C_KV_RANK)), full(wkv.shape),
                  full(wp.shape), full((1, B_WIDTH)),
                  pl.BlockSpec((1, 16, B_WIDTH), lambda i, j: (i, 0, 0))],
        out_specs=[tile(n) for n, _ in outs],
        out_shape=[jax.ShapeDtypeStruct((b, t, n), dt) for n, dt in outs],
        scratch_shapes=[pltpu.VMEM((16, B_WIDTH), F32)],
        compiler_params=_cparams(("arbitrary", "arbitrary")),
        name="premix",
    )(x, mod, gmix, w1, cs, sn, csq, gq, wq, gkv, wkv, wp, ps, hist16)


def _kvpast_kernel(lat_ref, kr_ref, wkv_ref, e_ref, ck_ref, cv_ref):
    kv = jnp.dot(lat_ref[0].astype(BF16), wkv_ref[...], preferred_element_type=F32)
    krp = jnp.dot(kr_ref[0].astype(BF16), e_ref[...], preferred_element_type=F32)
    for hh in range(C_HEADS):
        sl = slice(hh * LANES, (hh + 1) * LANES)
        ck_ref[0, :, sl] = (kv[:, sl] + krp).astype(BF16)
    cv_ref[0] = kv[:, C_HEADS * LANES:].astype(BF16)


def _kvpast(lat, kr, wkv):
    b, p, _ = lat.shape
    tm = min(512, p)
    assert p % tm == 0
    e = jnp.eye(C_ROPE, LANES, dtype=BF16)
    return pl.pallas_call(
        _kvpast_kernel,
        grid=(b, p // tm),
        in_specs=[pl.BlockSpec((1, tm, C_KV_RANK), lambda i, j: (i, j, 0)),
                  pl.BlockSpec((1, tm, C_ROPE), lambda i, j: (i, j, 0)),
                  pl.BlockSpec(wkv.shape, lambda i, j: (0, 0)),
                  pl.BlockSpec(e.shape, lambda i, j: (0, 0))],
        out_specs=[pl.BlockSpec((1, tm, C_HEADS * LANES), lambda i, j: (i, j, 0)),
                   pl.BlockSpec((1, tm, C_WIDTH), lambda i, j: (i, j, 0))],
        out_shape=[jax.ShapeDtypeStruct((b, p, C_HEADS * LANES), BF16),
                   jax.ShapeDtypeStruct((b, p, C_WIDTH), BF16)],
        compiler_params=_cparams(("arbitrary", "arbitrary")),
        name="mla_kv_past",
    )(lat, kr, wkv, e)


def _online_update(carry, s, v):
    m, l, acc = carry
    m_new = jnp.maximum(m, jnp.max(s, axis=-1, keepdims=True))
    alpha = jnp.exp2(m - m_new)
    p = jnp.exp2(s - m_new)
    l = alpha * l + jnp.sum(p, axis=-1, keepdims=True)
    acc = alpha * acc + jnp.dot(p.astype(BF16), v, preferred_element_type=F32)
    return m_new, l, acc


def _qk(q, k):
    return lax.dot_general(q, k, (((1,), (1,)), ((), ())), preferred_element_type=F32)


def _attn_kernel(*refs, diff, sub_bias, diag_bias, bq, bk, n_all, lam_init):
    it = iter(refs)
    lamp_ref = next(it) if diff else None
    g_ref = next(it) if diff else None
    q_ref, ka_ref, va_ref, kd_ref, vd_ref = (next(it) for _ in range(5))
    bs_ref = next(it) if sub_bias else None
    bd_ref = next(it) if diag_bias else None
    bmax_ref = next(it) if diff else None
    o_ref = next(it)
    vaug_ref, kmax_ref, acc_ref = (next(it) for _ in range(3))

    qi = pl.program_id(2)
    t_all = ka_ref.shape[1]
    bkd = kd_ref.shape[1]
    na = qi if n_all is None else n_all
    lane = lax.broadcasted_iota(jnp.int32, (bq, LANES), 1)
    q = q_ref[0]
    zero = jnp.zeros_like(q)
    ones_sq = jnp.ones((LANES, LANES), BF16)

    def vmask(v, n):
        ln = lax.broadcasted_iota(jnp.int32, (n, LANES), 1)
        return (ln >= A_QK * v) & (ln < A_QK * (v + 1))

    if diff:
        qs = jnp.concatenate([jnp.where(vmask(v, bq), q, zero) for v in range(4)], axis=0)
        groups = [(qs, 0, [0, 0, 1, 1])]
    else:
        groups = [(q[:, 0:LANES], 0, [0]), (q[:, LANES:2 * LANES], 1, [0])]
    n_var = sum(len(bi) for _, _, bi in groups)

    def key_norm2(k, var):
        kk = k.astype(F32)
        kk = kk * kk
        if diff:
            kk = jnp.where(vmask(var, k.shape[0]), kk, 0.0)
        n2 = jnp.dot(kk.astype(BF16), ones_sq, preferred_element_type=F32)
        return jnp.max(n2, axis=0, keepdims=True)

    def var_keys(k, var):
        return k if diff else k[:, var * LANES:(var + 1) * LANES]

    @pl.when(qi == 0)
    def _():
        vaug_ref[:, 0:LANES] = va_ref[0]
        vaug_ref[:, LANES:] = jnp.ones((t_all, LANES), BF16)
        rows = min(t_all, 1024)
        for var in range(n_var):
            km = jnp.zeros((1, LANES), F32)
            for c in range(t_all // rows):
                km = jnp.maximum(km, key_norm2(var_keys(ka_ref[0, c * rows:(c + 1) * rows, :], var), var))
            kmax_ref[var * 8:(var + 1) * 8, :] = jnp.broadcast_to(km, (8, LANES))

    def stack_bias(ref, bias_idx):
        tiles = [ref[0] if ref.shape[0] == 1 else ref[i] for i in bias_idx]
        return tiles[0] if len(tiles) == 1 else jnp.concatenate(tiles, axis=0)

    kd = kd_ref[0]
    vd = vd_ref[0]
    bias_d = [stack_bias(bd_ref, bi) if diag_bias else None for _, _, bi in groups]
    if sub_bias:
        off_s = pl.multiple_of(jnp.maximum(na - 1, 0) * bk, bk)
        gate = jnp.where(na >= 1, 0.0, NEG_BIG)
        bias_s = [stack_bias(bs_ref, bi) + gate for _, _, bi in groups]
        n_far = jnp.maximum(na - 1, 0)
    else:
        n_far = na

    refs_g = []
    var = 0
    for qg, kcol, bi in groups:
        qf = qg.astype(F32)
        qn2 = jnp.dot((qf * qf).astype(BF16), ones_sq, preferred_element_type=F32)
        parts = []
        for i, b_i in enumerate(bi):
            km2 = kmax_ref[var * 8:var * 8 + 1, :]
            if n_all is not None:
                km2 = jnp.maximum(km2, key_norm2(var_keys(kd, var), var))
            r = jnp.sqrt(qn2[i * bq:(i + 1) * bq] * km2) * (1.0 + 2.0 ** -6)
            if diff:
                r = r + bmax_ref[b_i, 0:1, :]
            parts.append(r)
            var += 1
        refs_g.append(parts[0] if len(parts) == 1 else jnp.concatenate(parts, axis=0))

    def probs(k, biases):
        out = []
        for (qg, kcol, _), mref, bias in zip(groups, refs_g, biases):
            s = _qk(qg, k[:, kcol * LANES:(kcol + 1) * LANES])
            if bias is not None:
                s = s + bias
            n = s.shape[1]
            if n >= LANES:
                cols = [jnp.exp2(s[:, c:c + LANES] - mref) for c in range(0, n, LANES)]
            else:
                cols = [jnp.exp2(s - mref[:, :n])]
            out.append((cols[0] if len(cols) == 1 else jnp.concatenate(cols, axis=1)).astype(BF16))
        return out[0] if len(out) == 1 else jnp.concatenate(out, axis=0)

    vd_aug = jnp.concatenate([vd, jnp.ones((bkd, LANES), BF16)], axis=1)
    acc_ref[...] = jnp.dot(probs(kd, bias_d), vd_aug, preferred_element_type=F32)
    if sub_bias:
        acc_ref[...] += jnp.dot(probs(ka_ref[0, pl.ds(off_s, bk), :], bias_s), vaug_ref[pl.ds(off_s, bk), :],
                                preferred_element_type=F32)
    wide = 4 * bk
    none_bias = [None] * len(groups)

    def far_body(width, base):
        def body(j, c):
            off = pl.multiple_of(base + j * width, bk)
            acc_ref[...] += jnp.dot(probs(ka_ref[0, pl.ds(off, width), :], none_bias),
                                    vaug_ref[pl.ds(off, width), :], preferred_element_type=F32)
            return c
        return body

    n_wide = n_far // 4
    lax.fori_loop(0, n_wide, far_body(wide, 0), 0)
    lax.fori_loop(0, n_far - 4 * n_wide, far_body(bk, n_wide * wide), 0)

    def finish(results):
        if diff:
            lp = lamp_ref[...]
            lam = (jnp.exp(jnp.sum(lp[0:1] * lp[1:2], axis=-1, keepdims=True))
                   - jnp.exp(jnp.sum(lp[2:3] * lp[3:4], axis=-1, keepdims=True)) + lam_init)
            outs = []
            for e in range(2):
                d = results[2 * e] - lam * results[2 * e + 1]
                own = (lane >= A_V * e) & (lane < A_V * (e + 1))
                ms = jnp.sum(jnp.where(own, d * d, 0.0), axis=-1, keepdims=True) * (1.0 / A_V)
                outs.append(d * lax.rsqrt(ms + EPS) * g_ref[...] * (1.0 - lam_init))
            out = jnp.where(lane < A_V, outs[0], outs[1])
        else:
            out = jnp.where(lane < C_V, results[0], results[1])
        o_ref[0] = out.astype(o_ref.dtype)

    acc = acc_ref[...]
    den = acc[:, LANES:]
    safe = (jnp.min(den) >= 2.0 ** -80) & (jnp.max(den) <= 2.0 ** 100)
    o_fast = acc[:, 0:LANES] / den
    finish([o_fast[i * bq:(i + 1) * bq] for i in range(n_var)])

    @pl.when(jnp.logical_not(safe))
    def _():
        def step(carries, k, v, biases):
            out = []
            for (qg, kcol, _), c, bias in zip(groups, carries, biases):
                s = _qk(qg, k[:, kcol * LANES:(kcol + 1) * LANES])
                out.append(_online_update(c, s if bias is None else s + bias, v))
            return tuple(out)

        carries = tuple((jnp.full((len(bi) * bq, 1), NEG_BIG, F32), jnp.zeros((len(bi) * bq, 1), F32),
                         jnp.zeros((len(bi) * bq, LANES), F32)) for _, _, bi in groups)
        carries = step(carries, kd, vd, bias_d)
        if sub_bias:
            carries = step(carries, ka_ref[0, pl.ds(off_s, bk), :], va_ref[0, pl.ds(off_s, bk), :], bias_s)

        def body(j, cs):
            off = pl.multiple_of(j * bk, bk)
            return step(cs, ka_ref[0, pl.ds(off, bk), :], va_ref[0, pl.ds(off, bk), :], none_bias)

        carries = lax.fori_loop(0, n_far, body, carries)
        results = []
        for (_, _, bi), (m, l, a) in zip(groups, carries):
            o = a / l
            results.extend(o[i * bq:(i + 1) * bq] for i in range(len(bi)))
        finish(results)


def _attention(q, k_all, v_all, k_new, v_new, bias_sub, bias_diag, *, diff, causal_blocks,
               lamp=None, g2=None, bias_max=None, lam_init=0.0):
    b, tq, _ = q.shape
    t_all = k_all.shape[1]
    bk = ATT_BLOCK
    bq = min(ATT_BLOCK, tq)
    assert tq % bq == 0 and t_all % bk == 0
    if causal_blocks:
        assert bq == bk and t_all == tq
    else:
        assert tq == bq
    qw = LANES if diff else 2 * LANES
    n_pairs = A_HEADS // 2
    in_specs, args = [], []
    if diff:
        in_specs += [pl.BlockSpec(lamp.shape, lambda i, p, j: (0, 0)),
                     pl.BlockSpec(g2.shape, lambda i, p, j: (0, 0))]
        args += [lamp, g2]
    in_specs += [pl.BlockSpec((1, bq, qw), lambda i, p, j: (i, j, p)),
                 pl.BlockSpec((1, t_all, qw), lambda i, p, j: (i, 0, p)),
                 pl.BlockSpec((1, t_all, LANES), lambda i, p, j: (i, 0, p)),
                 pl.BlockSpec((1, bq, qw), lambda i, p, j: (i, j, p)),
                 pl.BlockSpec((1, bq, LANES), lambda i, p, j: (i, j, p))]
    args += [q, k_all, v_all, k_new, v_new]
    for bias in (bias_sub, bias_diag):
        if bias is not None:
            if bias.shape[0] == 1:
                in_specs.append(pl.BlockSpec(bias.shape, lambda i, p, j: (0, 0, 0)))
            else:
                in_specs.append(pl.BlockSpec((2,) + bias.shape[1:], lambda i, p, j: (p, 0, 0)))
            args.append(bias)
    if diff:
        in_specs.append(pl.BlockSpec((2, 8, LANES), lambda i, p, j: (p, 0, 0)))
        args.append(bias_max)
    n_var = 4 if diff else 2
    kern = functools.partial(
        _attn_kernel, diff=diff, sub_bias=bias_sub is not None, diag_bias=bias_diag is not None,
        bq=bq, bk=bk, n_all=None if causal_blocks else t_all // bk, lam_init=lam_init)
    return pl.pallas_call(
        kern,
        grid=(b, n_pairs, tq // bq),
        in_specs=in_specs,
        out_specs=pl.BlockSpec((1, bq, LANES), lambda i, p, j: (i, j, p)),
        out_shape=jax.ShapeDtypeStruct((b, tq, n_pairs * LANES), BF16),
        scratch_shapes=[pltpu.VMEM((t_all, 2 * LANES), BF16),
                        pltpu.VMEM((n_var * 8, LANES), F32),
                        pltpu.VMEM((n_var * bq, 2 * LANES), F32)],
        compiler_params=_cparams(("arbitrary", "arbitrary", "arbitrary")),
        name="diff_attn" if diff else "mla_attn",
    )(*args)


def _postffn_kernel(a_ref, b_ref, c_ref, x_ref, mod_ref, wo_ref, gffn_ref, wg_ref, wv_ref, cw_ref,
                    cb_ref, wd_ref, hist_ref, gfin_ref, y_ref, conv_ref, carry_ref, *, tm, final):
    t = pl.program_id(1)
    cat = jnp.concatenate([a_ref[0], b_ref[0], c_ref[0]], axis=-1)
    mix = jnp.dot(cat, wo_ref[...], preferred_element_type=F32)
    gt1 = mod_ref[0, 2:3, :]
    sh2 = mod_ref[0, 3:4, :]
    sc2 = mod_ref[0, 4:5, :]
    gt2 = mod_ref[0, 5:6, :]
    x1 = x_ref[0] + gt1 * mix
    h2 = ((_rms(x1) * gffn_ref[...]) * (1.0 + sc2) + sh2).astype(BF16)
    g = jnp.dot(h2, wg_ref[...], preferred_element_type=F32)
    val = jnp.dot(h2, wv_ref[...], preferred_element_type=F32)

    @pl.when(t == 0)
    def _():
        carry_ref[...] = hist_ref[0]

    ext = jnp.concatenate([carry_ref[...], g], axis=0)
    tail = g[tm - 8:, :]
    carry_ref[...] = tail
    conv_ref[0] = tail
    gc = (cw_ref[0:1, :] * pltpu.roll(ext, 2, 0)[8:] + cw_ref[1:2, :] * pltpu.roll(ext, 1, 0)[8:]
          + cw_ref[2:3, :] * g + cb_ref[...])
    act = (gc * jax.nn.sigmoid(gc) * val).astype(BF16)
    f = jnp.dot(act, wd_ref[...], preferred_element_type=F32)
    x2 = x1 + gt2 * f
    if final:
        x2 = _rms(x2) * gfin_ref[...]
    y_ref[0] = x2


def _postffn(a, bo, c, x, mod, wo, gffn, wg, wv, cw, cb, wd, hist8, gfin, final):
    b, t, d = x.shape
    f = wg.shape[1]
    tm = min(256, t)
    assert t % tm == 0 and tm >= 8
    tile = lambda n: pl.BlockSpec((1, tm, n), lambda i, j: (i, j, 0))

    def resident(shape):
        return pl.BlockSpec(shape, lambda i, j: (0,) * len(shape), pipeline_mode=pl.Buffered(1))

    return pl.pallas_call(
        functools.partial(_postffn_kernel, tm=tm, final=final),
        grid=(b, t // tm),
        in_specs=[tile(A_WIDTH), tile(B_WIDTH), tile(C_WIDTH), tile(d),
                  pl.BlockSpec((1, 6, d), lambda i, j: (i, 0, 0)),
                  resident(wo.shape), resident((1, d)), resident(wg.shape), resident(wv.shape),
                  resident((CONV_W, f)), resident((1, f)), resident(wd.shape),
                  pl.BlockSpec((1, 8, f), lambda i, j: (i, 0, 0)),
                  resident((1, d))],
        out_specs=[tile(d), pl.BlockSpec((1, 8, f), lambda i, j: (i, 0, 0))],
        out_shape=[jax.ShapeDtypeStruct((b, t, d), F32), jax.ShapeDtypeStruct((b, 8, f), F32)],
        scratch_shapes=[pltpu.VMEM((8, f), F32)],
        compiler_params=_cparams(("arbitrary", "arbitrary")),
        name="postmix_ffn",
    )(a, bo, c, x, mod, wo, gffn, wg, wv, cw, cb, wd, hist8, gfin)


def _rot_cols(w):
    half = C_ROPE // 2
    return jnp.concatenate([-w[..., half:], w[..., :half]], axis=-1)


def _layer_weights(l, w_in, w_q_up, w_kv_up, pool_w, w_out, w_up):
    d = w_in.shape[1]
    wkr = w_in[l][:, OFF_CKR:]
    zpad = jnp.zeros((d, LANES - C_ROPE), F32)
    w1 = jnp.concatenate([w_in[l][:, :OFF_CKR], wkr, zpad, _rot_cols(wkr), zpad], axis=1).astype(BF16)

    wq = w_q_up[l].reshape(C_Q_RANK, C_HEADS, C_NOPE + C_ROPE)
    nope, rope = wq[..., :C_NOPE], wq[..., C_NOPE:]
    zq = jnp.zeros((C_Q_RANK, C_HEADS, LANES - C_NOPE - C_ROPE), F32)
    main = jnp.concatenate([rope, nope, zq], axis=-1).reshape(C_Q_RANK, C_HEADS * LANES)
    rot = jnp.concatenate([_rot_cols(rope), jnp.zeros((C_Q_RANK, C_HEADS, LANES - C_ROPE), F32)],
                          axis=-1).reshape(C_Q_RANK, C_HEADS * LANES)
    wq2 = jnp.concatenate([main, rot], axis=1).astype(BF16)

    wkv = w_kv_up[l].reshape(C_KV_RANK, C_HEADS, C_NOPE + C_V)
    zk = jnp.zeros((C_KV_RANK, C_HEADS, C_ROPE), F32)
    kpart = jnp.concatenate([zk, wkv[..., :C_NOPE], zk], axis=-1).reshape(C_KV_RANK, C_HEADS * LANES)
    vpart = wkv[..., C_NOPE:].reshape(C_KV_RANK, C_WIDTH)
    wkv2 = jnp.concatenate([kpart, vpart], axis=1).astype(BF16)

    wp = jnp.zeros((B_WIDTH, B_WIDTH), F32)
    for g in range(B_GROUPS):
        wp = wp.at[g * B_CH:(g + 1) * B_CH, g * B_CH:(g + 1) * B_CH].set(pool_w[l, g])
    ffn = w_up.shape[2] // 2
    return dict(w1=w1, wq=wq2, wkv=wkv2, wp=wp.astype(BF16), wo=w_out[l].astype(BF16),
                wg=w_up[l][:, :ffn].astype(BF16), wv=w_up[l][:, ffn:].astype(BF16))


def _rope_tables(pos):
    half = C_ROPE // 2
    inv = 1.0 / (ROPE_BASE ** (jnp.arange(half, dtype=jnp.float32) / half))
    ang = pos.astype(jnp.float32)[:, None] * inv[None, :]
    cos, sin = jnp.cos(ang), jnp.sin(ang)
    n = pos.shape[0]
    z = jnp.zeros((n, LANES - C_ROPE), F32)
    cs = jnp.concatenate([cos, cos, z], axis=1)
    sn = jnp.concatenate([sin, sin, z], axis=1)
    csq = jnp.concatenate([cos, cos, jnp.ones((n, C_NOPE), F32), jnp.zeros((n, LANES - C_ROPE - C_NOPE), F32)], axis=1)
    return cs, sn, csq


def kernel(x_prompt, x_sample, c_prompt, c_sample, cache_a_k, cache_a_v, cache_c_latent, cache_c_krope,
           state_b_pool, state_ffn_conv, w_ada, b_ada, g_mix, w_in, lam_q1, lam_k1, lam_q2, lam_k2,
           a_subln_g, rel_bias, pool_w, pool_scale, c_q_norm_g, w_q_up, c_kv_norm_g, w_kv_up, w_out,
           g_ffn, w_up, conv_w, conv_b, w_down, g_final):
    depth = w_in.shape[0]
    bp, tp, d = x_prompt.shape
    bs, ts, _ = x_sample.shape
    past = cache_a_k.shape[2]
    ffn = conv_w.shape[2]
    blk = ATT_BLOCK

    mod_all = _ada(jnp.concatenate([c_prompt, c_sample], axis=0), w_ada, b_ada)
    mod_all = mod_all.reshape(depth, bp + bs, 6, d)

    r = np.arange(blk)
    bkt, madd = _tile_bucket_mask(blk + r, r)
    pb_sub, bias_max = _bias_tiles(rel_bias, bkt, madd)
    bkt, madd = _tile_bucket_mask(r, r)
    pb_diag, _ = _bias_tiles(rel_bias, bkt, madd)
    mla_diag = madd[None]
    rs = np.arange(ts)
    bkt, madd = _tile_bucket_mask(past + rs, past - blk + r)
    sb_sub, _ = _bias_tiles(rel_bias, bkt, madd)
    bkt, madd = _tile_bucket_mask(past + rs, past + rs)
    sb_diag, _ = _bias_tiles(rel_bias, bkt, madd)
    mla_sdiag = madd[None]

    tabs_p = _rope_tables(jnp.arange(tp, dtype=jnp.int32))
    tabs_s = _rope_tables(past + jnp.arange(ts, dtype=jnp.int32))
    zero_pool = jnp.zeros((bp, 16, B_WIDTH), F32)
    zero_conv = jnp.zeros((bp, 8, ffn), F32)

    hp, hs = x_prompt, x_sample
    new_p, new_s = [], []
    for l in range(depth):
        w = _layer_weights(l, w_in, w_q_up, w_kv_up, pool_w, w_out, w_up)
        lam_init = 0.8 - 0.6 * math.exp(-0.3 * l)
        lamp = jnp.stack([lam_q1[l], lam_k1[l], lam_q2[l], lam_k2[l]], axis=0)
        g2 = jnp.concatenate([a_subln_g[l], a_subln_g[l]])[None]
        wd = w_down[l].astype(BF16)
        last = l == depth - 1

        def run(x, mod, tabs, pos0, hist16, hist8, pasts):
            (aq, ak, av, akb, avb, ub, bo, lat, kr, cq, ck, cv) = _premix(
                x, mod, g_mix[l][None], w["w1"], *tabs, c_q_norm_g[l][None], w["wq"],
                c_kv_norm_g[l][None], w["wkv"], w["wp"], pool_scale[l][None], hist16, pos0)
            if pasts is None:
                a_out = _attention(aq, akb, avb, akb, avb, pb_sub, pb_diag, diff=True, causal_blocks=True,
                                   lamp=lamp, g2=g2, bias_max=bias_max, lam_init=lam_init)
                c_out = _attention(cq, ck, cv, ck, cv, None, mla_diag, diff=False, causal_blocks=True)
            else:
                pk, pv, plat, pkr = pasts
                pkb = pk.reshape(pk.shape[0], past, A_WIDTH).astype(BF16)
                pvb = pv.reshape(pv.shape[0], past, A_WIDTH).astype(BF16)
                a_out = _attention(aq, pkb, pvb, akb, avb, sb_sub, sb_diag, diff=True, causal_blocks=False,
                                   lamp=lamp, g2=g2, bias_max=bias_max, lam_init=lam_init)
                ckp, cvp = _kvpast(plat, pkr, w["wkv"])
                c_out = _attention(cq, ckp, cvp, ck, cv, None, mla_sdiag, diff=False, causal_blocks=False)
            y, conv8 = _postffn(a_out, bo, c_out, x, mod, w["wo"], g_ffn[l][None], w["wg"], w["wv"],
                                conv_w[l], conv_b[l][None], wd, hist8, g_final[None], last)
            bsz, tt = x.shape[0], x.shape[1]
            state = (ak.reshape(bsz, tt, A_HEADS, 2 * A_QK), av.reshape(bsz, tt, A_HEADS, A_V), lat, kr,
                     ub[:, tt - POOL_HIST:], conv8[:, 8 - (CONV_W - 1):])
            return y, state

        hp, st = run(hp, mod_all[l, :bp], tabs_p, 0, zero_pool, zero_conv, None)
        new_p.append(st)
        hist16 = jnp.pad(state_b_pool[l], ((0, 0), (1, 0), (0, 0)))
        hist8 = jnp.pad(state_ffn_conv[l], ((0, 0), (8 - (CONV_W - 1), 0), (0, 0)))
        hs, st = run(hs, mod_all[l, bp:], tabs_s, past, hist16, hist8,
                     (cache_a_k[l], cache_a_v[l], cache_c_latent[l], cache_c_krope[l]))
        new_s.append(st)

    outs_p = [jnp.stack(z, axis=0) for z in zip(*new_p)]
    outs_s = [jnp.stack(z, axis=0) for z in zip(*new_s)]
    return (hp, hs, *outs_p, *outs_s)
```

```python
import functools
import math

import numpy as np
import jax
import jax.numpy as jnp
from jax import lax
from jax.experimental import pallas as pl
from jax.experimental.pallas import tpu as pltpu

F32 = jnp.float32
BF16 = jnp.bfloat16

CHUNK = 64
A_HEADS = 6
A_QK = 32
A_V = 64
B_GROUPS = 4
B_CH = 64
POOL_WINDOWS = (2, 4, 8, 16)
POOL_HIST = 15
C_HEADS = 6
C_NOPE = 64
C_ROPE = 32
C_V = 64
C_Q_RANK = 256
C_KV_RANK = 128
ROPE_BASE = 10000.0
REL_BUCKETS = 32
REL_MAX_DIST = 128
CONV_W = 3
EPS = 1e-6

A_WIDTH = A_HEADS * A_V
B_WIDTH = B_GROUPS * B_CH
C_WIDTH = C_HEADS * C_V
OFF_B = 3 * A_WIDTH
OFF_CQ = OFF_B + B_WIDTH
OFF_CKV = OFF_CQ + C_Q_RANK
OFF_CKR = OFF_CKV + C_KV_RANK

LANES = 128
ATT_BLOCK = 256
FFN_CHUNK = 1024
LOG2E = math.log2(math.e)
NEG_BIG = -1e30
VMEM_LIMIT = 56 * 1024 * 1024

N1 = OFF_CKR + 2 * LANES


def _cparams(sem):
    return pltpu.CompilerParams(dimension_semantics=sem, vmem_limit_bytes=VMEM_LIMIT)


def _rms(x):
    return x * lax.rsqrt(jnp.mean(x * x, axis=-1, keepdims=True) + EPS)


def _ada_kernel(c_ref, w_ref, b_ref, o_ref):
    c = c_ref[...]
    s = (c * jax.nn.sigmoid(c)).astype(BF16)
    o_ref[0] = jnp.dot(s, w_ref[0].astype(BF16), preferred_element_type=F32) + b_ref[0]


def _ada(c_all, w_ada, b_ada):
    depth, d, n = w_ada.shape
    nb = c_all.shape[0]
    tn = 1024
    return pl.pallas_call(
        _ada_kernel,
        grid=(depth, n // tn),
        in_specs=[pl.BlockSpec((nb, d), lambda l, j: (0, 0)),
                  pl.BlockSpec((1, d, tn), lambda l, j: (l, 0, j)),
                  pl.BlockSpec((1, 1, tn), lambda l, j: (l, 0, j))],
        out_specs=pl.BlockSpec((1, nb, tn), lambda l, j: (l, 0, j)),
        out_shape=jax.ShapeDtypeStruct((depth, nb, n), F32),
        compiler_params=_cparams(("arbitrary", "arbitrary")),
        name="ada_mod",
    )(c_all, w_ada, b_ada.reshape(depth, 1, n))


def _bias_kernel(rb_ref, bucket_ref, madd_ref, o_ref, mx_ref):
    h = pl.program_id(0)
    bucket = bucket_ref[...]
    far = rb_ref[REL_BUCKETS // 2 - 1, h]
    val = jnp.zeros(bucket.shape, F32)
    top = far - far
    for b in range(REL_BUCKETS):
        val = jnp.where(bucket == b, rb_ref[b, h] - far, val)
        top = jnp.maximum(top, rb_ref[b, h] - far)
    o_ref[0] = val * LOG2E + madd_ref[...]
    mx_ref[0] = jnp.full(mx_ref.shape[1:], top * LOG2E, F32)


def _bias_tiles(rel_bias, bucket, madd):
    r, c = bucket.shape
    return pl.pallas_call(
        _bias_kernel,
        grid=(A_HEADS,),
        in_specs=[pl.BlockSpec(memory_space=pltpu.SMEM),
                  pl.BlockSpec((r, c), lambda h: (0, 0)),
                  pl.BlockSpec((r, c), lambda h: (0, 0))],
        out_specs=[pl.BlockSpec((1, r, c), lambda h: (h, 0, 0)),
                   pl.BlockSpec((1, 8, LANES), lambda h: (h, 0, 0))],
        out_shape=[jax.ShapeDtypeStruct((A_HEADS, r, c), F32),
                   jax.ShapeDtypeStruct((A_HEADS, 8, LANES), F32)],
        compiler_params=_cparams(("arbitrary",)),
        name="rel_bias_tiles",
    )(rel_bias, bucket, madd)


def _t5_bucket(rel):
    half = REL_BUCKETS // 2
    exact = half // 2
    ret = jnp.where(rel > 0, half, 0)
    n = jnp.abs(rel)
    nf = jnp.maximum(n, 1).astype(jnp.float32)
    large = exact + (jnp.log(nf / exact) / math.log(REL_MAX_DIST / exact) * (half - exact)).astype(jnp.int32)
    large = jnp.minimum(large, half - 1)
    return ret + jnp.where(n < exact, n, large)


def _tile_bucket_mask(q_pos, k_pos):
    q_pos = jnp.asarray(q_pos, jnp.int32)
    k_pos = jnp.asarray(k_pos, jnp.int32)
    bucket = _t5_bucket(k_pos[None, :] - q_pos[:, None]).astype(jnp.int32)
    visible = (k_pos[None, :] // CHUNK) <= (q_pos[:, None] // CHUNK)
    return bucket, jnp.where(visible, 0.0, NEG_BIG).astype(F32)


def _premix_kernel(x_ref, mod_ref, gmix_ref, w1_ref, cs_ref, sn_ref, csq_ref, gq_ref, wq_ref,
                   gkv_ref, wkv_ref, wp_ref, ps_ref, hist_ref,
                   aq_ref, ak_ref, av_ref, akb_ref, avb_ref, ub_ref, bo_ref, lat_ref, kr_ref,
                   cq_ref, ck_ref, cv_ref, carry_ref, *, tm, pos0):
    t = pl.program_id(1)
    x = x_ref[0]
    sh1 = mod_ref[0, 0:1, :]
    sc1 = mod_ref[0, 1:2, :]
    h = (_rms(x) * gmix_ref[...]) * (1.0 + sc1) + sh1
    u = jnp.dot(h.astype(BF16), w1_ref[...], preferred_element_type=F32)

    aq_ref[0] = (u[:, 0:A_WIDTH] * (A_QK ** -0.5 * LOG2E)).astype(BF16)
    ak = u[:, A_WIDTH:2 * A_WIDTH]
    av = u[:, 2 * A_WIDTH:3 * A_WIDTH]
    ak_ref[0] = ak
    av_ref[0] = av
    akb_ref[0] = ak.astype(BF16)
    avb_ref[0] = av.astype(BF16)

    ub = u[:, OFF_B:OFF_CQ]
    ub_ref[0] = ub

    @pl.when(t == 0)
    def _():
        carry_ref[...] = hist_ref[0]

    ext = jnp.concatenate([carry_ref[...], ub], axis=0)
    carry_ref[...] = ub[tm - 16:, :]
    s2 = ext + pltpu.roll(ext, 1, 0)
    s4 = s2 + pltpu.roll(s2, 2, 0)
    s8 = s4 + pltpu.roll(s4, 4, 0)
    s16 = s8 + pltpu.roll(s8, 8, 0)
    lane = lax.broadcasted_iota(jnp.int32, (tm, B_WIDTH), 1)
    grp = lane // B_CH
    tot = jnp.where(grp == 0, s2[16:], jnp.where(grp == 1, s4[16:], jnp.where(grp == 2, s8[16:], s16[16:])))
    win = jnp.where(grp == 0, 2, jnp.where(grp == 1, 4, jnp.where(grp == 2, 8, 16)))
    pos = pos0 + t * tm + lax.broadcasted_iota(jnp.int32, (tm, B_WIDTH), 0)
    cnt = jnp.minimum(pos + 1, win).astype(F32)
    m = tot / cnt - ub
    y = jnp.dot(m.astype(BF16), wp_ref[...], preferred_element_type=F32) * ps_ref[...]
    bo_ref[0] = y.astype(BF16)

    cs = cs_ref[...]
    sn = sn_ref[...]
    csq = csq_ref[...]
    qn = (_rms(u[:, OFF_CQ:OFF_CKV]) * gq_ref[...]).astype(BF16)
    q2 = jnp.dot(qn, wq_ref[...], preferred_element_type=F32)
    qscale = (C_NOPE + C_ROPE) ** -0.5 * LOG2E
    lat = _rms(u[:, OFF_CKV:OFF_CKR]) * gkv_ref[...]
    lat_ref[0] = lat
    krp = u[:, OFF_CKR:OFF_CKR + LANES] * cs + u[:, OFF_CKR + LANES:N1] * sn
    kr_ref[0] = krp[:, 0:C_ROPE]
    kv = jnp.dot(lat.astype(BF16), wkv_ref[...], preferred_element_type=F32)
    for hh in range(C_HEADS):
        sl = slice(hh * LANES, (hh + 1) * LANES)
        sr = slice((C_HEADS + hh) * LANES, (C_HEADS + hh + 1) * LANES)
        cq_ref[0, :, sl] = ((q2[:, sl] * csq + q2[:, sr] * sn) * qscale).astype(BF16)
        ck_ref[0, :, sl] = (kv[:, sl] + krp).astype(BF16)
    cv_ref[0] = kv[:, C_HEADS * LANES:].astype(BF16)


def _premix(x, mod, gmix, w1, cs, sn, csq, gq, wq, gkv, wkv, wp, ps, hist16, pos0):
    b, t, d = x.shape
    tm = min(512, t)
    assert t % tm == 0 and tm >= 16
    full = lambda shape: pl.BlockSpec(shape, lambda i, j: (0,) * len(shape))
    tile = lambda n: pl.BlockSpec((1, tm, n), lambda i, j: (i, j, 0))
    tab = pl.BlockSpec((tm, LANES), lambda i, j: (j, 0))
    outs = [(A_WIDTH, BF16), (A_WIDTH, F32), (A_WIDTH, F32), (A_WIDTH, BF16), (A_WIDTH, BF16),
            (B_WIDTH, F32), (B_WIDTH, BF16), (C_KV_RANK, F32), (C_ROPE, F32),
            (C_HEADS * LANES, BF16), (C_HEADS * LANES, BF16), (C_WIDTH, BF16)]
    return pl.pallas_call(
        functools.partial(_premix_kernel, tm=tm, pos0=pos0),
        grid=(b, t // tm),
        in_specs=[tile(d),
                  pl.BlockSpec((1, 6, d), lambda i, j: (i, 0, 0)),
                  full((1, d)), full(w1.shape), tab, tab, tab,
                  full((1, C_Q_RANK)), full(wq.shape), full((1, C_KV_RANK)), full(wkv.shape),
                  full(wp.shape), full((1, B_WIDTH)),
                  pl.BlockSpec((1, 16, B_WIDTH), lambda i, j: (i, 0, 0))],
        out_specs=[tile(n) for n, _ in outs],
        out_shape=[jax.ShapeDtypeStruct((b, t, n), dt) for n, dt in outs],
        scratch_shapes=[pltpu.VMEM((16, B_WIDTH), F32)],
        compiler_params=_cparams(("arbitrary", "arbitrary")),
        name="premix",
    )(x, mod, gmix, w1, cs, sn, csq, gq, wq, gkv, wkv, wp, ps, hist16)


def _kvpast_kernel(lat_ref, kr_ref, wkv_ref, e_ref, ck_ref, cv_ref):
    kv = jnp.dot(lat_ref[0].astype(BF16), wkv_ref[...], preferred_element_type=F32)
    krp = jnp.dot(kr_ref[0].astype(BF16), e_ref[...], preferred_element_type=F32)
    for hh in range(C_HEADS):
        sl = slice(hh * LANES, (hh + 1) * LANES)
        ck_ref[0, :, sl] = (kv[:, sl] + krp).astype(BF16)
    cv_ref[0] = kv[:, C_HEADS * LANES:].astype(BF16)


def _kvpast(lat, kr, wkv):
    b, p, _ = lat.shape
    tm = min(512, p)
    assert p % tm == 0
    e = jnp.eye(C_ROPE, LANES, dtype=BF16)
    return pl.pallas_call(
        _kvpast_kernel,
        grid=(b, p // tm),
        in_specs=[pl.BlockSpec((1, tm, C_KV_RANK), lambda i, j: (i, j, 0)),
                  pl.BlockSpec((1, tm, C_ROPE), lambda i, j: (i, j, 0)),
                  pl.BlockSpec(wkv.shape, lambda i, j: (0, 0)),
                  pl.BlockSpec(e.shape, lambda i, j: (0, 0))],
        out_specs=[pl.BlockSpec((1, tm, C_HEADS * LANES), lambda i, j: (i, j, 0)),
                   pl.BlockSpec((1, tm, C_WIDTH), lambda i, j: (i, j, 0))],
        out_shape=[jax.ShapeDtypeStruct((b, p, C_HEADS * LANES), BF16),
                   jax.ShapeDtypeStruct((b, p, C_WIDTH), BF16)],
        compiler_params=_cparams(("arbitrary", "arbitrary")),
        name="mla_kv_past",
    )(lat, kr, wkv, e)


def _online_update(carry, s, v):
    m, l, acc = carry
    m_new = jnp.maximum(m, jnp.max(s, axis=-1, keepdims=True))
    alpha = jnp.exp2(m - m_new)
    p = jnp.exp2(s - m_new)
    l = alpha * l + jnp.sum(p, axis=-1, keepdims=True)
    acc = alpha * acc + jnp.dot(p.astype(BF16), v, preferred_element_type=F32)
    return m_new, l, acc


def _qk(q, k):
    return lax.dot_general(q, k, (((1,), (1,)), ((), ())), preferred_element_type=F32)


def _attn_kernel(*refs, diff, sub_bias, diag_bias, bq, bk, n_all, lam_init):
    it = iter(refs)
    lamp_ref = next(it) if diff else None
    g_ref = next(it) if diff else None
    q_ref, ka_ref, va_ref, kd_ref, vd_ref = (next(it) for _ in range(5))
    bs_ref = next(it) if sub_bias else None
    bd_ref = next(it) if diag_bias else None
    bmax_ref = next(it) if diff else None
    o_ref = next(it)
    vaug_ref, kmax_ref, acc_ref = (next(it) for _ in range(3))

    qi = pl.program_id(2)
    t_all = ka_ref.shape[1]
    bkd = kd_ref.shape[1]
    na = qi if n_all is None else n_all
    lane = lax.broadcasted_iota(jnp.int32, (bq, LANES), 1)
    q = q_ref[0]
    zero = jnp.zeros_like(q)
    ones_sq = jnp.ones((LANES, LANES), BF16)

    def vmask(v, n):
        ln = lax.broadcasted_iota(jnp.int32, (n, LANES), 1)
        return (ln >= A_QK * v) & (ln < A_QK * (v + 1))

    if diff:
        qs = jnp.concatenate([jnp.where(vmask(v, bq), q, zero) for v in range(4)], axis=0)
        groups = [(qs, 0, [0, 0, 1, 1])]
    else:
        groups = [(q[:, 0:LANES], 0, [0]), (q[:, LANES:2 * LANES], 1, [0])]
    n_var = sum(len(bi) for _, _, bi in groups)

    def key_norm2(k, var):
        kk = k.astype(F32)
        kk = kk * kk
        if diff:
            kk = jnp.where(vmask(var, k.shape[0]), kk, 0.0)
        n2 = jnp.dot(kk.astype(BF16), ones_sq, preferred_element_type=F32)
        return jnp.max(n2, axis=0, keepdims=True)

    def var_keys(k, var):
        return k if diff else k[:, var * LANES:(var + 1) * LANES]

    @pl.when(qi == 0)
    def _():
        vaug_ref[:, 0:LANES] = va_ref[0]
        vaug_ref[:, LANES:] = jnp.ones((t_all, LANES), BF16)
        rows = min(t_all, 1024)
        for var in range(n_var):
            km = jnp.zeros((1, LANES), F32)
            for c in range(t_all // rows):
                km = jnp.maximum(km, key_norm2(var_keys(ka_ref[0, c * rows:(c + 1) * rows, :], var), var))
            kmax_ref[var * 8:(var + 1) * 8, :] = jnp.broadcast_to(km, (8, LANES))

    def stack_bias(ref, bias_idx):
        tiles = [ref[0] if ref.shape[0] == 1 else ref[i] for i in bias_idx]
        return tiles[0] if len(tiles) == 1 else jnp.concatenate(tiles, axis=0)

    kd = kd_ref[0]
    vd = vd_ref[0]
    bias_d = [stack_bias(bd_ref, bi) if diag_bias else None for _, _, bi in groups]
    merged = n_all is None
    has_sub = sub_bias or merged
    if has_sub:
        n_far = jnp.maximum(na - 1, 0) if merged else max(na - 1, 0)
        off_s = pl.multiple_of(n_far * bk, bk) if merged else n_far * bk
        gate = jnp.where(na >= 1, 0.0, NEG_BIG)
        bias_s = [(stack_bias(bs_ref, bi) + gate) if sub_bias else jnp.full((len(bi) * bq, bk), gate, F32)
                  for _, _, bi in groups]
    else:
        n_far = na

    refs_g = []
    var = 0
    for qg, kcol, bi in groups:
        qf = qg.astype(F32)
        qn2 = jnp.dot((qf * qf).astype(BF16), ones_sq, preferred_element_type=F32)
        parts = []
        for i, b_i in enumerate(bi):
            km2 = kmax_ref[var * 8:var * 8 + 1, :]
            if n_all is not None:
                km2 = jnp.maximum(km2, key_norm2(var_keys(kd, var), var))
            r = jnp.sqrt(qn2[i * bq:(i + 1) * bq] * km2) * (1.0 + 2.0 ** -6)
            if diff:
                r = r + bmax_ref[b_i, 0:1, :]
            parts.append(r)
            var += 1
        refs_g.append(parts[0] if len(parts) == 1 else jnp.concatenate(parts, axis=0))

    def probs(k, biases):
        out = []
        for (qg, kcol, _), mref, bias in zip(groups, refs_g, biases):
            s = _qk(qg, k[:, kcol * LANES:(kcol + 1) * LANES])
            if bias is not None:
                s = s + bias
            n = s.shape[1]
            if n >= LANES:
                cols = [jnp.exp2(s[:, c:c + LANES] - mref) for c in range(0, n, LANES)]
            else:
                cols = [jnp.exp2(s - mref[:, :n])]
            out.append((cols[0] if len(cols) == 1 else jnp.concatenate(cols, axis=1)).astype(BF16))
        return out[0] if len(out) == 1 else jnp.concatenate(out, axis=0)

    vd_aug = jnp.concatenate([vd, jnp.ones((bkd, LANES), BF16)], axis=1)
    if merged:
        k_near = jnp.concatenate([ka_ref[0, pl.ds(off_s, bk), :], kd], axis=0)
        v_near = jnp.concatenate([vaug_ref[pl.ds(off_s, bk), :], vd_aug], axis=0)
        bias_near = [jnp.concatenate([b_s, jnp.zeros_like(b_s) if b_d is None else b_d], axis=1)
                     for b_s, b_d in zip(bias_s, bias_d)]
        acc_ref[...] = jnp.dot(probs(k_near, bias_near), v_near, preferred_element_type=F32)
    else:
        acc_ref[...] = jnp.dot(probs(kd, bias_d), vd_aug, preferred_element_type=F32)
        if has_sub:
            acc_ref[...] += jnp.dot(probs(ka_ref[0, pl.ds(off_s, bk), :], bias_s),
                                    vaug_ref[pl.ds(off_s, bk), :], preferred_element_type=F32)
    none_bias = [None] * len(groups)

    def far_step(off, width):
        if not isinstance(off, int):
            off = pl.multiple_of(off, bk)
        acc_ref[...] += jnp.dot(probs(ka_ref[0, pl.ds(off, width), :], none_bias),
                                vaug_ref[pl.ds(off, width), :], preferred_element_type=F32)

    def far_loop(j, c):
        far_step(j * (4 * bk), 4 * bk)
        return c

    n_wide = n_far // 4
    lax.fori_loop(0, n_wide, far_loop, 0)
    rem = n_far - 4 * n_wide
    base = n_wide * (4 * bk)
    if isinstance(rem, int):
        if rem & 2:
            far_step(base, 2 * bk)
        if rem & 1:
            far_step(base + (rem & 2) * bk, bk)
    else:
        pl.when((rem & 2) != 0)(lambda: far_step(base, 2 * bk))
        pl.when((rem & 1) != 0)(lambda: far_step(base + (rem & 2) * bk, bk))

    def finish(results):
        if diff:
            lp = lamp_ref[...]
            lam = (jnp.exp(jnp.sum(lp[0:1] * lp[1:2], axis=-1, keepdims=True))
                   - jnp.exp(jnp.sum(lp[2:3] * lp[3:4], axis=-1, keepdims=True)) + lam_init)
            outs = []
            for e in range(2):
                d = results[2 * e] - lam * results[2 * e + 1]
                own = (lane >= A_V * e) & (lane < A_V * (e + 1))
                ms = jnp.sum(jnp.where(own, d * d, 0.0), axis=-1, keepdims=True) * (1.0 / A_V)
                outs.append(d * lax.rsqrt(ms + EPS) * g_ref[...] * (1.0 - lam_init))
            out = jnp.where(lane < A_V, outs[0], outs[1])
        else:
            out = jnp.where(lane < C_V, results[0], results[1])
        o_ref[0] = out.astype(o_ref.dtype)

    acc = acc_ref[...]
    den = acc[:, LANES:]
    safe = (jnp.min(den) >= 2.0 ** -80) & (jnp.max(den) <= 2.0 ** 100)
    o_fast = acc[:, 0:LANES] / den
    finish([o_fast[i * bq:(i + 1) * bq] for i in range(n_var)])

    @pl.when(jnp.logical_not(safe))
    def _():
        def step(carries, k, v, biases):
            out = []
            for (qg, kcol, _), c, bias in zip(groups, carries, biases):
                s = _qk(qg, k[:, kcol * LANES:(kcol + 1) * LANES])
                out.append(_online_update(c, s if bias is None else s + bias, v))
            return tuple(out)

        carries = tuple((jnp.full((len(bi) * bq, 1), NEG_BIG, F32), jnp.zeros((len(bi) * bq, 1), F32),
                         jnp.zeros((len(bi) * bq, LANES), F32)) for _, _, bi in groups)
        carries = step(carries, kd, vd, bias_d)
        if has_sub:
            carries = step(carries, ka_ref[0, pl.ds(off_s, bk), :], va_ref[0, pl.ds(off_s, bk), :], bias_s)

        def body(j, cs):
            off = pl.multiple_of(j * bk, bk)
            return step(cs, ka_ref[0, pl.ds(off, bk), :], va_ref[0, pl.ds(off, bk), :], none_bias)

        carries = lax.fori_loop(0, n_far, body, carries)
        results = []
        for (_, _, bi), (m, l, a) in zip(groups, carries):
            o = a / l
            results.extend(o[i * bq:(i + 1) * bq] for i in range(len(bi)))
        finish(results)


def _attention(q, k_all, v_all, k_new, v_new, bias_sub, bias_diag, *, diff, causal_blocks,
               lamp=None, g2=None, bias_max=None, lam_init=0.0):
    b, tq, _ = q.shape
    t_all = k_all.shape[1]
    bk = ATT_BLOCK
    bq = min(ATT_BLOCK, tq)
    assert tq % bq == 0 and t_all % bk == 0
    if causal_blocks:
        assert bq == bk and t_all == tq
    else:
        assert tq == bq
    qw = LANES if diff else 2 * LANES
    n_pairs = A_HEADS // 2
    in_specs, args = [], []
    if diff:
        in_specs += [pl.BlockSpec(lamp.shape, lambda i, p, j: (0, 0)),
                     pl.BlockSpec(g2.shape, lambda i, p, j: (0, 0))]
        args += [lamp, g2]
    in_specs += [pl.BlockSpec((1, bq, qw), lambda i, p, j: (i, j, p)),
                 pl.BlockSpec((1, t_all, qw), lambda i, p, j: (i, 0, p)),
                 pl.BlockSpec((1, t_all, LANES), lambda i, p, j: (i, 0, p)),
                 pl.BlockSpec((1, bq, qw), lambda i, p, j: (i, j, p)),
                 pl.BlockSpec((1, bq, LANES), lambda i, p, j: (i, j, p))]
    args += [q, k_all, v_all, k_new, v_new]
    for bias in (bias_sub, bias_diag):
        if bias is not None:
            if bias.shape[0] == 1:
                in_specs.append(pl.BlockSpec(bias.shape, lambda i, p, j: (0, 0, 0)))
            else:
                in_specs.append(pl.BlockSpec((2,) + bias.shape[1:], lambda i, p, j: (p, 0, 0)))
            args.append(bias)
    if diff:
        in_specs.append(pl.BlockSpec((2, 8, LANES), lambda i, p, j: (p, 0, 0)))
        args.append(bias_max)
    n_var = 4 if diff else 2
    kern = functools.partial(
        _attn_kernel, diff=diff, sub_bias=bias_sub is not None, diag_bias=bias_diag is not None,
        bq=bq, bk=bk, n_all=None if causal_blocks else t_all // bk, lam_init=lam_init)
    return pl.pallas_call(
        kern,
        grid=(b, n_pairs, tq // bq),
        in_specs=in_specs,
        out_specs=pl.BlockSpec((1, bq, LANES), lambda i, p, j: (i, j, p)),
        out_shape=jax.ShapeDtypeStruct((b, tq, n_pairs * LANES), BF16),
        scratch_shapes=[pltpu.VMEM((t_all, 2 * LANES), BF16),
                        pltpu.VMEM((n_var * 8, LANES), F32),
                        pltpu.VMEM((n_var * bq, 2 * LANES), F32)],
        compiler_params=_cparams(("arbitrary", "arbitrary", "arbitrary")),
        name="diff_attn" if diff else "mla_attn",
    )(*args)


def _postffn_kernel(a_ref, b_ref, c_ref, x_ref, mod_ref, wo_ref, gffn_ref, wg_ref, wv_ref, cw_ref,
                    cb_ref, wd_ref, hist_ref, gfin_ref, y_ref, conv_ref, carry_ref, act_ref, *, tm, final):
    t = pl.program_id(1)
    ffn = wg_ref.shape[1]
    cat = jnp.concatenate([a_ref[0], b_ref[0], c_ref[0]], axis=-1)
    mix = jnp.dot(cat, wo_ref[...], preferred_element_type=F32)
    gt1 = mod_ref[0, 2:3, :]
    sh2 = mod_ref[0, 3:4, :]
    sc2 = mod_ref[0, 4:5, :]
    gt2 = mod_ref[0, 5:6, :]
    x1 = x_ref[0] + gt1 * mix
    h2 = ((_rms(x1) * gffn_ref[...]) * (1.0 + sc2) + sh2).astype(BF16)

    @pl.when(t == 0)
    def _():
        carry_ref[...] = hist_ref[0]

    for c0 in range(0, ffn, FFN_CHUNK):
        sl = slice(c0, min(c0 + FFN_CHUNK, ffn))
        g = jnp.dot(h2, wg_ref[:, sl], preferred_element_type=F32)
        val = jnp.dot(h2, wv_ref[:, sl], preferred_element_type=F32)
        ext = jnp.concatenate([carry_ref[:, sl], g], axis=0)
        tail = g[tm - 8:, :]
        carry_ref[:, sl] = tail
        conv_ref[0, :, sl] = tail
        gc = (cw_ref[0:1, sl] * pltpu.roll(ext, 2, 0)[8:] + cw_ref[1:2, sl] * pltpu.roll(ext, 1, 0)[8:]
              + cw_ref[2:3, sl] * g + cb_ref[:, sl])
        act_ref[:, sl] = (gc * jax.nn.sigmoid(gc) * val).astype(BF16)
    f = jnp.dot(act_ref[...], wd_ref[...], preferred_element_type=F32)
    x2 = x1 + gt2 * f
    if final:
        x2 = _rms(x2) * gfin_ref[...]
    y_ref[0] = x2


def _postffn(a, bo, c, x, mod, wo, gffn, wg, wv, cw, cb, wd, hist8, gfin, final):
    b, t, d = x.shape
    f = wg.shape[1]
    tm = min(512, t)
    assert t % tm == 0 and tm >= 8
    tile = lambda n: pl.BlockSpec((1, tm, n), lambda i, j: (i, j, 0))

    def resident(shape):
        return pl.BlockSpec(shape, lambda i, j: (0,) * len(shape), pipeline_mode=pl.Buffered(1))

    return pl.pallas_call(
        functools.partial(_postffn_kernel, tm=tm, final=final),
        grid=(b, t // tm),
        in_specs=[tile(A_WIDTH), tile(B_WIDTH), tile(C_WIDTH), tile(d),
                  pl.BlockSpec((1, 6, d), lambda i, j: (i, 0, 0)),
                  resident(wo.shape), resident((1, d)), resident(wg.shape), resident(wv.shape),
                  resident((CONV_W, f)), resident((1, f)), resident(wd.shape),
                  pl.BlockSpec((1, 8, f), lambda i, j: (i, 0, 0)),
                  resident((1, d))],
        out_specs=[tile(d), pl.BlockSpec((1, 8, f), lambda i, j: (i, 0, 0))],
        out_shape=[jax.ShapeDtypeStruct((b, t, d), F32), jax.ShapeDtypeStruct((b, 8, f), F32)],
        scratch_shapes=[pltpu.VMEM((8, f), F32), pltpu.VMEM((tm, f), BF16)],
        compiler_params=_cparams(("arbitrary", "arbitrary")),
        name="postmix_ffn",
    )(a, bo, c, x, mod, wo, gffn, wg, wv, cw, cb, wd, hist8, gfin)


def _rot_cols(w):
    half = C_ROPE // 2
    return jnp.concatenate([-w[..., half:], w[..., :half]], axis=-1)


def _layer_weights(l, w_in, w_q_up, w_kv_up, pool_w, w_out, w_up):
    d = w_in.shape[1]
    wkr = w_in[l][:, OFF_CKR:]
    zpad = jnp.zeros((d, LANES - C_ROPE), F32)
    w1 = jnp.concatenate([w_in[l][:, :OFF_CKR], wkr, zpad, _rot_cols(wkr), zpad], axis=1).astype(BF16)

    wq = w_q_up[l].reshape(C_Q_RANK, C_HEADS, C_NOPE + C_ROPE)
    nope, rope = wq[..., :C_NOPE], wq[..., C_NOPE:]
    zq = jnp.zeros((C_Q_RANK, C_HEADS, LANES - C_NOPE - C_ROPE), F32)
    main = jnp.concatenate([rope, nope, zq], axis=-1).reshape(C_Q_RANK, C_HEADS * LANES)
    rot = jnp.concatenate([_rot_cols(rope), jnp.zeros((C_Q_RANK, C_HEADS, LANES - C_ROPE), F32)],
                          axis=-1).reshape(C_Q_RANK, C_HEADS * LANES)
    wq2 = jnp.concatenate([main, rot], axis=1).astype(BF16)

    wkv = w_kv_up[l].reshape(C_KV_RANK, C_HEADS, C_NOPE + C_V)
    zk = jnp.zeros((C_KV_RANK, C_HEADS, C_ROPE), F32)
    kpart = jnp.concatenate([zk, wkv[..., :C_NOPE], zk], axis=-1).reshape(C_KV_RANK, C_HEADS * LANES)
    vpart = wkv[..., C_NOPE:].reshape(C_KV_RANK, C_WIDTH)
    wkv2 = jnp.concatenate([kpart, vpart], axis=1).astype(BF16)

    wp = jnp.zeros((B_WIDTH, B_WIDTH), F32)
    for g in range(B_GROUPS):
        wp = wp.at[g * B_CH:(g + 1) * B_CH, g * B_CH:(g + 1) * B_CH].set(pool_w[l, g])
    ffn = w_up.shape[2] // 2
    return dict(w1=w1, wq=wq2, wkv=wkv2, wp=wp.astype(BF16), wo=w_out[l].astype(BF16),
                wg=w_up[l][:, :ffn].astype(BF16), wv=w_up[l][:, ffn:].astype(BF16))


def _rope_tables(pos):
    half = C_ROPE // 2
    inv = 1.0 / (ROPE_BASE ** (jnp.arange(half, dtype=jnp.float32) / half))
    ang = pos.astype(jnp.float32)[:, None] * inv[None, :]
    cos, sin = jnp.cos(ang), jnp.sin(ang)
    n = pos.shape[0]
    z = jnp.zeros((n, LANES - C_ROPE), F32)
    cs = jnp.concatenate([cos, cos, z], axis=1)
    sn = jnp.concatenate([sin, sin, z], axis=1)
    csq = jnp.concatenate([cos, cos, jnp.ones((n, C_NOPE), F32), jnp.zeros((n, LANES - C_ROPE - C_NOPE), F32)], axis=1)
    return cs, sn, csq


def kernel(x_prompt, x_sample, c_prompt, c_sample, cache_a_k, cache_a_v, cache_c_latent, cache_c_krope,
           state_b_pool, state_ffn_conv, w_ada, b_ada, g_mix, w_in, lam_q1, lam_k1, lam_q2, lam_k2,
           a_subln_g, rel_bias, pool_w, pool_scale, c_q_norm_g, w_q_up, c_kv_norm_g, w_kv_up, w_out,
           g_ffn, w_up, conv_w, conv_b, w_down, g_final):
    depth = w_in.shape[0]
    bp, tp, d = x_prompt.shape
    bs, ts, _ = x_sample.shape
    past = cache_a_k.shape[2]
    ffn = conv_w.shape[2]
    blk = ATT_BLOCK

    mod_all = _ada(jnp.concatenate([c_prompt, c_sample], axis=0), w_ada, b_ada)
    mod_all = mod_all.reshape(depth, bp + bs, 6, d)

    r = np.arange(blk)
    bkt, madd = _tile_bucket_mask(blk + r, r)
    pb_sub, bias_max = _bias_tiles(rel_bias, bkt, madd)
    bkt, madd = _tile_bucket_mask(r, r)
    pb_diag, _ = _bias_tiles(rel_bias, bkt, madd)
    mla_diag = madd[None]
    rs = np.arange(ts)
    bkt, madd = _tile_bucket_mask(past + rs, past - blk + r)
    sb_sub, _ = _bias_tiles(rel_bias, bkt, madd)
    bkt, madd = _tile_bucket_mask(past + rs, past + rs)
    sb_diag, _ = _bias_tiles(rel_bias, bkt, madd)
    mla_sdiag = madd[None]

    tabs_p = _rope_tables(jnp.arange(tp, dtype=jnp.int32))
    tabs_s = _rope_tables(past + jnp.arange(ts, dtype=jnp.int32))
    zero_pool = jnp.zeros((bp, 16, B_WIDTH), F32)
    zero_conv = jnp.zeros((bp, 8, ffn), F32)

    hp, hs = x_prompt, x_sample
    new_p, new_s = [], []
    for l in range(depth):
        w = _layer_weights(l, w_in, w_q_up, w_kv_up, pool_w, w_out, w_up)
        lam_init = 0.8 - 0.6 * math.exp(-0.3 * l)
        lamp = jnp.stack([lam_q1[l], lam_k1[l], lam_q2[l], lam_k2[l]], axis=0)
        g2 = jnp.concatenate([a_subln_g[l], a_subln_g[l]])[None]
        wd = w_down[l].astype(BF16)
        last = l == depth - 1

        def run(x, mod, tabs, pos0, hist16, hist8, pasts):
            (aq, ak, av, akb, avb, ub, bo, lat, kr, cq, ck, cv) = _premix(
                x, mod, g_mix[l][None], w["w1"], *tabs, c_q_norm_g[l][None], w["wq"],
                c_kv_norm_g[l][None], w["wkv"], w["wp"], pool_scale[l][None], hist16, pos0)
            if pasts is None:
                a_out = _attention(aq, akb, avb, akb, avb, pb_sub, pb_diag, diff=True, causal_blocks=True,
                                   lamp=lamp, g2=g2, bias_max=bias_max, lam_init=lam_init)
                c_out = _attention(cq, ck, cv, ck, cv, None, mla_diag, diff=False, causal_blocks=True)
            else:
                pk, pv, plat, pkr = pasts
                pkb = pk.reshape(pk.shape[0], past, A_WIDTH).astype(BF16)
                pvb = pv.reshape(pv.shape[0], past, A_WIDTH).astype(BF16)
                a_out = _attention(aq, pkb, pvb, akb, avb, sb_sub, sb_diag, diff=True, causal_blocks=False,
                                   lamp=lamp, g2=g2, bias_max=bias_max, lam_init=lam_init)
                ckp, cvp = _kvpast(plat, pkr, w["wkv"])
                c_out = _attention(cq, ckp, cvp, ck, cv, None, mla_sdiag, diff=False, causal_blocks=False)
            y, conv8 = _postffn(a_out, bo, c_out, x, mod, w["wo"], g_ffn[l][None], w["wg"], w["wv"],
                                conv_w[l], conv_b[l][None], wd, hist8, g_final[None], last)
            bsz, tt = x.shape[0], x.shape[1]
            state = (ak.reshape(bsz, tt, A_HEADS, 2 * A_QK), av.reshape(bsz, tt, A_HEADS, A_V), lat, kr,
                     ub[:, tt - POOL_HIST:], conv8[:, 8 - (CONV_W - 1):])
            return y, state

        hp, st = run(hp, mod_all[l, :bp], tabs_p, 0, zero_pool, zero_conv, None)
        new_p.append(st)
        hist16 = jnp.pad(state_b_pool[l], ((0, 0), (1, 0), (0, 0)))
        hist8 = jnp.pad(state_ffn_conv[l], ((0, 0), (8 - (CONV_W - 1), 0), (0, 0)))
        hs, st = run(hs, mod_all[l, bp:], tabs_s, past, hist16, hist8,
                     (cache_a_k[l], cache_a_v[l], cache_c_latent[l], cache_c_krope[l]))
        new_s.append(st)

    outs_p = [jnp.stack(z, axis=0) for z in zip(*new_p)]
    outs_s = [jnp.stack(z, axis=0) for z in zip(*new_s)]
    return (hp, hs, *outs_p, *outs_s)
```

```python
import functools
import math

import numpy as np
import jax
import jax.numpy as jnp
from jax import lax
from jax.experimental import pallas as pl
from jax.experimental.pallas import tpu as pltpu

F32 = jnp.float32
BF16 = jnp.bfloat16

CHUNK = 64
A_HEADS = 6
A_QK = 32
A_V = 64
B_GROUPS = 4
B_CH = 64
POOL_WINDOWS = (2, 4, 8, 16)
POOL_HIST = 15
C_HEADS = 6
C_NOPE = 64
C_ROPE = 32
C_V = 64
C_Q_RANK = 256
C_KV_RANK = 128
ROPE_BASE = 10000.0
REL_BUCKETS = 32
REL_MAX_DIST = 128
CONV_W = 3
EPS = 1e-6

A_WIDTH = A_HEADS * A_V
B_WIDTH = B_GROUPS * B_CH
C_WIDTH = C_HEADS * C_V
OFF_B = 3 * A_WIDTH
OFF_CQ = OFF_B + B_WIDTH
OFF_CKV = OFF_CQ + C_Q_RANK
OFF_CKR = OFF_CKV + C_KV_RANK

LANES = 128
ATT_BLOCK = 256
FFN_CHUNK = 1024
LOG2E = math.log2(math.e)
NEG_BIG = -1e30
VMEM_LIMIT = 56 * 1024 * 1024

N1 = OFF_CKR + 2 * LANES


def _cparams(sem):
    return pltpu.CompilerParams(dimension_semantics=sem, vmem_limit_bytes=VMEM_LIMIT)


def _rms(x):
    return x * lax.rsqrt(jnp.mean(x * x, axis=-1, keepdims=True) + EPS)


def _ada_kernel(c_ref, w_ref, b_ref, o_ref):
    c = c_ref[...]
    s = (c * jax.nn.sigmoid(c)).astype(BF16)
    o_ref[0] = jnp.dot(s, w_ref[0].astype(BF16), preferred_element_type=F32) + b_ref[0]


def _ada(c_all, w_ada, b_ada):
    depth, d, n = w_ada.shape
    nb = c_all.shape[0]
    tn = 1024
    return pl.pallas_call(
        _ada_kernel,
        grid=(depth, n // tn),
        in_specs=[pl.BlockSpec((nb, d), lambda l, j: (0, 0)),
                  pl.BlockSpec((1, d, tn), lambda l, j: (l, 0, j)),
                  pl.BlockSpec((1, 1, tn), lambda l, j: (l, 0, j))],
        out_specs=pl.BlockSpec((1, nb, tn), lambda l, j: (l, 0, j)),
        out_shape=jax.ShapeDtypeStruct((depth, nb, n), F32),
        compiler_params=_cparams(("arbitrary", "arbitrary")),
        name="ada_mod",
    )(c_all, w_ada, b_ada.reshape(depth, 1, n))


def _bias_kernel(rb_ref, bucket_ref, madd_ref, o_ref, mx_ref):
    h = pl.program_id(0)
    bucket = bucket_ref[...]
    far = rb_ref[REL_BUCKETS // 2 - 1, h]
    val = jnp.zeros(bucket.shape, F32)
    top = far - far
    for b in range(REL_BUCKETS):
        val = jnp.where(bucket == b, rb_ref[b, h] - far, val)
        top = jnp.maximum(top, rb_ref[b, h] - far)
    o_ref[0] = val * LOG2E + madd_ref[...]
    mx_ref[0] = jnp.full(mx_ref.shape[1:], top * LOG2E, F32)


def _bias_tiles(rel_bias, bucket, madd):
    r, c = bucket.shape
    return pl.pallas_call(
        _bias_kernel,
        grid=(A_HEADS,),
        in_specs=[pl.BlockSpec(memory_space=pltpu.SMEM),
                  pl.BlockSpec((r, c), lambda h: (0, 0)),
                  pl.BlockSpec((r, c), lambda h: (0, 0))],
        out_specs=[pl.BlockSpec((1, r, c), lambda h: (h, 0, 0)),
                   pl.BlockSpec((1, 8, LANES), lambda h: (h, 0, 0))],
        out_shape=[jax.ShapeDtypeStruct((A_HEADS, r, c), F32),
                   jax.ShapeDtypeStruct((A_HEADS, 8, LANES), F32)],
        compiler_params=_cparams(("arbitrary",)),
        name="rel_bias_tiles",
    )(rel_bias, bucket, madd)


def _t5_bucket(rel):
    half = REL_BUCKETS // 2
    exact = half // 2
    ret = jnp.where(rel > 0, half, 0)
    n = jnp.abs(rel)
    nf = jnp.maximum(n, 1).astype(jnp.float32)
    large = exact + (jnp.log(nf / exact) / math.log(REL_MAX_DIST / exact) * (half - exact)).astype(jnp.int32)
    large = jnp.minimum(large, half - 1)
    return ret + jnp.where(n < exact, n, large)


def _tile_bucket_mask(q_pos, k_pos):
    q_pos = jnp.asarray(q_pos, jnp.int32)
    k_pos = jnp.asarray(k_pos, jnp.int32)
    bucket = _t5_bucket(k_pos[None, :] - q_pos[:, None]).astype(jnp.int32)
    visible = (k_pos[None, :] // CHUNK) <= (q_pos[:, None] // CHUNK)
    return bucket, jnp.where(visible, 0.0, NEG_BIG).astype(F32)


def _premix_kernel(x_ref, mod_ref, gmix_ref, w1_ref, cs_ref, sn_ref, csq_ref, gq_ref, wq_ref,
                   gkv_ref, wkv_ref, wp_ref, ps_ref, hist_ref,
                   aq_ref, ak_ref, av_ref, akb_ref, avb_ref, ub_ref, bo_ref, lat_ref, kr_ref,
                   cq_ref, ck_ref, cv_ref, carry_ref, *, tm, pos0):
    t = pl.program_id(1)
    x = x_ref[0]
    sh1 = mod_ref[0, 0:1, :]
    sc1 = mod_ref[0, 1:2, :]
    h = (_rms(x) * gmix_ref[...]) * (1.0 + sc1) + sh1
    u = jnp.dot(h.astype(BF16), w1_ref[...], preferred_element_type=F32)

    aq_ref[0] = (u[:, 0:A_WIDTH] * (A_QK ** -0.5 * LOG2E)).astype(BF16)
    ak = u[:, A_WIDTH:2 * A_WIDTH]
    av = u[:, 2 * A_WIDTH:3 * A_WIDTH]
    ak_ref[0] = ak
    av_ref[0] = av
    akb_ref[0] = ak.astype(BF16)
    avb_ref[0] = av.astype(BF16)

    ub = u[:, OFF_B:OFF_CQ]
    ub_ref[0] = ub

    @pl.when(t == 0)
    def _():
        carry_ref[...] = hist_ref[0]

    ext = jnp.concatenate([carry_ref[...], ub], axis=0)
    carry_ref[...] = ub[tm - 16:, :]
    s2 = ext + pltpu.roll(ext, 1, 0)
    s4 = s2 + pltpu.roll(s2, 2, 0)
    s8 = s4 + pltpu.roll(s4, 4, 0)
    s16 = s8 + pltpu.roll(s8, 8, 0)
    lane = lax.broadcasted_iota(jnp.int32, (tm, B_WIDTH), 1)
    grp = lane // B_CH
    tot = jnp.where(grp == 0, s2[16:], jnp.where(grp == 1, s4[16:], jnp.where(grp == 2, s8[16:], s16[16:])))
    win = jnp.where(grp == 0, 2, jnp.where(grp == 1, 4, jnp.where(grp == 2, 8, 16)))
    pos = pos0 + t * tm + lax.broadcasted_iota(jnp.int32, (tm, B_WIDTH), 0)
    cnt = jnp.minimum(pos + 1, win).astype(F32)
    m = tot / cnt - ub
    y = jnp.dot(m.astype(BF16), wp_ref[...], preferred_element_type=F32) * ps_ref[...]
    bo_ref[0] = y.astype(BF16)

    cs = cs_ref[...]
    sn = sn_ref[...]
    csq = csq_ref[...]
    qn = (_rms(u[:, OFF_CQ:OFF_CKV]) * gq_ref[...]).astype(BF16)
    q2 = jnp.dot(qn, wq_ref[...], preferred_element_type=F32)
    qscale = (C_NOPE + C_ROPE) ** -0.5 * LOG2E
    lat = _rms(u[:, OFF_CKV:OFF_CKR]) * gkv_ref[...]
    lat_ref[0] = lat
    krp = u[:, OFF_CKR:OFF_CKR + LANES] * cs + u[:, OFF_CKR + LANES:N1] * sn
    kr_ref[0] = krp[:, 0:C_ROPE]
    kv = jnp.dot(lat.astype(BF16), wkv_ref[...], preferred_element_type=F32)
    for hh in range(C_HEADS):
        sl = slice(hh * LANES, (hh + 1) * LANES)
        sr = slice((C_HEADS + hh) * LANES, (C_HEADS + hh + 1) * LANES)
        cq_ref[0, :, sl] = ((q2[:, sl] * csq + q2[:, sr] * sn) * qscale).astype(BF16)
        ck_ref[0, :, sl] = (kv[:, sl] + krp).astype(BF16)
    cv_ref[0] = kv[:, C_HEADS * LANES:].astype(BF16)


def _premix(x, mod, gmix, w1, cs, sn, csq, gq, wq, gkv, wkv, wp, ps, hist16, pos0):
    b, t, d = x.shape
    tm = min(1024, t)
    assert t % tm == 0 and tm >= 16
    full = lambda shape: pl.BlockSpec(shape, lambda i, j: (0,) * len(shape))
    tile = lambda n: pl.BlockSpec((1, tm, n), lambda i, j: (i, j, 0))
    tab = pl.BlockSpec((tm, LANES), lambda i, j: (j, 0))
    outs = [(A_WIDTH, BF16), (A_WIDTH, F32), (A_WIDTH, F32), (A_WIDTH, BF16), (A_WIDTH, BF16),
            (B_WIDTH, F32), (B_WIDTH, BF16), (C_KV_RANK, F32), (C_ROPE, F32),
            (C_HEADS * LANES, BF16), (C_HEADS * LANES, BF16), (C_WIDTH, BF16)]
    return pl.pallas_call(
        functools.partial(_premix_kernel, tm=tm, pos0=pos0),
        grid=(b, t // tm),
        in_specs=[tile(d),
                  pl.BlockSpec((1, 6, d), lambda i, j: (i, 0, 0)),
                  full((1, d)), full(w1.shape), tab, tab, tab,
                  full((1, C_Q_RANK)), full(wq.shape), full((1, C_KV_RANK)), full(wkv.shape),
                  full(wp.shape), full((1, B_WIDTH)),
                  pl.BlockSpec((1, 16, B_WIDTH), lambda i, j: (i, 0, 0))],
        out_specs=[tile(n) for n, _ in outs],
        out_shape=[jax.ShapeDtypeStruct((b, t, n), dt) for n, dt in outs],
        scratch_shapes=[pltpu.VMEM((16, B_WIDTH), F32)],
        compiler_params=_cparams(("arbitrary", "arbitrary")),
        name="premix",
    )(x, mod, gmix, w1, cs, sn, csq, gq, wq, gkv, wkv, wp, ps, hist16)


def _kvpast_kernel(lat_ref, kr_ref, wkv_ref, e_ref, ck_ref, cv_ref):
    kv = jnp.dot(lat_ref[0].astype(BF16), wkv_ref[...], preferred_element_type=F32)
    krp = jnp.dot(kr_ref[0].astype(BF16), e_ref[...], preferred_element_type=F32)
    for hh in range(C_HEADS):
        sl = slice(hh * LANES, (hh + 1) * LANES)
        ck_ref[0, :, sl] = (kv[:, sl] + krp).astype(BF16)
    cv_ref[0] = kv[:, C_HEADS * LANES:].astype(BF16)


def _kvpast(lat, kr, wkv):
    b, p, _ = lat.shape
    tm = min(512, p)
    assert p % tm == 0
    e = jnp.eye(C_ROPE, LANES, dtype=BF16)
    return pl.pallas_call(
        _kvpast_kernel,
        grid=(b, p // tm),
        in_specs=[pl.BlockSpec((1, tm, C_KV_RANK), lambda i, j: (i, j, 0)),
                  pl.BlockSpec((1, tm, C_ROPE), lambda i, j: (i, j, 0)),
                  pl.BlockSpec(wkv.shape, lambda i, j: (0, 0)),
                  pl.BlockSpec(e.shape, lambda i, j: (0, 0))],
        out_specs=[pl.BlockSpec((1, tm, C_HEADS * LANES), lambda i, j: (i, j, 0)),
                   pl.BlockSpec((1, tm, C_WIDTH), lambda i, j: (i, j, 0))],
        out_shape=[jax.ShapeDtypeStruct((b, p, C_HEADS * LANES), BF16),
                   jax.ShapeDtypeStruct((b, p, C_WIDTH), BF16)],
        compiler_params=_cparams(("arbitrary", "arbitrary")),
        name="mla_kv_past",
    )(lat, kr, wkv, e)


def _online_update(carry, s, v):
    m, l, acc = carry
    m_new = jnp.maximum(m, jnp.max(s, axis=-1, keepdims=True))
    alpha = jnp.exp2(m - m_new)
    p = jnp.exp2(s - m_new)
    l = alpha * l + jnp.sum(p, axis=-1, keepdims=True)
    acc = alpha * acc + jnp.dot(p.astype(BF16), v, preferred_element_type=F32)
    return m_new, l, acc


def _qk(q, k):
    return lax.dot_general(q, k, (((1,), (1,)), ((), ())), preferred_element_type=F32)


def _attn_kernel(*refs, diff, sub_bias, diag_bias, bq, bk, n_all, lam_init):
    it = iter(refs)
    lamp_ref = next(it) if diff else None
    g_ref = next(it) if diff else None
    q_ref, ka_ref, va_ref, kd_ref, vd_ref = (next(it) for _ in range(5))
    bs_ref = next(it) if sub_bias else None
    bd_ref = next(it) if diag_bias else None
    bmax_ref = next(it) if diff else None
    o_ref = next(it)
    vaug_ref, kmax_ref, acc_ref = (next(it) for _ in range(3))
    merged = n_all is None

    qi = pl.program_id(2)
    t_all = ka_ref.shape[1]
    bkd = kd_ref.shape[1]
    na = qi if n_all is None else n_all
    lane = lax.broadcasted_iota(jnp.int32, (bq, LANES), 1)
    q = q_ref[0]
    zero = jnp.zeros_like(q)
    ones_sq = jnp.ones((LANES, LANES), BF16)

    def vmask(v, n):
        ln = lax.broadcasted_iota(jnp.int32, (n, LANES), 1)
        return (ln >= A_QK * v) & (ln < A_QK * (v + 1))

    if diff:
        qs = jnp.concatenate([jnp.where(vmask(v, bq), q, zero) for v in range(4)], axis=0)
        groups = [(qs, 0, [0, 0, 1, 1])]
    else:
        groups = [(q[:, 0:LANES], 0, [0]), (q[:, LANES:2 * LANES], 1, [0])]
    n_var = sum(len(bi) for _, _, bi in groups)

    def key_norm2(k, var):
        kk = k.astype(BF16).astype(F32)
        kk = kk * kk
        if diff:
            kk = jnp.where(vmask(var, k.shape[0]), kk, 0.0)
        n2 = jnp.dot(kk.astype(BF16), ones_sq, preferred_element_type=F32)
        return jnp.max(n2, axis=0, keepdims=True)

    def var_keys(k, var):
        return k if diff else k[:, var * LANES:(var + 1) * LANES]

    @pl.when(qi == 0)
    def _():
        vaug_ref[:, 0:LANES] = va_ref[0].astype(BF16)
        vaug_ref[:, LANES:] = jnp.ones((t_all, LANES), BF16)
        rows = min(t_all, 1024)
        for var in range(n_var):
            km = jnp.zeros((1, LANES), F32)
            for c in range(t_all // rows):
                km = jnp.maximum(km, key_norm2(var_keys(ka_ref[0, c * rows:(c + 1) * rows, :], var), var))
            kmax_ref[var * 8:(var + 1) * 8, :] = jnp.broadcast_to(km, (8, LANES))

    def stack_bias(ref, bias_idx):
        tiles = [ref[0] if ref.shape[0] == 1 else ref[i] for i in bias_idx]
        return tiles[0] if len(tiles) == 1 else jnp.concatenate(tiles, axis=0)

    kd = kd_ref[0]
    vd = vd_ref[0]
    has_sub = sub_bias or merged
    if has_sub:
        n_far = jnp.maximum(na - 1, 0) if merged else max(na - 1, 0)
        off_s = pl.multiple_of(n_far * bk, bk) if merged else n_far * bk
    else:
        n_far = na

    def diag_biases():
        return [stack_bias(bd_ref, bi) if diag_bias else None for _, _, bi in groups]

    def sub_biases():
        gate = jnp.where(na >= 1, 0.0, NEG_BIG)
        return [(stack_bias(bs_ref, bi) + gate) if sub_bias else jnp.full((len(bi) * bq, bk), gate, F32)
                for _, _, bi in groups]

    refs_g = []
    var = 0
    for qg, kcol, bi in groups:
        qf = qg.astype(F32)
        qn2 = jnp.sum(qf * qf, axis=-1, keepdims=True)
        parts = []
        for i, b_i in enumerate(bi):
            km2 = kmax_ref[var * 8:var * 8 + 1, :]
            if n_all is not None:
                km2 = jnp.maximum(km2, key_norm2(var_keys(kd, var), var))
            r = jnp.sqrt(qn2[i * bq:(i + 1) * bq] * km2) * (1.0 + 2.0 ** -6)
            if diff:
                r = r + bmax_ref[b_i, 0:1, :]
            parts.append(r)
            var += 1
        refs_g.append(parts[0] if len(parts) == 1 else jnp.concatenate(parts, axis=0))

    def probs(k, biases):
        out = []
        for (qg, kcol, _), mref, bias in zip(groups, refs_g, biases):
            s = _qk(qg, k[:, kcol * LANES:(kcol + 1) * LANES].astype(BF16))
            if bias is not None:
                s = s + bias
            n = s.shape[1]
            if n >= LANES:
                cols = [jnp.exp2(s[:, c:c + LANES] - mref) for c in range(0, n, LANES)]
            else:
                cols = [jnp.exp2(s - mref[:, :n])]
            out.append((cols[0] if len(cols) == 1 else jnp.concatenate(cols, axis=1)).astype(BF16))
        return out[0] if len(out) == 1 else jnp.concatenate(out, axis=0)

    none_bias = [None] * len(groups)
    vd_aug = jnp.concatenate([vd, jnp.ones((bkd, LANES), BF16)], axis=1)
    if merged:
        k_near = jnp.concatenate([ka_ref[0, pl.ds(off_s, bk), :], kd], axis=0)
        v_near = jnp.concatenate([vaug_ref[pl.ds(off_s, bk), :], vd_aug], axis=0)
        bias_near = [jnp.concatenate([b_s, jnp.zeros_like(b_s) if b_d is None else b_d], axis=1)
                     for b_s, b_d in zip(sub_biases(), diag_biases())]
        acc_ref[...] = jnp.dot(probs(k_near, bias_near), v_near, preferred_element_type=F32)

        def far_step(off, width):
            off = pl.multiple_of(off, bk)
            acc_ref[...] += jnp.dot(probs(ka_ref[0, pl.ds(off, width), :], none_bias),
                                    vaug_ref[pl.ds(off, width), :], preferred_element_type=F32)

        def far_loop(j, c):
            far_step(j * (4 * bk), 4 * bk)
            return c

        n_wide = n_far // 4
        lax.fori_loop(0, n_wide, far_loop, 0)
        rem = n_far - 4 * n_wide
        base = n_wide * (4 * bk)
        pl.when((rem & 2) != 0)(lambda: far_step(base, 2 * bk))
        pl.when((rem & 1) != 0)(lambda: far_step(base + (rem & 2) * bk, bk))
    else:
        acc_ref[...] = jnp.dot(probs(kd, diag_biases()), vd_aug, preferred_element_type=F32)
        if has_sub:
            acc_ref[...] += jnp.dot(probs(ka_ref[0, pl.ds(off_s, bk), :], sub_biases()),
                                    vaug_ref[pl.ds(off_s, bk), :], preferred_element_type=F32)
        for off in range(0, n_far * bk, 4 * bk):
            width = min(4 * bk, n_far * bk - off)
            acc_ref[...] += jnp.dot(probs(ka_ref[0, off:off + width, :], none_bias),
                                    vaug_ref[off:off + width, :], preferred_element_type=F32)

    def finish(results):
        if diff:
            lp = lamp_ref[...]
            lam = (jnp.exp(jnp.sum(lp[0:1] * lp[1:2], axis=-1, keepdims=True))
                   - jnp.exp(jnp.sum(lp[2:3] * lp[3:4], axis=-1, keepdims=True)) + lam_init)
            outs = []
            for e in range(2):
                d = results[2 * e] - lam * results[2 * e + 1]
                own = (lane >= A_V * e) & (lane < A_V * (e + 1))
                ms = jnp.sum(jnp.where(own, d * d, 0.0), axis=-1, keepdims=True) * (1.0 / A_V)
                outs.append(d * lax.rsqrt(ms + EPS) * g_ref[...] * (1.0 - lam_init))
            out = jnp.where(lane < A_V, outs[0], outs[1])
        else:
            out = jnp.where(lane < C_V, results[0], results[1])
        o_ref[0] = out.astype(o_ref.dtype)

    acc = acc_ref[...]
    den = acc[:, LANES:]
    safe = (jnp.min(den) >= 2.0 ** -80) & (jnp.max(den) <= 2.0 ** 100)
    o_fast = acc[:, 0:LANES] / den
    finish([o_fast[i * bq:(i + 1) * bq] for i in range(n_var)])

    @pl.when(jnp.logical_not(safe))
    def _():
        def step(carries, k, v, biases):
            out = []
            for (qg, kcol, _), c, bias in zip(groups, carries, biases):
                s = _qk(qg, k[:, kcol * LANES:(kcol + 1) * LANES].astype(BF16))
                out.append(_online_update(c, s if bias is None else s + bias, v.astype(BF16)))
            return tuple(out)

        carries = tuple((jnp.full((len(bi) * bq, 1), NEG_BIG, F32), jnp.zeros((len(bi) * bq, 1), F32),
                         jnp.zeros((len(bi) * bq, LANES), F32)) for _, _, bi in groups)
        carries = step(carries, kd, vd, diag_biases())
        if has_sub:
            carries = step(carries, ka_ref[0, pl.ds(off_s, bk), :], va_ref[0, pl.ds(off_s, bk), :],
                           sub_biases())

        def body(j, cs):
            off = pl.multiple_of(j * bk, bk)
            return step(cs, ka_ref[0, pl.ds(off, bk), :], va_ref[0, pl.ds(off, bk), :], none_bias)

        carries = lax.fori_loop(0, n_far, body, carries)
        results = []
        for (_, _, bi), (m, l, a) in zip(groups, carries):
            o = a / l
            results.extend(o[i * bq:(i + 1) * bq] for i in range(len(bi)))
        finish(results)


def _attention(q, k_all, v_all, k_new, v_new, bias_sub, bias_diag, *, diff, causal_blocks,
               lamp=None, g2=None, bias_max=None, lam_init=0.0):
    b, tq, _ = q.shape
    t_all = k_all.shape[1]
    bk = ATT_BLOCK
    bq = min(ATT_BLOCK, tq)
    assert tq % bq == 0 and t_all % bk == 0
    if causal_blocks:
        assert bq == bk and t_all == tq
    else:
        assert tq == bq
    qw = LANES if diff else 2 * LANES
    n_pairs = A_HEADS // 2
    in_specs, args = [], []
    if diff:
        in_specs += [pl.BlockSpec(lamp.shape, lambda i, p, j: (0, 0)),
                     pl.BlockSpec(g2.shape, lambda i, p, j: (0, 0))]
        args += [lamp, g2]
    in_specs += [pl.BlockSpec((1, bq, qw), lambda i, p, j: (i, j, p)),
                 pl.BlockSpec((1, t_all, qw), lambda i, p, j: (i, 0, p)),
                 pl.BlockSpec((1, t_all, LANES), lambda i, p, j: (i, 0, p)),
                 pl.BlockSpec((1, bq, qw), lambda i, p, j: (i, j, p)),
                 pl.BlockSpec((1, bq, LANES), lambda i, p, j: (i, j, p))]
    args += [q, k_all, v_all, k_new, v_new]
    for bias in (bias_sub, bias_diag):
        if bias is not None:
            if bias.shape[0] == 1:
                in_specs.append(pl.BlockSpec(bias.shape, lambda i, p, j: (0, 0, 0)))
            else:
                in_specs.append(pl.BlockSpec((2,) + bias.shape[1:], lambda i, p, j: (p, 0, 0)))
            args.append(bias)
    if diff:
        in_specs.append(pl.BlockSpec((2, 8, LANES), lambda i, p, j: (p, 0, 0)))
        args.append(bias_max)
    n_var = 4 if diff else 2
    kern = functools.partial(
        _attn_kernel, diff=diff, sub_bias=bias_sub is not None, diag_bias=bias_diag is not None,
        bq=bq, bk=bk, n_all=None if causal_blocks else t_all // bk, lam_init=lam_init)
    return pl.pallas_call(
        kern,
        grid=(b, n_pairs, tq // bq),
        in_specs=in_specs,
        out_specs=pl.BlockSpec((1, bq, LANES), lambda i, p, j: (i, j, p)),
        out_shape=jax.ShapeDtypeStruct((b, tq, n_pairs * LANES), BF16),
        scratch_shapes=[pltpu.VMEM((t_all, 2 * LANES), BF16),
                        pltpu.VMEM((n_var * 8, LANES), F32),
                        pltpu.VMEM((n_var * bq, 2 * LANES), F32)],
        compiler_params=_cparams(("arbitrary", "arbitrary", "arbitrary")),
        name="diff_attn" if diff else "mla_attn",
    )(*args)


def _postffn_kernel(a_ref, b_ref, c_ref, x_ref, mod_ref, wo_ref, gffn_ref, wg_ref, wv_ref, cw_ref,
                    cb_ref, wd_ref, hist_ref, gfin_ref, y_ref, conv_ref, carry_ref, act_ref, *, tm, final):
    t = pl.program_id(1)
    ffn = wg_ref.shape[1]
    cat = jnp.concatenate([a_ref[0], b_ref[0], c_ref[0]], axis=-1)
    mix = jnp.dot(cat, wo_ref[...], preferred_element_type=F32)
    gt1 = mod_ref[0, 2:3, :]
    sh2 = mod_ref[0, 3:4, :]
    sc2 = mod_ref[0, 4:5, :]
    gt2 = mod_ref[0, 5:6, :]
    x1 = x_ref[0] + gt1 * mix
    h2 = ((_rms(x1) * gffn_ref[...]) * (1.0 + sc2) + sh2).astype(BF16)

    @pl.when(t == 0)
    def _():
        carry_ref[...] = hist_ref[0]

    for c0 in range(0, ffn, FFN_CHUNK):
        sl = slice(c0, min(c0 + FFN_CHUNK, ffn))
        g = jnp.dot(h2, wg_ref[:, sl], preferred_element_type=F32)
        val = jnp.dot(h2, wv_ref[:, sl], preferred_element_type=F32)
        ext = jnp.concatenate([carry_ref[:, sl], g], axis=0)
        tail = g[tm - 8:, :]
        carry_ref[:, sl] = tail
        conv_ref[0, :, sl] = tail
        gc = (cw_ref[0:1, sl] * pltpu.roll(ext, 2, 0)[8:] + cw_ref[1:2, sl] * pltpu.roll(ext, 1, 0)[8:]
              + cw_ref[2:3, sl] * g + cb_ref[:, sl])
        act_ref[:, sl] = (gc * jax.nn.sigmoid(gc) * val).astype(BF16)
    f = jnp.dot(act_ref[...], wd_ref[...], preferred_element_type=F32)
    x2 = x1 + gt2 * f
    if final:
        x2 = _rms(x2) * gfin_ref[...]
    y_ref[0] = x2


def _postffn(a, bo, c, x, mod, wo, gffn, wg, wv, cw, cb, wd, hist8, gfin, final):
    b, t, d = x.shape
    f = wg.shape[1]
    tm = min(512, t)
    assert t % tm == 0 and tm >= 8
    tile = lambda n: pl.BlockSpec((1, tm, n), lambda i, j: (i, j, 0))

    def resident(shape):
        return pl.BlockSpec(shape, lambda i, j: (0,) * len(shape), pipeline_mode=pl.Buffered(1))

    return pl.pallas_call(
        functools.partial(_postffn_kernel, tm=tm, final=final),
        grid=(b, t // tm),
        in_specs=[tile(A_WIDTH), tile(B_WIDTH), tile(C_WIDTH), tile(d),
                  pl.BlockSpec((1, 6, d), lambda i, j: (i, 0, 0)),
                  resident(wo.shape), resident((1, d)), resident(wg.shape), resident(wv.shape),
                  resident((CONV_W, f)), resident((1, f)), resident(wd.shape),
                  pl.BlockSpec((1, 8, f), lambda i, j: (i, 0, 0)),
                  resident((1, d))],
        out_specs=[tile(d), pl.BlockSpec((1, 8, f), lambda i, j: (i, 0, 0))],
        out_shape=[jax.ShapeDtypeStruct((b, t, d), F32), jax.ShapeDtypeStruct((b, 8, f), F32)],
        scratch_shapes=[pltpu.VMEM((8, f), F32), pltpu.VMEM((tm, f), BF16)],
        compiler_params=_cparams(("arbitrary", "arbitrary")),
        name="postmix_ffn",
    )(a, bo, c, x, mod, wo, gffn, wg, wv, cw, cb, wd, hist8, gfin)


def _rot_cols(w):
    half = C_ROPE // 2
    return jnp.concatenate([-w[..., half:], w[..., :half]], axis=-1)


def _layer_weights(l, w_in, w_q_up, w_kv_up, pool_w, w_out, w_up):
    d = w_in.shape[1]
    wkr = w_in[l][:, OFF_CKR:]
    zpad = jnp.zeros((d, LANES - C_ROPE), F32)
    w1 = jnp.concatenate([w_in[l][:, :OFF_CKR], wkr, zpad, _rot_cols(wkr), zpad], axis=1).astype(BF16)

    wq = w_q_up[l].reshape(C_Q_RANK, C_HEADS, C_NOPE + C_ROPE)
    nope, rope = wq[..., :C_NOPE], wq[..., C_NOPE:]
    zq = jnp.zeros((C_Q_RANK, C_HEADS, LANES - C_NOPE - C_ROPE), F32)
    main = jnp.concatenate([rope, nope, zq], axis=-1).reshape(C_Q_RANK, C_HEADS * LANES)
    rot = jnp.concatenate([_rot_cols(rope), jnp.zeros((C_Q_RANK, C_HEADS, LANES - C_ROPE), F32)],
                          axis=-1).reshape(C_Q_RANK, C_HEADS * LANES)
    wq2 = jnp.concatenate([main, rot], axis=1).astype(BF16)

    wkv = w_kv_up[l].reshape(C_KV_RANK, C_HEADS, C_NOPE + C_V)
    zk = jnp.zeros((C_KV_RANK, C_HEADS, C_ROPE), F32)
    kpart = jnp.concatenate([zk, wkv[..., :C_NOPE], zk], axis=-1).reshape(C_KV_RANK, C_HEADS * LANES)
    vpart = wkv[..., C_NOPE:].reshape(C_KV_RANK, C_WIDTH)
    wkv2 = jnp.concatenate([kpart, vpart], axis=1).astype(BF16)

    wp = jnp.zeros((B_WIDTH, B_WIDTH), F32)
    for g in range(B_GROUPS):
        wp = wp.at[g * B_CH:(g + 1) * B_CH, g * B_CH:(g + 1) * B_CH].set(pool_w[l, g])
    ffn = w_up.shape[2] // 2
    return dict(w1=w1, wq=wq2, wkv=wkv2, wp=wp.astype(BF16), wo=w_out[l].astype(BF16),
                wg=w_up[l][:, :ffn].astype(BF16), wv=w_up[l][:, ffn:].astype(BF16))


def _rope_tables(pos):
    half = C_ROPE // 2
    inv = 1.0 / (ROPE_BASE ** (jnp.arange(half, dtype=jnp.float32) / half))
    ang = pos.astype(jnp.float32)[:, None] * inv[None, :]
    cos, sin = jnp.cos(ang), jnp.sin(ang)
    n = pos.shape[0]
    z = jnp.zeros((n, LANES - C_ROPE), F32)
    cs = jnp.concatenate([cos, cos, z], axis=1)
    sn = jnp.concatenate([sin, sin, z], axis=1)
    csq = jnp.concatenate([cos, cos, jnp.ones((n, C_NOPE), F32), jnp.zeros((n, LANES - C_ROPE - C_NOPE), F32)], axis=1)
    return cs, sn, csq


def kernel(x_prompt, x_sample, c_prompt, c_sample, cache_a_k, cache_a_v, cache_c_latent, cache_c_krope,
           state_b_pool, state_ffn_conv, w_ada, b_ada, g_mix, w_in, lam_q1, lam_k1, lam_q2, lam_k2,
           a_subln_g, rel_bias, pool_w, pool_scale, c_q_norm_g, w_q_up, c_kv_norm_g, w_kv_up, w_out,
           g_ffn, w_up, conv_w, conv_b, w_down, g_final):
    depth = w_in.shape[0]
    bp, tp, d = x_prompt.shape
    bs, ts, _ = x_sample.shape
    past = cache_a_k.shape[2]
    ffn = conv_w.shape[2]
    blk = ATT_BLOCK

    mod_all = _ada(jnp.concatenate([c_prompt, c_sample], axis=0), w_ada, b_ada)
    mod_all = mod_all.reshape(depth, bp + bs, 6, d)

    r = np.arange(blk)
    bkt, madd = _tile_bucket_mask(blk + r, r)
    pb_sub, bias_max = _bias_tiles(rel_bias, bkt, madd)
    bkt, madd = _tile_bucket_mask(r, r)
    pb_diag, _ = _bias_tiles(rel_bias, bkt, madd)
    mla_diag = madd[None]
    rs = np.arange(ts)
    bkt, madd = _tile_bucket_mask(past + rs, past - blk + r)
    sb_sub, _ = _bias_tiles(rel_bias, bkt, madd)
    bkt, madd = _tile_bucket_mask(past + rs, past + rs)
    sb_diag, _ = _bias_tiles(rel_bias, bkt, madd)
    mla_sdiag = madd[None]

    tabs_p = _rope_tables(jnp.arange(tp, dtype=jnp.int32))
    tabs_s = _rope_tables(past + jnp.arange(ts, dtype=jnp.int32))
    zero_pool = jnp.zeros((bp, 16, B_WIDTH), F32)
    zero_conv = jnp.zeros((bp, 8, ffn), F32)

    hp, hs = x_prompt, x_sample
    new_p, new_s = [], []
    for l in range(depth):
        w = _layer_weights(l, w_in, w_q_up, w_kv_up, pool_w, w_out, w_up)
        lam_init = 0.8 - 0.6 * math.exp(-0.3 * l)
        lamp = jnp.stack([lam_q1[l], lam_k1[l], lam_q2[l], lam_k2[l]], axis=0)
        g2 = jnp.concatenate([a_subln_g[l], a_subln_g[l]])[None]
        wd = w_down[l].astype(BF16)
        last = l == depth - 1

        def run(x, mod, tabs, pos0, hist16, hist8, pasts):
            (aq, ak, av, akb, avb, ub, bo, lat, kr, cq, ck, cv) = _premix(
                x, mod, g_mix[l][None], w["w1"], *tabs, c_q_norm_g[l][None], w["wq"],
                c_kv_norm_g[l][None], w["wkv"], w["wp"], pool_scale[l][None], hist16, pos0)
            if pasts is None:
                a_out = _attention(aq, akb, avb, akb, avb, pb_sub, pb_diag, diff=True, causal_blocks=True,
                                   lamp=lamp, g2=g2, bias_max=bias_max, lam_init=lam_init)
                c_out = _attention(cq, ck, cv, ck, cv, None, mla_diag, diff=False, causal_blocks=True)
            else:
                pk, pv, plat, pkr = pasts
                pkb = pk.reshape(pk.shape[0], past, A_WIDTH)
                pvb = pv.reshape(pv.shape[0], past, A_WIDTH)
                a_out = _attention(aq, pkb, pvb, akb, avb, sb_sub, sb_diag, diff=True, causal_blocks=False,
                                   lamp=lamp, g2=g2, bias_max=bias_max, lam_init=lam_init)
                ckp, cvp = _kvpast(plat, pkr, w["wkv"])
                c_out = _attention(cq, ckp, cvp, ck, cv, None, mla_sdiag, diff=False, causal_blocks=False)
            y, conv8 = _postffn(a_out, bo, c_out, x, mod, w["wo"], g_ffn[l][None], w["wg"], w["wv"],
                                conv_w[l], conv_b[l][None], wd, hist8, g_final[None], last)
            bsz, tt = x.shape[0], x.shape[1]
            state = (ak.reshape(bsz, tt, A_HEADS, 2 * A_QK), av.reshape(bsz, tt, A_HEADS, A_V), lat, kr,
                     ub[:, tt - POOL_HIST:], conv8[:, 8 - (CONV_W - 1):])
            return y, state

        hp, st = run(hp, mod_all[l, :bp], tabs_p, 0, zero_pool, zero_conv, None)
        new_p.append(st)
        hist16 = jnp.pad(state_b_pool[l], ((0, 0), (1, 0), (0, 0)))
        hist8 = jnp.pad(state_ffn_conv[l], ((0, 0), (8 - (CONV_W - 1), 0), (0, 0)))
        hs, st = run(hs, mod_all[l, bp:], tabs_s, past, hist16, hist8,
                     (cache_a_k[l], cache_a_v[l], cache_c_latent[l], cache_c_krope[l]))
        new_s.append(st)

    outs_p = [jnp.stack(z, axis=0) for z in zip(*new_p)]
    outs_s = [jnp.stack(z, axis=0) for z in zip(*new_s)]
    return (hp, hs, *outs_p, *outs_s)
```

```python
import functools
import math

import numpy as np
import jax
import jax.numpy as jnp
from jax import lax
from jax.experimental import pallas as pl
from jax.experimental.pallas import tpu as pltpu

F32 = jnp.float32
BF16 = jnp.bfloat16

CHUNK = 64
A_HEADS = 6
A_QK = 32
A_V = 64
B_GROUPS = 4
B_CH = 64
POOL_WINDOWS = (2, 4, 8, 16)
POOL_HIST = 15
C_HEADS = 6
C_NOPE = 64
C_ROPE = 32
C_V = 64
C_Q_RANK = 256
C_KV_RANK = 128
ROPE_BASE = 10000.0
REL_BUCKETS = 32
REL_MAX_DIST = 128
CONV_W = 3
EPS = 1e-6

A_WIDTH = A_HEADS * A_V
B_WIDTH = B_GROUPS * B_CH
C_WIDTH = C_HEADS * C_V
OFF_B = 3 * A_WIDTH
OFF_CQ = OFF_B + B_WIDTH
OFF_CKV = OFF_CQ + C_Q_RANK
OFF_CKR = OFF_CKV + C_KV_RANK

LANES = 128
ATT_BLOCK = 256
FFN_CHUNK = 1024
LOG2E = math.log2(math.e)
NEG_BIG = -1e30
VMEM_LIMIT = 56 * 1024 * 1024

U_AQ = 0
U_B = U_AQ + A_WIDTH
U_CQ = U_B + B_WIDTH
U_CKV = U_CQ + C_Q_RANK
U_KR = U_CKV + C_KV_RANK
N1 = U_KR + 2 * LANES


def _cparams(sem):
    return pltpu.CompilerParams(dimension_semantics=sem, vmem_limit_bytes=VMEM_LIMIT)


def _rms(x):
    return x * lax.rsqrt(jnp.mean(x * x, axis=-1, keepdims=True) + EPS)


def _ada_kernel(c_ref, w_ref, b_ref, o_ref):
    c = c_ref[...]
    s = (c * jax.nn.sigmoid(c)).astype(BF16)
    o_ref[0] = jnp.dot(s, w_ref[0].astype(BF16), preferred_element_type=F32) + b_ref[0]


def _ada(c_all, w_ada, b_ada):
    depth, d, n = w_ada.shape
    nb = c_all.shape[0]
    tn = 1024
    return pl.pallas_call(
        _ada_kernel,
        grid=(depth, n // tn),
        in_specs=[pl.BlockSpec((nb, d), lambda l, j: (0, 0)),
                  pl.BlockSpec((1, d, tn), lambda l, j: (l, 0, j)),
                  pl.BlockSpec((1, 1, tn), lambda l, j: (l, 0, j))],
        out_specs=pl.BlockSpec((1, nb, tn), lambda l, j: (l, 0, j)),
        out_shape=jax.ShapeDtypeStruct((depth, nb, n), F32),
        compiler_params=_cparams(("arbitrary", "arbitrary")),
        name="ada_mod",
    )(c_all, w_ada, b_ada.reshape(depth, 1, n))


def _bias_kernel(rb_ref, bucket_ref, madd_ref, o_ref, mx_ref):
    h = pl.program_id(0)
    bucket = bucket_ref[...]
    far = rb_ref[REL_BUCKETS // 2 - 1, h]
    val = jnp.zeros(bucket.shape, F32)
    top = far - far
    for b in range(REL_BUCKETS):
        val = jnp.where(bucket == b, rb_ref[b, h] - far, val)
        top = jnp.maximum(top, rb_ref[b, h] - far)
    o_ref[0] = val * LOG2E + madd_ref[...]
    mx_ref[0] = jnp.full(mx_ref.shape[1:], top * LOG2E, F32)


def _bias_tiles(rel_bias, bucket, madd):
    r, c = bucket.shape
    return pl.pallas_call(
        _bias_kernel,
        grid=(A_HEADS,),
        in_specs=[pl.BlockSpec(memory_space=pltpu.SMEM),
                  pl.BlockSpec((r, c), lambda h: (0, 0)),
                  pl.BlockSpec((r, c), lambda h: (0, 0))],
        out_specs=[pl.BlockSpec((1, r, c), lambda h: (h, 0, 0)),
                   pl.BlockSpec((1, 8, LANES), lambda h: (h, 0, 0))],
        out_shape=[jax.ShapeDtypeStruct((A_HEADS, r, c), F32),
                   jax.ShapeDtypeStruct((A_HEADS, 8, LANES), F32)],
        compiler_params=_cparams(("arbitrary",)),
        name="rel_bias_tiles",
    )(rel_bias, bucket, madd)


def _t5_bucket(rel):
    half = REL_BUCKETS // 2
    exact = half // 2
    ret = jnp.where(rel > 0, half, 0)
    n = jnp.abs(rel)
    nf = jnp.maximum(n, 1).astype(jnp.float32)
    large = exact + (jnp.log(nf / exact) / math.log(REL_MAX_DIST / exact) * (half - exact)).astype(jnp.int32)
    large = jnp.minimum(large, half - 1)
    return ret + jnp.where(n < exact, n, large)


def _tile_bucket_mask(q_pos, k_pos):
    q_pos = jnp.asarray(q_pos, jnp.int32)
    k_pos = jnp.asarray(k_pos, jnp.int32)
    bucket = _t5_bucket(k_pos[None, :] - q_pos[:, None]).astype(jnp.int32)
    visible = (k_pos[None, :] // CHUNK) <= (q_pos[:, None] // CHUNK)
    return bucket, jnp.where(visible, 0.0, NEG_BIG).astype(F32)


def _premix_kernel(x_ref, mod_ref, gmix_ref, w1_ref, wkvt_ref, cs_ref, sn_ref, csq_ref, gq_ref, wq_ref,
                   gkv_ref, wkv_ref, wp_ref, ps_ref, hist_ref,
                   aq_ref, akt_ref, avt_ref, aktb_ref, avtb_ref, ub_ref, bo_ref, lat_ref, kr_ref,
                   cq_ref, ck_ref, cv_ref, carry_ref, *, tm, pos0):
    t = pl.program_id(1)
    x = x_ref[0]
    sh1 = mod_ref[0, 0:1, :]
    sc1 = mod_ref[0, 1:2, :]
    h = ((_rms(x) * gmix_ref[...]) * (1.0 + sc1) + sh1).astype(BF16)
    u = jnp.dot(h, w1_ref[...], preferred_element_type=F32)
    kvt = lax.dot_general(wkvt_ref[...], h, (((1,), (1,)), ((), ())), preferred_element_type=F32)
    akt = kvt[0:A_WIDTH]
    avt = kvt[A_WIDTH:]
    akt_ref[0] = akt
    avt_ref[0] = avt
    aktb_ref[0] = akt.astype(BF16)
    avtb_ref[0] = avt.astype(BF16)
    aq_ref[0] = (u[:, U_AQ:U_B] * (A_QK ** -0.5 * LOG2E)).astype(BF16)

    ub = u[:, U_B:U_CQ]
    ub_ref[0] = ub

    @pl.when(t == 0)
    def _():
        carry_ref[...] = hist_ref[0]

    ext = jnp.concatenate([carry_ref[...], ub], axis=0)
    carry_ref[...] = ub[tm - 16:, :]
    s2 = ext + pltpu.roll(ext, 1, 0)
    s4 = s2 + pltpu.roll(s2, 2, 0)
    s8 = s4 + pltpu.roll(s4, 4, 0)
    s16 = s8 + pltpu.roll(s8, 8, 0)
    lane = lax.broadcasted_iota(jnp.int32, (tm, B_WIDTH), 1)
    grp = lane // B_CH
    tot = jnp.where(grp == 0, s2[16:], jnp.where(grp == 1, s4[16:], jnp.where(grp == 2, s8[16:], s16[16:])))
    win = jnp.where(grp == 0, 2, jnp.where(grp == 1, 4, jnp.where(grp == 2, 8, 16)))
    pos = pos0 + t * tm + lax.broadcasted_iota(jnp.int32, (tm, B_WIDTH), 0)
    cnt = jnp.minimum(pos + 1, win).astype(F32)
    m = tot / cnt - ub
    y = jnp.dot(m.astype(BF16), wp_ref[...], preferred_element_type=F32) * ps_ref[...]
    bo_ref[0] = y.astype(BF16)

    cs = cs_ref[...]
    sn = sn_ref[...]
    csq = csq_ref[...]
    qn = (_rms(u[:, U_CQ:U_CKV]) * gq_ref[...]).astype(BF16)
    q2 = jnp.dot(qn, wq_ref[...], preferred_element_type=F32)
    qscale = (C_NOPE + C_ROPE) ** -0.5 * LOG2E
    lat = _rms(u[:, U_CKV:U_KR]) * gkv_ref[...]
    lat_ref[0] = lat
    krp = u[:, U_KR:U_KR + LANES] * cs + u[:, U_KR + LANES:N1] * sn
    kr_ref[0] = krp[:, 0:C_ROPE]
    kv = jnp.dot(lat.astype(BF16), wkv_ref[...], preferred_element_type=F32)
    for hh in range(C_HEADS):
        sl = slice(hh * LANES, (hh + 1) * LANES)
        sr = slice((C_HEADS + hh) * LANES, (C_HEADS + hh + 1) * LANES)
        cq_ref[0, :, sl] = ((q2[:, sl] * csq + q2[:, sr] * sn) * qscale).astype(BF16)
        ck_ref[0, :, sl] = (kv[:, sl] + krp).astype(BF16)
    cv_ref[0] = kv[:, C_HEADS * LANES:].astype(BF16)


def _premix(x, mod, gmix, w1, wkvt, cs, sn, csq, gq, wq, gkv, wkv, wp, ps, hist16, pos0):
    b, t, d = x.shape
    tm = min(1024, t)
    assert t % tm == 0 and tm >= 16
    full = lambda shape: pl.BlockSpec(shape, lambda i, j: (0,) * len(shape))
    tile = lambda n: pl.BlockSpec((1, tm, n), lambda i, j: (i, j, 0))
    tile_t = pl.BlockSpec((1, A_WIDTH, tm), lambda i, j: (i, 0, j))
    tab = pl.BlockSpec((tm, LANES), lambda i, j: (j, 0))
    outs = [(A_WIDTH, BF16), (None, F32), (None, F32), (None, BF16), (None, BF16),
            (B_WIDTH, F32), (B_WIDTH, BF16), (C_KV_RANK, F32), (C_ROPE, F32),
            (C_HEADS * LANES, BF16), (C_HEADS * LANES, BF16), (C_WIDTH, BF16)]
    return pl.pallas_call(
        functools.partial(_premix_kernel, tm=tm, pos0=pos0),
        grid=(b, t // tm),
        in_specs=[tile(d),
                  pl.BlockSpec((1, 6, d), lambda i, j: (i, 0, 0)),
                  full((1, d)), full(w1.shape), full(wkvt.shape), tab, tab, tab,
                  full((1, C_Q_RANK)), full(wq.shape), full((1, C_KV_RANK)), full(wkv.shape),
                  full(wp.shape), full((1, B_WIDTH)),
                  pl.BlockSpec((1, 16, B_WIDTH), lambda i, j: (i, 0, 0))],
        out_specs=[tile_t if n is None else tile(n) for n, _ in outs],
        out_shape=[jax.ShapeDtypeStruct((b, A_WIDTH, t) if n is None else (b, t, n), dt) for n, dt in outs],
        scratch_shapes=[pltpu.VMEM((16, B_WIDTH), F32)],
        compiler_params=_cparams(("arbitrary", "arbitrary")),
        name="premix",
    )(x, mod, gmix, w1, wkvt, cs, sn, csq, gq, wq, gkv, wkv, wp, ps, hist16)


def _kvpast_kernel(lat_ref, kr_ref, wkv_ref, e_ref, ck_ref, cv_ref):
    kv = jnp.dot(lat_ref[0].astype(BF16), wkv_ref[...], preferred_element_type=F32)
    krp = jnp.dot(kr_ref[0].astype(BF16), e_ref[...], preferred_element_type=F32)
    for hh in range(C_HEADS):
        sl = slice(hh * LANES, (hh + 1) * LANES)
        ck_ref[0, :, sl] = (kv[:, sl] + krp).astype(BF16)
    cv_ref[0] = kv[:, C_HEADS * LANES:].astype(BF16)


def _kvpast(lat, kr, wkv):
    b, p, _ = lat.shape
    tm = min(512, p)
    assert p % tm == 0
    e = jnp.eye(C_ROPE, LANES, dtype=BF16)
    return pl.pallas_call(
        _kvpast_kernel,
        grid=(b, p // tm),
        in_specs=[pl.BlockSpec((1, tm, C_KV_RANK), lambda i, j: (i, j, 0)),
                  pl.BlockSpec((1, tm, C_ROPE), lambda i, j: (i, j, 0)),
                  pl.BlockSpec(wkv.shape, lambda i, j: (0, 0)),
                  pl.BlockSpec(e.shape, lambda i, j: (0, 0))],
        out_specs=[pl.BlockSpec((1, tm, C_HEADS * LANES), lambda i, j: (i, j, 0)),
                   pl.BlockSpec((1, tm, C_WIDTH), lambda i, j: (i, j, 0))],
        out_shape=[jax.ShapeDtypeStruct((b, p, C_HEADS * LANES), BF16),
                   jax.ShapeDtypeStruct((b, p, C_WIDTH), BF16)],
        compiler_params=_cparams(("arbitrary", "arbitrary")),
        name="mla_kv_past",
    )(lat, kr, wkv, e)


def _online_update(carry, s, v, pv):
    m, l, acc = carry
    m_new = jnp.maximum(m, jnp.max(s, axis=-1, keepdims=True))
    alpha = jnp.exp2(m - m_new)
    p = jnp.exp2(s - m_new)
    l = alpha * l + jnp.sum(p, axis=-1, keepdims=True)
    acc = alpha * acc + pv(p.astype(BF16), v)
    return m_new, l, acc


def _dot_nt(a, b):
    return lax.dot_general(a, b, (((1,), (1,)), ((), ())), preferred_element_type=F32)


def _attn_kernel(*refs, diff, sub_bias, diag_bias, bq, bk, n_all, lam_init):
    it = iter(refs)
    lamp_ref = next(it) if diff else None
    g_ref = next(it) if diff else None
    q_ref, ka_ref, va_ref, kd_ref, vd_ref = (next(it) for _ in range(5))
    bs_ref = next(it) if sub_bias else None
    bd_ref = next(it) if diag_bias else None
    bmax_ref = next(it) if diff else None
    o_ref = next(it)
    vaug_ref, kmax_ref, acc_ref = (next(it) for _ in range(3))
    merged = n_all is None

    qi = pl.program_id(2)
    kt = diff
    key_axis = 2 if kt else 1
    t_all = ka_ref.shape[key_axis]
    bkd = kd_ref.shape[key_axis]

    def k_tile(off, width):
        return ka_ref[0, :, pl.ds(off, width)] if kt else ka_ref[0, pl.ds(off, width), :]

    def v_tile(off, width):
        return va_ref[0, :, pl.ds(off, width)] if kt else va_ref[0, pl.ds(off, width), :]

    def vaug_tile(off, width):
        return vaug_ref[:, pl.ds(off, width)] if kt else vaug_ref[pl.ds(off, width), :]

    def qk(qg, k, kcol):
        if kt:
            return jnp.dot(qg, k.astype(BF16), preferred_element_type=F32)
        return _dot_nt(qg, k[:, kcol * LANES:(kcol + 1) * LANES].astype(BF16))

    def pv(p, v):
        return _dot_nt(p, v.astype(BF16)) if kt else jnp.dot(p, v.astype(BF16), preferred_element_type=F32)
    na = qi if n_all is None else n_all
    lane = lax.broadcasted_iota(jnp.int32, (bq, LANES), 1)
    q = q_ref[0]
    zero = jnp.zeros_like(q)
    ones_sq = jnp.ones((LANES, LANES), BF16)

    def vmask(v, n):
        ln = lax.broadcasted_iota(jnp.int32, (n, LANES), 1)
        return (ln >= A_QK * v) & (ln < A_QK * (v + 1))

    if diff:
        qs = jnp.concatenate([jnp.where(vmask(v, bq), q, zero) for v in range(4)], axis=0)
        groups = [(qs, 0, [0, 0, 1, 1])]
    else:
        groups = [(q[:, 0:LANES], 0, [0]), (q[:, LANES:2 * LANES], 1, [0])]
    n_var = sum(len(bi) for _, _, bi in groups)

    def key_norm2(k, var):
        kk = k.astype(BF16).astype(F32)
        kk = kk * kk
        if kt:
            row = lax.broadcasted_iota(jnp.int32, kk.shape, 0)
            kk = jnp.where((row >= A_QK * var) & (row < A_QK * (var + 1)), kk, 0.0)
            n2 = jnp.sum(kk, axis=0, keepdims=True)
            return jnp.broadcast_to(jnp.max(n2, axis=1, keepdims=True), (1, LANES))
        n2 = jnp.dot(kk.astype(BF16), ones_sq, preferred_element_type=F32)
        return jnp.max(n2, axis=0, keepdims=True)

    def var_keys(k, var):
        return k if diff else k[:, var * LANES:(var + 1) * LANES]

    @pl.when(qi == 0)
    def _():
        if kt:
            vaug_ref[0:LANES, :] = va_ref[0].astype(BF16)
            vaug_ref[LANES:, :] = jnp.ones((LANES, t_all), BF16)
        else:
            vaug_ref[:, 0:LANES] = va_ref[0].astype(BF16)
            vaug_ref[:, LANES:] = jnp.ones((t_all, LANES), BF16)
        rows = min(t_all, 1024)
        for var in range(n_var):
            km = jnp.zeros((1, LANES), F32)
            for c in range(t_all // rows):
                km = jnp.maximum(km, key_norm2(var_keys(k_tile(c * rows, rows), var), var))
            kmax_ref[var * 8:(var + 1) * 8, :] = jnp.broadcast_to(km, (8, LANES))

    def stack_bias(ref, bias_idx):
        tiles = [ref[0] if ref.shape[0] == 1 else ref[i] for i in bias_idx]
        return tiles[0] if len(tiles) == 1 else jnp.concatenate(tiles, axis=0)

    kd = kd_ref[0]
    vd = vd_ref[0]
    has_sub = sub_bias or merged
    if has_sub:
        n_far = jnp.maximum(na - 1, 0) if merged else max(na - 1, 0)
        off_s = pl.multiple_of(n_far * bk, bk) if merged else n_far * bk
    else:
        n_far = na

    def diag_biases():
        return [stack_bias(bd_ref, bi) if diag_bias else None for _, _, bi in groups]

    def sub_biases():
        gate = jnp.where(na >= 1, 0.0, NEG_BIG)
        return [(stack_bias(bs_ref, bi) + gate) if sub_bias else jnp.full((len(bi) * bq, bk), gate, F32)
                for _, _, bi in groups]

    refs_g = []
    var = 0
    for qg, kcol, bi in groups:
        qf = qg.astype(F32)
        qn2 = jnp.sum(qf * qf, axis=-1, keepdims=True)
        parts = []
        for i, b_i in enumerate(bi):
            km2 = kmax_ref[var * 8:var * 8 + 1, :]
            if n_all is not None:
                km2 = jnp.maximum(km2, key_norm2(var_keys(kd, var), var))
            r = jnp.sqrt(qn2[i * bq:(i + 1) * bq] * km2) * (1.0 + 2.0 ** -6)
            if diff:
                r = r + bmax_ref[b_i, 0:1, :]
            parts.append(r)
            var += 1
        refs_g.append(parts[0] if len(parts) == 1 else jnp.concatenate(parts, axis=0))

    def probs(k, biases):
        out = []
        for (qg, kcol, _), mref, bias in zip(groups, refs_g, biases):
            s = qk(qg, k, kcol)
            if bias is not None:
                s = s + bias
            n = s.shape[1]
            if n >= LANES:
                cols = [jnp.exp2(s[:, c:c + LANES] - mref) for c in range(0, n, LANES)]
            else:
                cols = [jnp.exp2(s - mref[:, :n])]
            out.append((cols[0] if len(cols) == 1 else jnp.concatenate(cols, axis=1)).astype(BF16))
        return out[0] if len(out) == 1 else jnp.concatenate(out, axis=0)

    none_bias = [None] * len(groups)
    if kt:
        vd_aug = jnp.concatenate([vd, jnp.ones((LANES, bkd), BF16)], axis=0)
    else:
        vd_aug = jnp.concatenate([vd, jnp.ones((bkd, LANES), BF16)], axis=1)
    if merged:
        k_near = jnp.concatenate([k_tile(off_s, bk), kd], axis=key_axis - 1)
        v_near = jnp.concatenate([vaug_tile(off_s, bk), vd_aug], axis=key_axis - 1)
        bias_near = [jnp.concatenate([b_s, jnp.zeros_like(b_s) if b_d is None else b_d], axis=1)
                     for b_s, b_d in zip(sub_biases(), diag_biases())]
        acc_ref[...] = pv(probs(k_near, bias_near), v_near)

        def far_step(off, width):
            off = pl.multiple_of(off, bk)
            acc_ref[...] += pv(probs(k_tile(off, width), none_bias), vaug_tile(off, width))

        def far_loop(j, c):
            far_step(j * (4 * bk), 4 * bk)
            return c

        n_wide = n_far // 4
        lax.fori_loop(0, n_wide, far_loop, 0)
        rem = n_far - 4 * n_wide
        base = n_wide * (4 * bk)
        pl.when((rem & 2) != 0)(lambda: far_step(base, 2 * bk))
        pl.when((rem & 1) != 0)(lambda: far_step(base + (rem & 2) * bk, bk))
    else:
        acc_ref[...] = pv(probs(kd, diag_biases()), vd_aug)
        if has_sub:
            acc_ref[...] += pv(probs(k_tile(off_s, bk), sub_biases()), vaug_tile(off_s, bk))
        for off in range(0, n_far * bk, 4 * bk):
            width = min(4 * bk, n_far * bk - off)
            acc_ref[...] += pv(probs(k_tile(off, width), none_bias), vaug_tile(off, width))

    def finish(results):
        if diff:
            lp = lamp_ref[...]
            lam = (jnp.exp(jnp.sum(lp[0:1] * lp[1:2], axis=-1, keepdims=True))
                   - jnp.exp(jnp.sum(lp[2:3] * lp[3:4], axis=-1, keepdims=True)) + lam_init)
            outs = []
            for e in range(2):
                d = results[2 * e] - lam * results[2 * e + 1]
                own = (lane >= A_V * e) & (lane < A_V * (e + 1))
                ms = jnp.sum(jnp.where(own, d * d, 0.0), axis=-1, keepdims=True) * (1.0 / A_V)
                outs.append(d * lax.rsqrt(ms + EPS) * g_ref[...] * (1.0 - lam_init))
            out = jnp.where(lane < A_V, outs[0], outs[1])
        else:
            out = jnp.where(lane < C_V, results[0], results[1])
        o_ref[0] = out.astype(o_ref.dtype)

    acc = acc_ref[...]
    den = acc[:, LANES:]
    safe = (jnp.min(den) >= 2.0 ** -80) & (jnp.max(den) <= 2.0 ** 100)
    o_fast = acc[:, 0:LANES] / den
    finish([o_fast[i * bq:(i + 1) * bq] for i in range(n_var)])

    @pl.when(jnp.logical_not(safe))
    def _():
        def step(carries, k, v, biases):
            out = []
            for (qg, kcol, _), c, bias in zip(groups, carries, biases):
                s = qk(qg, k, kcol)
                out.append(_online_update(c, s if bias is None else s + bias, v, pv))
            return tuple(out)

        carries = tuple((jnp.full((len(bi) * bq, 1), NEG_BIG, F32), jnp.zeros((len(bi) * bq, 1), F32),
                         jnp.zeros((len(bi) * bq, LANES), F32)) for _, _, bi in groups)
        carries = step(carries, kd, vd, diag_biases())
        if has_sub:
            carries = step(carries, k_tile(off_s, bk), v_tile(off_s, bk), sub_biases())

        def body(j, cs):
            off = pl.multiple_of(j * bk, bk)
            return step(cs, k_tile(off, bk), v_tile(off, bk), none_bias)

        carries = lax.fori_loop(0, n_far, body, carries)
        results = []
        for (_, _, bi), (m, l, a) in zip(groups, carries):
            o = a / l
            results.extend(o[i * bq:(i + 1) * bq] for i in range(len(bi)))
        finish(results)


def _attention(q, k_all, v_all, k_new, v_new, bias_sub, bias_diag, *, diff, causal_blocks,
               lamp=None, g2=None, bias_max=None, lam_init=0.0, all_batch0=0):
    b, tq, _ = q.shape
    t_all = k_all.shape[2 if diff else 1]
    bk = ATT_BLOCK
    bq = min(ATT_BLOCK, tq)
    assert tq % bq == 0 and t_all % bk == 0
    if causal_blocks:
        assert bq == bk and t_all == tq
    else:
        assert tq == bq
    qw = LANES if diff else 2 * LANES
    n_pairs = A_HEADS // 2
    in_specs, args = [], []
    if diff:
        in_specs += [pl.BlockSpec(lamp.shape, lambda i, p, j: (0, 0)),
                     pl.BlockSpec(g2.shape, lambda i, p, j: (0, 0))]
        args += [lamp, g2]
    in_specs.append(pl.BlockSpec((1, bq, qw), lambda i, p, j: (i, j, p)))
    if diff:
        in_specs += [pl.BlockSpec((1, LANES, t_all), lambda i, p, j: (i + all_batch0, p, 0)),
                     pl.BlockSpec((1, LANES, t_all), lambda i, p, j: (i + all_batch0, p, 0)),
                     pl.BlockSpec((1, LANES, bq), lambda i, p, j: (i, p, j)),
                     pl.BlockSpec((1, LANES, bq), lambda i, p, j: (i, p, j))]
    else:
        in_specs += [pl.BlockSpec((1, t_all, qw), lambda i, p, j: (i, 0, p)),
                     pl.BlockSpec((1, t_all, LANES), lambda i, p, j: (i, 0, p)),
                     pl.BlockSpec((1, bq, qw), lambda i, p, j: (i, j, p)),
                     pl.BlockSpec((1, bq, LANES), lambda i, p, j: (i, j, p))]
    args += [q, k_all, v_all, k_new, v_new]
    for bias in (bias_sub, bias_diag):
        if bias is not None:
            if bias.shape[0] == 1:
                in_specs.append(pl.BlockSpec(bias.shape, lambda i, p, j: (0, 0, 0)))
            else:
                in_specs.append(pl.BlockSpec((2,) + bias.shape[1:], lambda i, p, j: (p, 0, 0)))
            args.append(bias)
    if diff:
        in_specs.append(pl.BlockSpec((2, 8, LANES), lambda i, p, j: (p, 0, 0)))
        args.append(bias_max)
    n_var = 4 if diff else 2
    kern = functools.partial(
        _attn_kernel, diff=diff, sub_bias=bias_sub is not None, diag_bias=bias_diag is not None,
        bq=bq, bk=bk, n_all=None if causal_blocks else t_all // bk, lam_init=lam_init)
    return pl.pallas_call(
        kern,
        grid=(b, n_pairs, tq // bq),
        in_specs=in_specs,
        out_specs=pl.BlockSpec((1, bq, LANES), lambda i, p, j: (i, j, p)),
        out_shape=jax.ShapeDtypeStruct((b, tq, n_pairs * LANES), BF16),
        scratch_shapes=[pltpu.VMEM((2 * LANES, t_all) if diff else (t_all, 2 * LANES), BF16),
                        pltpu.VMEM((n_var * 8, LANES), F32),
                        pltpu.VMEM((n_var * bq, 2 * LANES), F32)],
        compiler_params=_cparams(("arbitrary", "arbitrary", "arbitrary")),
        name="diff_attn" if diff else "mla_attn",
    )(*args)


def _postffn_kernel(a_ref, b_ref, c_ref, x_ref, mod_ref, wo_ref, gffn_ref, wg_ref, wv_ref, cw_ref,
                    cb_ref, wd_ref, hist_ref, gfin_ref, y_ref, conv_ref, carry_ref, act_ref, *, tm, final):
    t = pl.program_id(1)
    ffn = wg_ref.shape[1]
    cat = jnp.concatenate([a_ref[0], b_ref[0], c_ref[0]], axis=-1)
    mix = jnp.dot(cat, wo_ref[...], preferred_element_type=F32)
    gt1 = mod_ref[0, 2:3, :]
    sh2 = mod_ref[0, 3:4, :]
    sc2 = mod_ref[0, 4:5, :]
    gt2 = mod_ref[0, 5:6, :]
    x1 = x_ref[0] + gt1 * mix
    h2 = ((_rms(x1) * gffn_ref[...]) * (1.0 + sc2) + sh2).astype(BF16)

    @pl.when(t == 0)
    def _():
        carry_ref[...] = hist_ref[0]

    for c0 in range(0, ffn, FFN_CHUNK):
        sl = slice(c0, min(c0 + FFN_CHUNK, ffn))
        g = jnp.dot(h2, wg_ref[:, sl], preferred_element_type=F32)
        val = jnp.dot(h2, wv_ref[:, sl], preferred_element_type=F32)
        ext = jnp.concatenate([carry_ref[:, sl], g], axis=0)
        tail = g[tm - 8:, :]
        carry_ref[:, sl] = tail
        conv_ref[0, :, sl] = tail
        gc = (cw_ref[0:1, sl] * pltpu.roll(ext, 2, 0)[8:] + cw_ref[1:2, sl] * pltpu.roll(ext, 1, 0)[8:]
              + cw_ref[2:3, sl] * g + cb_ref[:, sl])
        act_ref[:, sl] = (gc * jax.nn.sigmoid(gc) * val).astype(BF16)
    f = jnp.dot(act_ref[...], wd_ref[...], preferred_element_type=F32)
    x2 = x1 + gt2 * f
    if final:
        x2 = _rms(x2) * gfin_ref[...]
    y_ref[0] = x2


def _postffn(a, bo, c, x, mod, wo, gffn, wg, wv, cw, cb, wd, hist8, gfin, final):
    b, t, d = x.shape
    f = wg.shape[1]
    tm = min(512, t)
    assert t % tm == 0 and tm >= 8
    tile = lambda n: pl.BlockSpec((1, tm, n), lambda i, j: (i, j, 0))

    def resident(shape):
        return pl.BlockSpec(shape, lambda i, j: (0,) * len(shape), pipeline_mode=pl.Buffered(1))

    return pl.pallas_call(
        functools.partial(_postffn_kernel, tm=tm, final=final),
        grid=(b, t // tm),
        in_specs=[tile(A_WIDTH), tile(B_WIDTH), tile(C_WIDTH), tile(d),
                  pl.BlockSpec((1, 6, d), lambda i, j: (i, 0, 0)),
                  resident(wo.shape), resident((1, d)), resident(wg.shape), resident(wv.shape),
                  resident((CONV_W, f)), resident((1, f)), resident(wd.shape),
                  pl.BlockSpec((1, 8, f), lambda i, j: (i, 0, 0)),
                  resident((1, d))],
        out_specs=[tile(d), pl.BlockSpec((1, 8, f), lambda i, j: (i, 0, 0))],
        out_shape=[jax.ShapeDtypeStruct((b, t, d), F32), jax.ShapeDtypeStruct((b, 8, f), F32)],
        scratch_shapes=[pltpu.VMEM((8, f), F32), pltpu.VMEM((tm, f), BF16)],
        compiler_params=_cparams(("arbitrary", "arbitrary")),
        name="postmix_ffn",
    )(a, bo, c, x, mod, wo, gffn, wg, wv, cw, cb, wd, hist8, gfin)


def _rot_cols(w):
    half = C_ROPE // 2
    return jnp.concatenate([-w[..., half:], w[..., :half]], axis=-1)


def _layer_weights(l, w_in, w_q_up, w_kv_up, pool_w, w_out, w_up):
    d = w_in.shape[1]
    wkr = w_in[l][:, OFF_CKR:]
    zpad = jnp.zeros((d, LANES - C_ROPE), F32)
    w1 = jnp.concatenate([w_in[l][:, :A_WIDTH], w_in[l][:, OFF_B:OFF_CKR], wkr, zpad, _rot_cols(wkr), zpad],
                         axis=1).astype(BF16)
    wkvt = w_in[l][:, A_WIDTH:OFF_B].T.astype(BF16)

    wq = w_q_up[l].reshape(C_Q_RANK, C_HEADS, C_NOPE + C_ROPE)
    nope, rope = wq[..., :C_NOPE], wq[..., C_NOPE:]
    zq = jnp.zeros((C_Q_RANK, C_HEADS, LANES - C_NOPE - C_ROPE), F32)
    main = jnp.concatenate([rope, nope, zq], axis=-1).reshape(C_Q_RANK, C_HEADS * LANES)
    rot = jnp.concatenate([_rot_cols(rope), jnp.zeros((C_Q_RANK, C_HEADS, LANES - C_ROPE), F32)],
                          axis=-1).reshape(C_Q_RANK, C_HEADS * LANES)
    wq2 = jnp.concatenate([main, rot], axis=1).astype(BF16)

    wkv = w_kv_up[l].reshape(C_KV_RANK, C_HEADS, C_NOPE + C_V)
    zk = jnp.zeros((C_KV_RANK, C_HEADS, C_ROPE), F32)
    kpart = jnp.concatenate([zk, wkv[..., :C_NOPE], zk], axis=-1).reshape(C_KV_RANK, C_HEADS * LANES)
    vpart = wkv[..., C_NOPE:].reshape(C_KV_RANK, C_WIDTH)
    wkv2 = jnp.concatenate([kpart, vpart], axis=1).astype(BF16)

    wp = jnp.zeros((B_WIDTH, B_WIDTH), F32)
    for g in range(B_GROUPS):
        wp = wp.at[g * B_CH:(g + 1) * B_CH, g * B_CH:(g + 1) * B_CH].set(pool_w[l, g])
    ffn = w_up.shape[2] // 2
    return dict(w1=w1, wkvt=wkvt, wq=wq2, wkv=wkv2, wp=wp.astype(BF16), wo=w_out[l].astype(BF16),
                wg=w_up[l][:, :ffn].astype(BF16), wv=w_up[l][:, ffn:].astype(BF16))


def _rope_tables(pos):
    half = C_ROPE // 2
    inv = 1.0 / (ROPE_BASE ** (jnp.arange(half, dtype=jnp.float32) / half))
    ang = pos.astype(jnp.float32)[:, None] * inv[None, :]
    cos, sin = jnp.cos(ang), jnp.sin(ang)
    n = pos.shape[0]
    z = jnp.zeros((n, LANES - C_ROPE), F32)
    cs = jnp.concatenate([cos, cos, z], axis=1)
    sn = jnp.concatenate([sin, sin, z], axis=1)
    csq = jnp.concatenate([cos, cos, jnp.ones((n, C_NOPE), F32), jnp.zeros((n, LANES - C_ROPE - C_NOPE), F32)], axis=1)
    return cs, sn, csq


def kernel(x_prompt, x_sample, c_prompt, c_sample, cache_a_k, cache_a_v, cache_c_latent, cache_c_krope,
           state_b_pool, state_ffn_conv, w_ada, b_ada, g_mix, w_in, lam_q1, lam_k1, lam_q2, lam_k2,
           a_subln_g, rel_bias, pool_w, pool_scale, c_q_norm_g, w_q_up, c_kv_norm_g, w_kv_up, w_out,
           g_ffn, w_up, conv_w, conv_b, w_down, g_final):
    depth = w_in.shape[0]
    bp, tp, d = x_prompt.shape
    bs, ts, _ = x_sample.shape
    past = cache_a_k.shape[2]
    ffn = conv_w.shape[2]
    blk = ATT_BLOCK

    mod_all = _ada(jnp.concatenate([c_prompt, c_sample], axis=0), w_ada, b_ada)
    mod_all = mod_all.reshape(depth, bp + bs, 6, d)

    r = np.arange(blk)
    bkt, madd = _tile_bucket_mask(blk + r, r)
    pb_sub, bias_max = _bias_tiles(rel_bias, bkt, madd)
    bkt, madd = _tile_bucket_mask(r, r)
    pb_diag, _ = _bias_tiles(rel_bias, bkt, madd)
    mla_diag = madd[None]
    rs = np.arange(ts)
    bkt, madd = _tile_bucket_mask(past + rs, past - blk + r)
    sb_sub, _ = _bias_tiles(rel_bias, bkt, madd)
    bkt, madd = _tile_bucket_mask(past + rs, past + rs)
    sb_diag, _ = _bias_tiles(rel_bias, bkt, madd)
    mla_sdiag = madd[None]

    tabs_p = _rope_tables(jnp.arange(tp, dtype=jnp.int32))
    tabs_s = _rope_tables(past + jnp.arange(ts, dtype=jnp.int32))
    cache_kt = jnp.transpose(cache_a_k, (0, 1, 3, 4, 2)).reshape(depth * bs, A_WIDTH, past)
    cache_vt = jnp.transpose(cache_a_v, (0, 1, 3, 4, 2)).reshape(depth * bs, A_WIDTH, past)
    zero_pool = jnp.zeros((bp, 16, B_WIDTH), F32)
    zero_conv = jnp.zeros((bp, 8, ffn), F32)

    hp, hs = x_prompt, x_sample
    new_p, new_s = [], []
    for l in range(depth):
        w = _layer_weights(l, w_in, w_q_up, w_kv_up, pool_w, w_out, w_up)
        lam_init = 0.8 - 0.6 * math.exp(-0.3 * l)
        lamp = jnp.stack([lam_q1[l], lam_k1[l], lam_q2[l], lam_k2[l]], axis=0)
        g2 = jnp.concatenate([a_subln_g[l], a_subln_g[l]])[None]
        wd = w_down[l].astype(BF16)
        last = l == depth - 1

        def run(x, mod, tabs, pos0, hist16, hist8, pasts):
            (aq, akt, avt, aktb, avtb, ub, bo, lat, kr, cq, ck, cv) = _premix(
                x, mod, g_mix[l][None], w["w1"], w["wkvt"], *tabs, c_q_norm_g[l][None], w["wq"],
                c_kv_norm_g[l][None], w["wkv"], w["wp"], pool_scale[l][None], hist16, pos0)
            if pasts is None:
                a_out = _attention(aq, aktb, avtb, aktb, avtb, pb_sub, pb_diag, diff=True, causal_blocks=True,
                                   lamp=lamp, g2=g2, bias_max=bias_max, lam_init=lam_init)
                c_out = _attention(cq, ck, cv, ck, cv, None, mla_diag, diff=False, causal_blocks=True)
            else:
                plat, pkr = pasts
                a_out = _attention(aq, cache_kt, cache_vt, aktb, avtb, sb_sub, sb_diag, diff=True,
                                   causal_blocks=False, lamp=lamp, g2=g2, bias_max=bias_max, lam_init=lam_init,
                                   all_batch0=l * bs)
                ckp, cvp = _kvpast(plat, pkr, w["wkv"])
                c_out = _attention(cq, ckp, cvp, ck, cv, None, mla_sdiag, diff=False, causal_blocks=False)
            y, conv8 = _postffn(a_out, bo, c_out, x, mod, w["wo"], g_ffn[l][None], w["wg"], w["wv"],
                                conv_w[l], conv_b[l][None], wd, hist8, g_final[None], last)
            bsz, tt = x.shape[0], x.shape[1]
            to_cache = lambda a: jnp.transpose(a.reshape(bsz, A_HEADS, A_V, tt), (0, 3, 1, 2))
            state = (to_cache(akt), to_cache(avt), lat, kr,
                     ub[:, tt - POOL_HIST:], conv8[:, 8 - (CONV_W - 1):])
            return y, state

        hp, st = run(hp, mod_all[l, :bp], tabs_p, 0, zero_pool, zero_conv, None)
        new_p.append(st)
        hist16 = jnp.pad(state_b_pool[l], ((0, 0), (1, 0), (0, 0)))
        hist8 = jnp.pad(state_ffn_conv[l], ((0, 0), (8 - (CONV_W - 1), 0), (0, 0)))
        hs, st = run(hs, mod_all[l, bp:], tabs_s, past, hist16, hist8, (cache_c_latent[l], cache_c_krope[l]))
        new_s.append(st)

    outs_p = [jnp.stack(z, axis=0) for z in zip(*new_p)]
    outs_s = [jnp.stack(z, axis=0) for z in zip(*new_s)]
    return (hp, hs, *outs_p, *outs_s)
```

```python
import functools
import math

import numpy as np
import jax
import jax.numpy as jnp
from jax import lax
from jax.experimental import pallas as pl
from jax.experimental.pallas import tpu as pltpu

F32 = jnp.float32
BF16 = jnp.bfloat16

CHUNK = 64
A_HEADS = 6
A_QK = 32
A_V = 64
B_GROUPS = 4
B_CH = 64
POOL_WINDOWS = (2, 4, 8, 16)
POOL_HIST = 15
C_HEADS = 6
C_NOPE = 64
C_ROPE = 32
C_V = 64
C_Q_RANK = 256
C_KV_RANK = 128
ROPE_BASE = 10000.0
REL_BUCKETS = 32
REL_MAX_DIST = 128
CONV_W = 3
EPS = 1e-6

A_WIDTH = A_HEADS * A_V
B_WIDTH = B_GROUPS * B_CH
C_WIDTH = C_HEADS * C_V
OFF_B = 3 * A_WIDTH
OFF_CQ = OFF_B + B_WIDTH
OFF_CKV = OFF_CQ + C_Q_RANK
OFF_CKR = OFF_CKV + C_KV_RANK

LANES = 128
ATT_BLOCK = 256
MLA_Q_BLOCK = 512
FFN_CHUNK = 1024
LOG2E = math.log2(math.e)
NEG_BIG = -1e30
VMEM_LIMIT = 56 * 1024 * 1024

U_AQ = 0
U_B = U_AQ + A_WIDTH
U_CQ = U_B + B_WIDTH
U_CKV = U_CQ + C_Q_RANK
U_KR = U_CKV + C_KV_RANK
N1 = U_KR + 2 * LANES


def _cparams(sem):
    return pltpu.CompilerParams(dimension_semantics=sem, vmem_limit_bytes=VMEM_LIMIT)


def _rms(x):
    return x * lax.rsqrt(jnp.mean(x * x, axis=-1, keepdims=True) + EPS)


def _ada_kernel(c_ref, w_ref, b_ref, o_ref):
    c = c_ref[...]
    s = (c * jax.nn.sigmoid(c)).astype(BF16)
    o_ref[0] = jnp.dot(s, w_ref[0].astype(BF16), preferred_element_type=F32) + b_ref[0]


def _ada(c_all, w_ada, b_ada):
    depth, d, n = w_ada.shape
    nb = c_all.shape[0]
    tn = 1024
    return pl.pallas_call(
        _ada_kernel,
        grid=(depth, n // tn),
        in_specs=[pl.BlockSpec((nb, d), lambda l, j: (0, 0)),
                  pl.BlockSpec((1, d, tn), lambda l, j: (l, 0, j)),
                  pl.BlockSpec((1, 1, tn), lambda l, j: (l, 0, j))],
        out_specs=pl.BlockSpec((1, nb, tn), lambda l, j: (l, 0, j)),
        out_shape=jax.ShapeDtypeStruct((depth, nb, n), F32),
        compiler_params=_cparams(("arbitrary", "arbitrary")),
        name="ada_mod",
    )(c_all, w_ada, b_ada.reshape(depth, 1, n))


def _bias_kernel(rb_ref, bucket_ref, madd_ref, o_ref, mx_ref):
    h = pl.program_id(0)
    bucket = bucket_ref[...]
    far = rb_ref[REL_BUCKETS // 2 - 1, h]
    val = jnp.zeros(bucket.shape, F32)
    top = far - far
    for b in range(REL_BUCKETS):
        val = jnp.where(bucket == b, rb_ref[b, h] - far, val)
        top = jnp.maximum(top, rb_ref[b, h] - far)
    o_ref[0] = val * LOG2E + madd_ref[...]
    mx_ref[0] = jnp.full(mx_ref.shape[1:], top * LOG2E, F32)


def _bias_tiles(rel_bias, bucket, madd):
    r, c = bucket.shape
    return pl.pallas_call(
        _bias_kernel,
        grid=(A_HEADS,),
        in_specs=[pl.BlockSpec(memory_space=pltpu.SMEM),
                  pl.BlockSpec((r, c), lambda h: (0, 0)),
                  pl.BlockSpec((r, c), lambda h: (0, 0))],
        out_specs=[pl.BlockSpec((1, r, c), lambda h: (h, 0, 0)),
                   pl.BlockSpec((1, 8, LANES), lambda h: (h, 0, 0))],
        out_shape=[jax.ShapeDtypeStruct((A_HEADS, r, c), F32),
                   jax.ShapeDtypeStruct((A_HEADS, 8, LANES), F32)],
        compiler_params=_cparams(("arbitrary",)),
        name="rel_bias_tiles",
    )(rel_bias, bucket, madd)


def _t5_bucket(rel):
    half = REL_BUCKETS // 2
    exact = half // 2
    ret = jnp.where(rel > 0, half, 0)
    n = jnp.abs(rel)
    nf = jnp.maximum(n, 1).astype(jnp.float32)
    large = exact + (jnp.log(nf / exact) / math.log(REL_MAX_DIST / exact) * (half - exact)).astype(jnp.int32)
    large = jnp.minimum(large, half - 1)
    return ret + jnp.where(n < exact, n, large)


def _tile_bucket_mask(q_pos, k_pos):
    q_pos = jnp.asarray(q_pos, jnp.int32)
    k_pos = jnp.asarray(k_pos, jnp.int32)
    bucket = _t5_bucket(k_pos[None, :] - q_pos[:, None]).astype(jnp.int32)
    visible = (k_pos[None, :] // CHUNK) <= (q_pos[:, None] // CHUNK)
    return bucket, jnp.where(visible, 0.0, NEG_BIG).astype(F32)


def _premix_kernel(x_ref, mod_ref, gmix_ref, w1_ref, wkvt_ref, cs_ref, sn_ref, csq_ref, gq_ref, wq_ref,
                   gkv_ref, wkv_ref, wp_ref, ps_ref, hist_ref,
                   aq_ref, akt_ref, avt_ref, aktb_ref, avtb_ref, ub_ref, bo_ref, lat_ref, kr_ref,
                   cq_ref, ck_ref, cv_ref, carry_ref, *, tm, pos0):
    t = pl.program_id(1)
    x = x_ref[0]
    sh1 = mod_ref[0, 0:1, :]
    sc1 = mod_ref[0, 1:2, :]
    h = ((_rms(x) * gmix_ref[...]) * (1.0 + sc1) + sh1).astype(BF16)
    u = jnp.dot(h, w1_ref[...], preferred_element_type=F32)
    kvt = lax.dot_general(wkvt_ref[...], h, (((1,), (1,)), ((), ())), preferred_element_type=F32)
    akt = kvt[0:A_WIDTH]
    avt = kvt[A_WIDTH:]
    akt_ref[0] = akt
    avt_ref[0] = avt
    aktb_ref[0] = akt.astype(BF16)
    avtb_ref[0] = avt.astype(BF16)
    aq_ref[0] = (u[:, U_AQ:U_B] * (A_QK ** -0.5 * LOG2E)).astype(BF16)

    ub = u[:, U_B:U_CQ]
    ub_ref[0] = ub

    @pl.when(t == 0)
    def _():
        carry_ref[...] = hist_ref[0]

    ext = jnp.concatenate([carry_ref[...], ub], axis=0)
    carry_ref[...] = ub[tm - 16:, :]
    s2 = ext + pltpu.roll(ext, 1, 0)
    s4 = s2 + pltpu.roll(s2, 2, 0)
    s8 = s4 + pltpu.roll(s4, 4, 0)
    s16 = s8 + pltpu.roll(s8, 8, 0)
    lane = lax.broadcasted_iota(jnp.int32, (tm, B_WIDTH), 1)
    grp = lane // B_CH
    tot = jnp.where(grp == 0, s2[16:], jnp.where(grp == 1, s4[16:], jnp.where(grp == 2, s8[16:], s16[16:])))
    win = jnp.where(grp == 0, 2, jnp.where(grp == 1, 4, jnp.where(grp == 2, 8, 16)))
    pos = pos0 + t * tm + lax.broadcasted_iota(jnp.int32, (tm, B_WIDTH), 0)
    cnt = jnp.minimum(pos + 1, win).astype(F32)
    m = tot / cnt - ub
    y = jnp.dot(m.astype(BF16), wp_ref[...], preferred_element_type=F32) * ps_ref[...]
    bo_ref[0] = y.astype(BF16)

    cs = cs_ref[...]
    sn = sn_ref[...]
    csq = csq_ref[...]
    qn = (_rms(u[:, U_CQ:U_CKV]) * gq_ref[...]).astype(BF16)
    q2 = jnp.dot(qn, wq_ref[...], preferred_element_type=F32)
    qscale = (C_NOPE + C_ROPE) ** -0.5 * LOG2E
    lat = _rms(u[:, U_CKV:U_KR]) * gkv_ref[...]
    lat_ref[0] = lat
    krp = u[:, U_KR:U_KR + LANES] * cs + u[:, U_KR + LANES:N1] * sn
    kr_ref[0] = krp[:, 0:C_ROPE]
    kv = jnp.dot(lat.astype(BF16), wkv_ref[...], preferred_element_type=F32)
    for hh in range(C_HEADS):
        sl = slice(hh * LANES, (hh + 1) * LANES)
        sr = slice((C_HEADS + hh) * LANES, (C_HEADS + hh + 1) * LANES)
        cq_ref[0, :, sl] = ((q2[:, sl] * csq + q2[:, sr] * sn) * qscale).astype(BF16)
        ck_ref[0, :, sl] = (kv[:, sl] + krp).astype(BF16)
    cv_ref[0] = kv[:, C_HEADS * LANES:].astype(BF16)


def _premix(x, mod, gmix, w1, wkvt, cs, sn, csq, gq, wq, gkv, wkv, wp, ps, hist16, pos0):
    b, t, d = x.shape
    tm = min(1024, t)
    assert t % tm == 0 and tm >= 16
    full = lambda shape: pl.BlockSpec(shape, lambda i, j: (0,) * len(shape))
    tile = lambda n: pl.BlockSpec((1, tm, n), lambda i, j: (i, j, 0))
    tile_t = pl.BlockSpec((1, A_WIDTH, tm), lambda i, j: (i, 0, j))
    tab = pl.BlockSpec((tm, LANES), lambda i, j: (j, 0))
    outs = [(A_WIDTH, BF16), (None, F32), (None, F32), (None, BF16), (None, BF16),
            (B_WIDTH, F32), (B_WIDTH, BF16), (C_KV_RANK, F32), (C_ROPE, F32),
            (C_HEADS * LANES, BF16), (C_HEADS * LANES, BF16), (C_WIDTH, BF16)]
    return pl.pallas_call(
        functools.partial(_premix_kernel, tm=tm, pos0=pos0),
        grid=(b, t // tm),
        in_specs=[tile(d),
                  pl.BlockSpec((1, 6, d), lambda i, j: (i, 0, 0)),
                  full((1, d)), full(w1.shape), full(wkvt.shape), tab, tab, tab,
                  full((1, C_Q_RANK)), full(wq.shape), full((1, C_KV_RANK)), full(wkv.shape),
                  full(wp.shape), full((1, B_WIDTH)),
                  pl.BlockSpec((1, 16, B_WIDTH), lambda i, j: (i, 0, 0))],
        out_specs=[tile_t if n is None else tile(n) for n, _ in outs],
        out_shape=[jax.ShapeDtypeStruct((b, A_WIDTH, t) if n is None else (b, t, n), dt) for n, dt in outs],
        scratch_shapes=[pltpu.VMEM((16, B_WIDTH), F32)],
        compiler_params=_cparams(("arbitrary", "arbitrary")),
        name="premix",
    )(x, mod, gmix, w1, wkvt, cs, sn, csq, gq, wq, gkv, wkv, wp, ps, hist16)


def _kvpast_kernel(lat_ref, kr_ref, wkv_ref, e_ref, ck_ref, cv_ref):
    kv = jnp.dot(lat_ref[0].astype(BF16), wkv_ref[...], preferred_element_type=F32)
    krp = jnp.dot(kr_ref[0].astype(BF16), e_ref[...], preferred_element_type=F32)
    for hh in range(C_HEADS):
        sl = slice(hh * LANES, (hh + 1) * LANES)
        ck_ref[0, :, sl] = (kv[:, sl] + krp).astype(BF16)
    cv_ref[0] = kv[:, C_HEADS * LANES:].astype(BF16)


def _kvpast(lat, kr, wkv):
    b, p, _ = lat.shape
    tm = min(512, p)
    assert p % tm == 0
    e = jnp.eye(C_ROPE, LANES, dtype=BF16)
    return pl.pallas_call(
        _kvpast_kernel,
        grid=(b, p // tm),
        in_specs=[pl.BlockSpec((1, tm, C_KV_RANK), lambda i, j: (i, j, 0)),
                  pl.BlockSpec((1, tm, C_ROPE), lambda i, j: (i, j, 0)),
                  pl.BlockSpec(wkv.shape, lambda i, j: (0, 0)),
                  pl.BlockSpec(e.shape, lambda i, j: (0, 0))],
        out_specs=[pl.BlockSpec((1, tm, C_HEADS * LANES), lambda i, j: (i, j, 0)),
                   pl.BlockSpec((1, tm, C_WIDTH), lambda i, j: (i, j, 0))],
        out_shape=[jax.ShapeDtypeStruct((b, p, C_HEADS * LANES), BF16),
                   jax.ShapeDtypeStruct((b, p, C_WIDTH), BF16)],
        compiler_params=_cparams(("arbitrary", "arbitrary")),
        name="mla_kv_past",
    )(lat, kr, wkv, e)


def _online_update(carry, s, v, pv):
    m, l, acc = carry
    m_new = jnp.maximum(m, jnp.max(s, axis=-1, keepdims=True))
    alpha = jnp.exp2(m - m_new)
    p = jnp.exp2(s - m_new)
    l = alpha * l + jnp.sum(p, axis=-1, keepdims=True)
    acc = alpha * acc + pv(p.astype(BF16), v)
    return m_new, l, acc


def _dot_nt(a, b):
    return lax.dot_general(a, b, (((1,), (1,)), ((), ())), preferred_element_type=F32)


def _attn_kernel(*refs, diff, sub_bias, diag_bias, bq, bk, n_all, lam_init):
    it = iter(refs)
    lamp_ref = next(it) if diff else None
    g_ref = next(it) if diff else None
    q_ref, ka_ref, va_ref, kd_ref, vd_ref = (next(it) for _ in range(5))
    bs_ref = next(it) if sub_bias else None
    bd_ref = next(it) if diag_bias else None
    bmax_ref = next(it) if diff else None
    o_ref = next(it)
    vaug_ref, kmax_ref, acc_ref = (next(it) for _ in range(3))
    causal = n_all is None
    merged = causal and sub_bias

    qi = pl.program_id(2)
    kt = diff
    key_axis = 2 if kt else 1
    t_all = ka_ref.shape[key_axis]
    bkd = kd_ref.shape[key_axis]

    def k_tile(off, width):
        return ka_ref[0, :, pl.ds(off, width)] if kt else ka_ref[0, pl.ds(off, width), :]

    def v_tile(off, width):
        return va_ref[0, :, pl.ds(off, width)] if kt else va_ref[0, pl.ds(off, width), :]

    def vaug_tile(off, width):
        return vaug_ref[:, pl.ds(off, width)] if kt else vaug_ref[pl.ds(off, width), :]

    def qk(qg, k, kcol):
        if kt:
            return jnp.dot(qg, k.astype(BF16), preferred_element_type=F32)
        return _dot_nt(qg, k[:, kcol * LANES:(kcol + 1) * LANES].astype(BF16))

    def pv(p, v):
        return _dot_nt(p, v.astype(BF16)) if kt else jnp.dot(p, v.astype(BF16), preferred_element_type=F32)
    na = qi * (bq // bk) if causal else n_all
    lane = lax.broadcasted_iota(jnp.int32, (bq, LANES), 1)
    q = q_ref[0]
    zero = jnp.zeros_like(q)
    ones_sq = jnp.ones((LANES, LANES), BF16)

    def vmask(v, n):
        ln = lax.broadcasted_iota(jnp.int32, (n, LANES), 1)
        return (ln >= A_QK * v) & (ln < A_QK * (v + 1))

    if diff:
        qs = jnp.concatenate([jnp.where(vmask(v, bq), q, zero) for v in range(4)], axis=0)
        groups = [(qs, 0, [0, 0, 1, 1])]
    else:
        groups = [(q[:, 0:LANES], 0, [0]), (q[:, LANES:2 * LANES], 1, [0])]
    n_var = sum(len(bi) for _, _, bi in groups)

    def key_norm2(k, var):
        kk = k.astype(BF16).astype(F32)
        kk = kk * kk
        if kt:
            row = lax.broadcasted_iota(jnp.int32, kk.shape, 0)
            kk = jnp.where((row >= A_QK * var) & (row < A_QK * (var + 1)), kk, 0.0)
            n2 = jnp.sum(kk, axis=0, keepdims=True)
            return jnp.broadcast_to(jnp.max(n2, axis=1, keepdims=True), (1, LANES))
        n2 = jnp.dot(kk.astype(BF16), ones_sq, preferred_element_type=F32)
        return jnp.max(n2, axis=0, keepdims=True)

    def var_keys(k, var):
        return k if diff else k[:, var * LANES:(var + 1) * LANES]

    @pl.when(qi == 0)
    def _():
        if kt:
            vaug_ref[0:LANES, :] = va_ref[0].astype(BF16)
            vaug_ref[LANES:, :] = jnp.ones((LANES, t_all), BF16)
        else:
            vaug_ref[:, 0:LANES] = va_ref[0].astype(BF16)
            vaug_ref[:, LANES:] = jnp.ones((t_all, LANES), BF16)
        rows = min(t_all, 1024)
        for var in range(n_var):
            km = jnp.zeros((1, LANES), F32)
            for c in range(t_all // rows):
                km = jnp.maximum(km, key_norm2(var_keys(k_tile(c * rows, rows), var), var))
            kmax_ref[var * 8:(var + 1) * 8, :] = jnp.broadcast_to(km, (8, LANES))

    def stack_bias(ref, bias_idx):
        tiles = [ref[0] if ref.shape[0] == 1 else ref[i] for i in bias_idx]
        return tiles[0] if len(tiles) == 1 else jnp.concatenate(tiles, axis=0)

    kd = kd_ref[0]
    vd = vd_ref[0]
    has_sub = sub_bias
    if has_sub:
        n_far = jnp.maximum(na - 1, 0) if causal else max(na - 1, 0)
        off_s = pl.multiple_of(n_far * bk, bk) if causal else n_far * bk
    else:
        n_far = na

    def diag_biases():
        return [stack_bias(bd_ref, bi) if diag_bias else None for _, _, bi in groups]

    def sub_biases():
        gate = jnp.where(na >= 1, 0.0, NEG_BIG)
        return [stack_bias(bs_ref, bi) + gate for _, _, bi in groups]

    refs_g = []
    var = 0
    for qg, kcol, bi in groups:
        qf = qg.astype(F32)
        qn2 = jnp.sum(qf * qf, axis=-1, keepdims=True)
        parts = []
        for i, b_i in enumerate(bi):
            km2 = kmax_ref[var * 8:var * 8 + 1, :]
            if n_all is not None:
                km2 = jnp.maximum(km2, key_norm2(var_keys(kd, var), var))
            r = jnp.sqrt(qn2[i * bq:(i + 1) * bq] * km2) * (1.0 + 2.0 ** -6)
            if diff:
                r = r + bmax_ref[b_i, 0:1, :]
            parts.append(r)
            var += 1
        refs_g.append(parts[0] if len(parts) == 1 else jnp.concatenate(parts, axis=0))

    def probs(k, biases):
        out = []
        for (qg, kcol, _), mref, bias in zip(groups, refs_g, biases):
            s = qk(qg, k, kcol)
            if bias is not None:
                s = s + bias
            n = s.shape[1]
            if n >= LANES:
                cols = [jnp.exp2(s[:, c:c + LANES] - mref) for c in range(0, n, LANES)]
            else:
                cols = [jnp.exp2(s - mref[:, :n])]
            out.append((cols[0] if len(cols) == 1 else jnp.concatenate(cols, axis=1)).astype(BF16))
        return out[0] if len(out) == 1 else jnp.concatenate(out, axis=0)

    none_bias = [None] * len(groups)
    if kt:
        vd_aug = jnp.concatenate([vd, jnp.ones((LANES, bkd), BF16)], axis=0)
    else:
        vd_aug = jnp.concatenate([vd, jnp.ones((bkd, LANES), BF16)], axis=1)
    if merged:
        k_near = jnp.concatenate([k_tile(off_s, bk), kd], axis=key_axis - 1)
        v_near = jnp.concatenate([vaug_tile(off_s, bk), vd_aug], axis=key_axis - 1)
        bias_near = [jnp.concatenate([b_s, jnp.zeros_like(b_s) if b_d is None else b_d], axis=1)
                     for b_s, b_d in zip(sub_biases(), diag_biases())]
        acc_ref[...] = pv(probs(k_near, bias_near), v_near)
    else:
        acc_ref[...] = pv(probs(kd, diag_biases()), vd_aug)
        if has_sub:
            acc_ref[...] += pv(probs(k_tile(off_s, bk), sub_biases()), vaug_tile(off_s, bk))
    if causal:
        def far_step(off, width):
            off = pl.multiple_of(off, bk)
            acc_ref[...] += pv(probs(k_tile(off, width), none_bias), vaug_tile(off, width))

        def far_loop(j, c):
            far_step(j * (4 * bk), 4 * bk)
            return c

        n_wide = n_far // 4
        lax.fori_loop(0, n_wide, far_loop, 0)
        rem = n_far - 4 * n_wide
        base = n_wide * (4 * bk)
        pl.when((rem & 2) != 0)(lambda: far_step(base, 2 * bk))
        pl.when((rem & 1) != 0)(lambda: far_step(base + (rem & 2) * bk, bk))
    else:
        for off in range(0, n_far * bk, 4 * bk):
            width = min(4 * bk, n_far * bk - off)
            acc_ref[...] += pv(probs(k_tile(off, width), none_bias), vaug_tile(off, width))

    def finish(results):
        if diff:
            lp = lamp_ref[...]
            lam = (jnp.exp(jnp.sum(lp[0:1] * lp[1:2], axis=-1, keepdims=True))
                   - jnp.exp(jnp.sum(lp[2:3] * lp[3:4], axis=-1, keepdims=True)) + lam_init)
            outs = []
            for e in range(2):
                d = results[2 * e] - lam * results[2 * e + 1]
                own = (lane >= A_V * e) & (lane < A_V * (e + 1))
                ms = jnp.sum(jnp.where(own, d * d, 0.0), axis=-1, keepdims=True) * (1.0 / A_V)
                outs.append(d * lax.rsqrt(ms + EPS) * g_ref[...] * (1.0 - lam_init))
            out = jnp.where(lane < A_V, outs[0], outs[1])
        else:
            out = jnp.where(lane < C_V, results[0], results[1])
        o_ref[0] = out.astype(o_ref.dtype)

    acc = acc_ref[...]
    den = acc[:, LANES:]
    safe = (jnp.min(den) >= 2.0 ** -80) & (jnp.max(den) <= 2.0 ** 100)
    o_fast = acc[:, 0:LANES] / den
    finish([o_fast[i * bq:(i + 1) * bq] for i in range(n_var)])

    @pl.when(jnp.logical_not(safe))
    def _():
        def step(carries, k, v, biases):
            out = []
            for (qg, kcol, _), c, bias in zip(groups, carries, biases):
                s = qk(qg, k, kcol)
                out.append(_online_update(c, s if bias is None else s + bias, v, pv))
            return tuple(out)

        carries = tuple((jnp.full((len(bi) * bq, 1), NEG_BIG, F32), jnp.zeros((len(bi) * bq, 1), F32),
                         jnp.zeros((len(bi) * bq, LANES), F32)) for _, _, bi in groups)
        carries = step(carries, kd, vd, diag_biases())
        if has_sub:
            carries = step(carries, k_tile(off_s, bk), v_tile(off_s, bk), sub_biases())

        def body(j, cs):
            off = pl.multiple_of(j * bk, bk)
            return step(cs, k_tile(off, bk), v_tile(off, bk), none_bias)

        carries = lax.fori_loop(0, n_far, body, carries)
        results = []
        for (_, _, bi), (m, l, a) in zip(groups, carries):
            o = a / l
            results.extend(o[i * bq:(i + 1) * bq] for i in range(len(bi)))
        finish(results)


def _attention(q, k_all, v_all, k_new, v_new, bias_sub, bias_diag, *, diff, causal_blocks,
               lamp=None, g2=None, bias_max=None, lam_init=0.0, all_batch0=0):
    b, tq, _ = q.shape
    t_all = k_all.shape[2 if diff else 1]
    bk = ATT_BLOCK
    bq = min(ATT_BLOCK if diff or not causal_blocks else MLA_Q_BLOCK, tq)
    assert tq % bq == 0 and t_all % bk == 0
    if causal_blocks:
        assert bq % bk == 0 and t_all == tq
    else:
        assert tq == bq
    qw = LANES if diff else 2 * LANES
    n_pairs = A_HEADS // 2
    in_specs, args = [], []
    if diff:
        in_specs += [pl.BlockSpec(lamp.shape, lambda i, p, j: (0, 0)),
                     pl.BlockSpec(g2.shape, lambda i, p, j: (0, 0))]
        args += [lamp, g2]
    in_specs.append(pl.BlockSpec((1, bq, qw), lambda i, p, j: (i, j, p)))
    if diff:
        in_specs += [pl.BlockSpec((1, LANES, t_all), lambda i, p, j: (i + all_batch0, p, 0)),
                     pl.BlockSpec((1, LANES, t_all), lambda i, p, j: (i + all_batch0, p, 0)),
                     pl.BlockSpec((1, LANES, bq), lambda i, p, j: (i, p, j)),
                     pl.BlockSpec((1, LANES, bq), lambda i, p, j: (i, p, j))]
    else:
        in_specs += [pl.BlockSpec((1, t_all, qw), lambda i, p, j: (i, 0, p)),
                     pl.BlockSpec((1, t_all, LANES), lambda i, p, j: (i, 0, p)),
                     pl.BlockSpec((1, bq, qw), lambda i, p, j: (i, j, p)),
                     pl.BlockSpec((1, bq, LANES), lambda i, p, j: (i, j, p))]
    args += [q, k_all, v_all, k_new, v_new]
    for bias in (bias_sub, bias_diag):
        if bias is not None:
            if bias.shape[0] == 1:
                in_specs.append(pl.BlockSpec(bias.shape, lambda i, p, j: (0, 0, 0)))
            else:
                in_specs.append(pl.BlockSpec((2,) + bias.shape[1:], lambda i, p, j: (p, 0, 0)))
            args.append(bias)
    if diff:
        in_specs.append(pl.BlockSpec((2, 8, LANES), lambda i, p, j: (p, 0, 0)))
        args.append(bias_max)
    n_var = 4 if diff else 2
    kern = functools.partial(
        _attn_kernel, diff=diff, sub_bias=bias_sub is not None, diag_bias=bias_diag is not None,
        bq=bq, bk=bk, n_all=None if causal_blocks else t_all // bk, lam_init=lam_init)
    return pl.pallas_call(
        kern,
        grid=(b, n_pairs, tq // bq),
        in_specs=in_specs,
        out_specs=pl.BlockSpec((1, bq, LANES), lambda i, p, j: (i, j, p)),
        out_shape=jax.ShapeDtypeStruct((b, tq, n_pairs * LANES), BF16),
        scratch_shapes=[pltpu.VMEM((2 * LANES, t_all) if diff else (t_all, 2 * LANES), BF16),
                        pltpu.VMEM((n_var * 8, LANES), F32),
                        pltpu.VMEM((n_var * bq, 2 * LANES), F32)],
        compiler_params=_cparams(("arbitrary", "arbitrary", "arbitrary")),
        name="diff_attn" if diff else "mla_attn",
    )(*args)


def _postffn_kernel(a_ref, b_ref, c_ref, x_ref, mod_ref, wo_ref, gffn_ref, wg_ref, wv_ref, cw_ref,
                    cb_ref, wd_ref, hist_ref, gfin_ref, y_ref, conv_ref, carry_ref, act_ref, *, tm, final):
    t = pl.program_id(1)
    ffn = wg_ref.shape[1]
    cat = jnp.concatenate([a_ref[0], b_ref[0], c_ref[0]], axis=-1)
    mix = jnp.dot(cat, wo_ref[...], preferred_element_type=F32)
    gt1 = mod_ref[0, 2:3, :]
    sh2 = mod_ref[0, 3:4, :]
    sc2 = mod_ref[0, 4:5, :]
    gt2 = mod_ref[0, 5:6, :]
    x1 = x_ref[0] + gt1 * mix
    h2 = ((_rms(x1) * gffn_ref[...]) * (1.0 + sc2) + sh2).astype(BF16)

    @pl.when(t == 0)
    def _():
        carry_ref[...] = hist_ref[0]

    for c0 in range(0, ffn, FFN_CHUNK):
        sl = slice(c0, min(c0 + FFN_CHUNK, ffn))
        g = jnp.dot(h2, wg_ref[:, sl], preferred_element_type=F32)
        val = jnp.dot(h2, wv_ref[:, sl], preferred_element_type=F32)
        ext = jnp.concatenate([carry_ref[:, sl], g], axis=0)
        tail = g[tm - 8:, :]
        carry_ref[:, sl] = tail
        conv_ref[0, :, sl] = tail
        gc = (cw_ref[0:1, sl] * pltpu.roll(ext, 2, 0)[8:] + cw_ref[1:2, sl] * pltpu.roll(ext, 1, 0)[8:]
              + cw_ref[2:3, sl] * g + cb_ref[:, sl])
        act_ref[:, sl] = (gc * jax.nn.sigmoid(gc) * val).astype(BF16)
    f = jnp.dot(act_ref[...], wd_ref[...], preferred_element_type=F32)
    x2 = x1 + gt2 * f
    if final:
        x2 = _rms(x2) * gfin_ref[...]
    y_ref[0] = x2


def _postffn(a, bo, c, x, mod, wo, gffn, wg, wv, cw, cb, wd, hist8, gfin, final):
    b, t, d = x.shape
    f = wg.shape[1]
    tm = min(512, t)
    assert t % tm == 0 and tm >= 8
    tile = lambda n: pl.BlockSpec((1, tm, n), lambda i, j: (i, j, 0))

    def resident(shape):
        return pl.BlockSpec(shape, lambda i, j: (0,) * len(shape), pipeline_mode=pl.Buffered(1))

    return pl.pallas_call(
        functools.partial(_postffn_kernel, tm=tm, final=final),
        grid=(b, t // tm),
        in_specs=[tile(A_WIDTH), tile(B_WIDTH), tile(C_WIDTH), tile(d),
                  pl.BlockSpec((1, 6, d), lambda i, j: (i, 0, 0)),
                  resident(wo.shape), resident((1, d)), resident(wg.shape), resident(wv.shape),
                  resident((CONV_W, f)), resident((1, f)), resident(wd.shape),
                  pl.BlockSpec((1, 8, f), lambda i, j: (i, 0, 0)),
                  resident((1, d))],
        out_specs=[tile(d), pl.BlockSpec((1, 8, f), lambda i, j: (i, 0, 0))],
        out_shape=[jax.ShapeDtypeStruct((b, t, d), F32), jax.ShapeDtypeStruct((b, 8, f), F32)],
        scratch_shapes=[pltpu.VMEM((8, f), F32), pltpu.VMEM((tm, f), BF16)],
        compiler_params=_cparams(("arbitrary", "arbitrary")),
        name="postmix_ffn",
    )(a, bo, c, x, mod, wo, gffn, wg, wv, cw, cb, wd, hist8, gfin)


def _rot_cols(w):
    half = C_ROPE // 2
    return jnp.concatenate([-w[..., half:], w[..., :half]], axis=-1)


def _layer_weights(l, w_in, w_q_up, w_kv_up, pool_w, w_out, w_up):
    d = w_in.shape[1]
    wkr = w_in[l][:, OFF_CKR:]
    zpad = jnp.zeros((d, LANES - C_ROPE), F32)
    w1 = jnp.concatenate([w_in[l][:, :A_WIDTH], w_in[l][:, OFF_B:OFF_CKR], wkr, zpad, _rot_cols(wkr), zpad],
                         axis=1).astype(BF16)
    wkvt = w_in[l][:, A_WIDTH:OFF_B].T.astype(BF16)

    wq = w_q_up[l].reshape(C_Q_RANK, C_HEADS, C_NOPE + C_ROPE)
    nope, rope = wq[..., :C_NOPE], wq[..., C_NOPE:]
    zq = jnp.zeros((C_Q_RANK, C_HEADS, LANES - C_NOPE - C_ROPE), F32)
    main = jnp.concatenate([rope, nope, zq], axis=-1).reshape(C_Q_RANK, C_HEADS * LANES)
    rot = jnp.concatenate([_rot_cols(rope), jnp.zeros((C_Q_RANK, C_HEADS, LANES - C_ROPE), F32)],
                          axis=-1).reshape(C_Q_RANK, C_HEADS * LANES)
    wq2 = jnp.concatenate([main, rot], axis=1).astype(BF16)

    wkv = w_kv_up[l].reshape(C_KV_RANK, C_HEADS, C_NOPE + C_V)
    zk = jnp.zeros((C_KV_RANK, C_HEADS, C_ROPE), F32)
    kpart = jnp.concatenate([zk, wkv[..., :C_NOPE], zk], axis=-1).reshape(C_KV_RANK, C_HEADS * LANES)
    vpart = wkv[..., C_NOPE:].reshape(C_KV_RANK, C_WIDTH)
    wkv2 = jnp.concatenate([kpart, vpart], axis=1).astype(BF16)

    wp = jnp.zeros((B_WIDTH, B_WIDTH), F32)
    for g in range(B_GROUPS):
        wp = wp.at[g * B_CH:(g + 1) * B_CH, g * B_CH:(g + 1) * B_CH].set(pool_w[l, g])
    ffn = w_up.shape[2] // 2
    return dict(w1=w1, wkvt=wkvt, wq=wq2, wkv=wkv2, wp=wp.astype(BF16), wo=w_out[l].astype(BF16),
                wg=w_up[l][:, :ffn].astype(BF16), wv=w_up[l][:, ffn:].astype(BF16))


def _rope_tables(pos):
    half = C_ROPE // 2
    inv = 1.0 / (ROPE_BASE ** (jnp.arange(half, dtype=jnp.float32) / half))
    ang = pos.astype(jnp.float32)[:, None] * inv[None, :]
    cos, sin = jnp.cos(ang), jnp.sin(ang)
    n = pos.shape[0]
    z = jnp.zeros((n, LANES - C_ROPE), F32)
    cs = jnp.concatenate([cos, cos, z], axis=1)
    sn = jnp.concatenate([sin, sin, z], axis=1)
    csq = jnp.concatenate([cos, cos, jnp.ones((n, C_NOPE), F32), jnp.zeros((n, LANES - C_ROPE - C_NOPE), F32)], axis=1)
    return cs, sn, csq


def kernel(x_prompt, x_sample, c_prompt, c_sample, cache_a_k, cache_a_v, cache_c_latent, cache_c_krope,
           state_b_pool, state_ffn_conv, w_ada, b_ada, g_mix, w_in, lam_q1, lam_k1, lam_q2, lam_k2,
           a_subln_g, rel_bias, pool_w, pool_scale, c_q_norm_g, w_q_up, c_kv_norm_g, w_kv_up, w_out,
           g_ffn, w_up, conv_w, conv_b, w_down, g_final):
    depth = w_in.shape[0]
    bp, tp, d = x_prompt.shape
    bs, ts, _ = x_sample.shape
    past = cache_a_k.shape[2]
    ffn = conv_w.shape[2]
    blk = ATT_BLOCK

    mod_all = _ada(jnp.concatenate([c_prompt, c_sample], axis=0), w_ada, b_ada)
    mod_all = mod_all.reshape(depth, bp + bs, 6, d)

    r = np.arange(blk)
    bkt, madd = _tile_bucket_mask(blk + r, r)
    pb_sub, bias_max = _bias_tiles(rel_bias, bkt, madd)
    bkt, madd = _tile_bucket_mask(r, r)
    pb_diag, _ = _bias_tiles(rel_bias, bkt, madd)
    rm = np.arange(min(MLA_Q_BLOCK, tp))
    mla_diag = _tile_bucket_mask(rm, rm)[1][None]
    rs = np.arange(ts)
    bkt, madd = _tile_bucket_mask(past + rs, past - blk + r)
    sb_sub, _ = _bias_tiles(rel_bias, bkt, madd)
    bkt, madd = _tile_bucket_mask(past + rs, past + rs)
    sb_diag, _ = _bias_tiles(rel_bias, bkt, madd)
    mla_sdiag = madd[None]

    tabs_p = _rope_tables(jnp.arange(tp, dtype=jnp.int32))
    tabs_s = _rope_tables(past + jnp.arange(ts, dtype=jnp.int32))
    cache_kt = jnp.transpose(cache_a_k, (0, 1, 3, 4, 2)).reshape(depth * bs, A_WIDTH, past)
    cache_vt = jnp.transpose(cache_a_v, (0, 1, 3, 4, 2)).reshape(depth * bs, A_WIDTH, past)
    zero_pool = jnp.zeros((bp, 16, B_WIDTH), F32)
    zero_conv = jnp.zeros((bp, 8, ffn), F32)

    hp, hs = x_prompt, x_sample
    new_p, new_s = [], []
    for l in range(depth):
        w = _layer_weights(l, w_in, w_q_up, w_kv_up, pool_w, w_out, w_up)
        lam_init = 0.8 - 0.6 * math.exp(-0.3 * l)
        lamp = jnp.stack([lam_q1[l], lam_k1[l], lam_q2[l], lam_k2[l]], axis=0)
        g2 = jnp.concatenate([a_subln_g[l], a_subln_g[l]])[None]
        wd = w_down[l].astype(BF16)
        last = l == depth - 1

        def run(x, mod, tabs, pos0, hist16, hist8, pasts):
            (aq, akt, avt, aktb, avtb, ub, bo, lat, kr, cq, ck, cv) = _premix(
                x, mod, g_mix[l][None], w["w1"], w["wkvt"], *tabs, c_q_norm_g[l][None], w["wq"],
                c_kv_norm_g[l][None], w["wkv"], w["wp"], pool_scale[l][None], hist16, pos0)
            if pasts is None:
                a_out = _attention(aq, aktb, avtb, aktb, avtb, pb_sub, pb_diag, diff=True, causal_blocks=True,
                                   lamp=lamp, g2=g2, bias_max=bias_max, lam_init=lam_init)
                c_out = _attention(cq, ck, cv, ck, cv, None, mla_diag, diff=False, causal_blocks=True)
            else:
                plat, pkr = pasts
                a_out = _attention(aq, cache_kt, cache_vt, aktb, avtb, sb_sub, sb_diag, diff=True,
                                   causal_blocks=False, lamp=lamp, g2=g2, bias_max=bias_max, lam_init=lam_init,
                                   all_batch0=l * bs)
                ckp, cvp = _kvpast(plat, pkr, w["wkv"])
                c_out = _attention(cq, ckp, cvp, ck, cv, None, mla_sdiag, diff=False, causal_blocks=False)
            y, conv8 = _postffn(a_out, bo, c_out, x, mod, w["wo"], g_ffn[l][None], w["wg"], w["wv"],
                                conv_w[l], conv_b[l][None], wd, hist8, g_final[None], last)
            bsz, tt = x.shape[0], x.shape[1]
            to_cache = lambda a: jnp.transpose(a.reshape(bsz, A_HEADS, A_V, tt), (0, 3, 1, 2))
            state = (to_cache(akt), to_cache(avt), lat, kr,
                     ub[:, tt - POOL_HIST:], conv8[:, 8 - (CONV_W - 1):])
            return y, state

        hp, st = run(hp, mod_all[l, :bp], tabs_p, 0, zero_pool, zero_conv, None)
        new_p.append(st)
        hist16 = jnp.pad(state_b_pool[l], ((0, 0), (1, 0), (0, 0)))
        hist8 = jnp.pad(state_ffn_conv[l], ((0, 0), (8 - (CONV_W - 1), 0), (0, 0)))
        hs, st = run(hs, mod_all[l, bp:], tabs_s, past, hist16, hist8, (cache_c_latent[l], cache_c_krope[l]))
        new_s.append(st)

    outs_p = [jnp.stack(z, axis=0) for z in zip(*new_p)]
    outs_s = [jnp.stack(z, axis=0) for z in zip(*new_s)]
    return (hp, hs, *outs_p, *outs_s)
```

```python
import functools
import math

import numpy as np
import jax
import jax.numpy as jnp
from jax import lax
from jax.experimental import pallas as pl
from jax.experimental.pallas import tpu as pltpu

F32 = jnp.float32
BF16 = jnp.bfloat16

CHUNK = 64
A_HEADS = 6
A_QK = 32
A_V = 64
B_GROUPS = 4
B_CH = 64
POOL_WINDOWS = (2, 4, 8, 16)
POOL_HIST = 15
C_HEADS = 6
C_NOPE = 64
C_ROPE = 32
C_V = 64
C_Q_RANK = 256
C_KV_RANK = 128
ROPE_BASE = 10000.0
REL_BUCKETS = 32
REL_MAX_DIST = 128
CONV_W = 3
EPS = 1e-6

A_WIDTH = A_HEADS * A_V
B_WIDTH = B_GROUPS * B_CH
C_WIDTH = C_HEADS * C_V
OFF_B = 3 * A_WIDTH
OFF_CQ = OFF_B + B_WIDTH
OFF_CKV = OFF_CQ + C_Q_RANK
OFF_CKR = OFF_CKV + C_KV_RANK

LANES = 128
ATT_BLOCK = 256
CAUSAL_Q_BLOCK = 512
FFN_CHUNK = 1024
LOG2E = math.log2(math.e)
NEG_BIG = -1e30
VMEM_LIMIT = 56 * 1024 * 1024

U_AQ = 0
U_B = U_AQ + A_WIDTH
U_CQ = U_B + B_WIDTH
U_CKV = U_CQ + C_Q_RANK
U_KR = U_CKV + C_KV_RANK
N1 = U_KR + 2 * LANES


def _cparams(sem):
    return pltpu.CompilerParams(dimension_semantics=sem, vmem_limit_bytes=VMEM_LIMIT)


def _rms(x):
    return x * lax.rsqrt(jnp.mean(x * x, axis=-1, keepdims=True) + EPS)


def _ada_kernel(c_ref, w_ref, b_ref, o_ref):
    c = c_ref[...]
    s = (c * jax.nn.sigmoid(c)).astype(BF16)
    o_ref[0] = jnp.dot(s, w_ref[0].astype(BF16), preferred_element_type=F32) + b_ref[0]


def _ada(c_all, w_ada, b_ada):
    depth, d, n = w_ada.shape
    nb = c_all.shape[0]
    tn = 1024
    return pl.pallas_call(
        _ada_kernel,
        grid=(depth, n // tn),
        in_specs=[pl.BlockSpec((nb, d), lambda l, j: (0, 0)),
                  pl.BlockSpec((1, d, tn), lambda l, j: (l, 0, j)),
                  pl.BlockSpec((1, 1, tn), lambda l, j: (l, 0, j))],
        out_specs=pl.BlockSpec((1, nb, tn), lambda l, j: (l, 0, j)),
        out_shape=jax.ShapeDtypeStruct((depth, nb, n), F32),
        compiler_params=_cparams(("arbitrary", "arbitrary")),
        name="ada_mod",
    )(c_all, w_ada, b_ada.reshape(depth, 1, n))


def _bias_kernel(rb_ref, bucket_ref, madd_ref, o_ref, mx_ref):
    h = pl.program_id(0)
    bucket = bucket_ref[...]
    far = rb_ref[REL_BUCKETS // 2 - 1, h]
    val = jnp.zeros(bucket.shape, F32)
    top = far - far
    for b in range(REL_BUCKETS):
        val = jnp.where(bucket == b, rb_ref[b, h] - far, val)
        top = jnp.maximum(top, rb_ref[b, h] - far)
    o_ref[0] = val * LOG2E + madd_ref[...]
    mx_ref[0] = jnp.full(mx_ref.shape[1:], top * LOG2E, F32)


def _bias_tiles(rel_bias, bucket, madd):
    r, c = bucket.shape
    return pl.pallas_call(
        _bias_kernel,
        grid=(A_HEADS,),
        in_specs=[pl.BlockSpec(memory_space=pltpu.SMEM),
                  pl.BlockSpec((r, c), lambda h: (0, 0)),
                  pl.BlockSpec((r, c), lambda h: (0, 0))],
        out_specs=[pl.BlockSpec((1, r, c), lambda h: (h, 0, 0)),
                   pl.BlockSpec((1, 8, LANES), lambda h: (h, 0, 0))],
        out_shape=[jax.ShapeDtypeStruct((A_HEADS, r, c), F32),
                   jax.ShapeDtypeStruct((A_HEADS, 8, LANES), F32)],
        compiler_params=_cparams(("arbitrary",)),
        name="rel_bias_tiles",
    )(rel_bias, bucket, madd)


def _t5_bucket(rel):
    half = REL_BUCKETS // 2
    exact = half // 2
    ret = jnp.where(rel > 0, half, 0)
    n = jnp.abs(rel)
    nf = jnp.maximum(n, 1).astype(jnp.float32)
    large = exact + (jnp.log(nf / exact) / math.log(REL_MAX_DIST / exact) * (half - exact)).astype(jnp.int32)
    large = jnp.minimum(large, half - 1)
    return ret + jnp.where(n < exact, n, large)


def _tile_bucket_mask(q_pos, k_pos):
    q_pos = jnp.asarray(q_pos, jnp.int32)
    k_pos = jnp.asarray(k_pos, jnp.int32)
    bucket = _t5_bucket(k_pos[None, :] - q_pos[:, None]).astype(jnp.int32)
    visible = (k_pos[None, :] // CHUNK) <= (q_pos[:, None] // CHUNK)
    return bucket, jnp.where(visible, 0.0, NEG_BIG).astype(F32)


def _premix_kernel(x_ref, mod_ref, gmix_ref, w1_ref, wkvt_ref, cs_ref, sn_ref, csq_ref, gq_ref, wq_ref,
                   gkv_ref, wkv_ref, wp_ref, ps_ref, hist_ref,
                   aq_ref, akt_ref, avt_ref, aktb_ref, avtb_ref, ub_ref, bo_ref, lat_ref, kr_ref,
                   cq_ref, ck_ref, cv_ref, carry_ref, *, tm, pos0):
    t = pl.program_id(1)
    x = x_ref[0]
    sh1 = mod_ref[0, 0:1, :]
    sc1 = mod_ref[0, 1:2, :]
    h = ((_rms(x) * gmix_ref[...]) * (1.0 + sc1) + sh1).astype(BF16)
    u = jnp.dot(h, w1_ref[...], preferred_element_type=F32)
    kvt = lax.dot_general(wkvt_ref[...], h, (((1,), (1,)), ((), ())), preferred_element_type=F32)
    akt = kvt[0:A_WIDTH]
    avt = kvt[A_WIDTH:]
    akt_ref[0] = akt
    avt_ref[0] = avt
    aktb_ref[0] = akt.astype(BF16)
    avtb_ref[0] = avt.astype(BF16)
    aq_ref[0] = (u[:, U_AQ:U_B] * (A_QK ** -0.5 * LOG2E)).astype(BF16)

    ub = u[:, U_B:U_CQ]
    ub_ref[0] = ub

    @pl.when(t == 0)
    def _():
        carry_ref[...] = hist_ref[0]

    ext = jnp.concatenate([carry_ref[...], ub], axis=0)
    carry_ref[...] = ub[tm - 16:, :]
    s2 = ext + pltpu.roll(ext, 1, 0)
    s4 = s2 + pltpu.roll(s2, 2, 0)
    s8 = s4 + pltpu.roll(s4, 4, 0)
    s16 = s8 + pltpu.roll(s8, 8, 0)
    lane = lax.broadcasted_iota(jnp.int32, (tm, B_WIDTH), 1)
    grp = lane // B_CH
    tot = jnp.where(grp == 0, s2[16:], jnp.where(grp == 1, s4[16:], jnp.where(grp == 2, s8[16:], s16[16:])))
    win = jnp.where(grp == 0, 2, jnp.where(grp == 1, 4, jnp.where(grp == 2, 8, 16)))
    pos = pos0 + t * tm + lax.broadcasted_iota(jnp.int32, (tm, B_WIDTH), 0)
    cnt = jnp.minimum(pos + 1, win).astype(F32)
    m = tot / cnt - ub
    y = jnp.dot(m.astype(BF16), wp_ref[...], preferred_element_type=F32) * ps_ref[...]
    bo_ref[0] = y.astype(BF16)

    cs = cs_ref[...]
    sn = sn_ref[...]
    csq = csq_ref[...]
    qn = (_rms(u[:, U_CQ:U_CKV]) * gq_ref[...]).astype(BF16)
    q2 = jnp.dot(qn, wq_ref[...], preferred_element_type=F32)
    qscale = (C_NOPE + C_ROPE) ** -0.5 * LOG2E
    lat = _rms(u[:, U_CKV:U_KR]) * gkv_ref[...]
    lat_ref[0] = lat
    krp = u[:, U_KR:U_KR + LANES] * cs + u[:, U_KR + LANES:N1] * sn
    kr_ref[0] = krp[:, 0:C_ROPE]
    kv = jnp.dot(lat.astype(BF16), wkv_ref[...], preferred_element_type=F32)
    for hh in range(C_HEADS):
        sl = slice(hh * LANES, (hh + 1) * LANES)
        sr = slice((C_HEADS + hh) * LANES, (C_HEADS + hh + 1) * LANES)
        cq_ref[0, :, sl] = ((q2[:, sl] * csq + q2[:, sr] * sn) * qscale).astype(BF16)
        ck_ref[0, :, sl] = (kv[:, sl] + krp).astype(BF16)
    cv_ref[0] = kv[:, C_HEADS * LANES:].astype(BF16)


def _premix(x, mod, gmix, w1, wkvt, cs, sn, csq, gq, wq, gkv, wkv, wp, ps, hist16, pos0):
    b, t, d = x.shape
    tm = min(1024, t)
    assert t % tm == 0 and tm >= 16
    full = lambda shape: pl.BlockSpec(shape, lambda i, j: (0,) * len(shape))
    tile = lambda n: pl.BlockSpec((1, tm, n), lambda i, j: (i, j, 0))
    tile_t = pl.BlockSpec((1, A_WIDTH, tm), lambda i, j: (i, 0, j))
    tab = pl.BlockSpec((tm, LANES), lambda i, j: (j, 0))
    outs = [(A_WIDTH, BF16), (None, F32), (None, F32), (None, BF16), (None, BF16),
            (B_WIDTH, F32), (B_WIDTH, BF16), (C_KV_RANK, F32), (C_ROPE, F32),
            (C_HEADS * LANES, BF16), (C_HEADS * LANES, BF16), (C_WIDTH, BF16)]
    return pl.pallas_call(
        functools.partial(_premix_kernel, tm=tm, pos0=pos0),
        grid=(b, t // tm),
        in_specs=[tile(d),
                  pl.BlockSpec((1, 6, d), lambda i, j: (i, 0, 0)),
                  full((1, d)), full(w1.shape), full(wkvt.shape), tab, tab, tab,
                  full((1, C_Q_RANK)), full(wq.shape), full((1, C_KV_RANK)), full(wkv.shape),
                  full(wp.shape), full((1, B_WIDTH)),
                  pl.BlockSpec((1, 16, B_WIDTH), lambda i, j: (i, 0, 0))],
        out_specs=[tile_t if n is None else tile(n) for n, _ in outs],
        out_shape=[jax.ShapeDtypeStruct((b, A_WIDTH, t) if n is None else (b, t, n), dt) for n, dt in outs],
        scratch_shapes=[pltpu.VMEM((16, B_WIDTH), F32)],
        compiler_params=_cparams(("arbitrary", "arbitrary")),
        name="premix",
    )(x, mod, gmix, w1, wkvt, cs, sn, csq, gq, wq, gkv, wkv, wp, ps, hist16)


def _kvpast_kernel(lat_ref, kr_ref, wkv_ref, e_ref, ck_ref, cv_ref):
    kv = jnp.dot(lat_ref[0].astype(BF16), wkv_ref[...], preferred_element_type=F32)
    krp = jnp.dot(kr_ref[0].astype(BF16), e_ref[...], preferred_element_type=F32)
    for hh in range(C_HEADS):
        sl = slice(hh * LANES, (hh + 1) * LANES)
        ck_ref[0, :, sl] = (kv[:, sl] + krp).astype(BF16)
    cv_ref[0] = kv[:, C_HEADS * LANES:].astype(BF16)


def _kvpast(lat, kr, wkv):
    b, p, _ = lat.shape
    tm = min(512, p)
    assert p % tm == 0
    e = jnp.eye(C_ROPE, LANES, dtype=BF16)
    return pl.pallas_call(
        _kvpast_kernel,
        grid=(b, p // tm),
        in_specs=[pl.BlockSpec((1, tm, C_KV_RANK), lambda i, j: (i, j, 0)),
                  pl.BlockSpec((1, tm, C_ROPE), lambda i, j: (i, j, 0)),
                  pl.BlockSpec(wkv.shape, lambda i, j: (0, 0)),
                  pl.BlockSpec(e.shape, lambda i, j: (0, 0))],
        out_specs=[pl.BlockSpec((1, tm, C_HEADS * LANES), lambda i, j: (i, j, 0)),
                   pl.BlockSpec((1, tm, C_WIDTH), lambda i, j: (i, j, 0))],
        out_shape=[jax.ShapeDtypeStruct((b, p, C_HEADS * LANES), BF16),
                   jax.ShapeDtypeStruct((b, p, C_WIDTH), BF16)],
        compiler_params=_cparams(("arbitrary", "arbitrary")),
        name="mla_kv_past",
    )(lat, kr, wkv, e)


def _online_update(carry, s, v, pv):
    m, l, acc = carry
    m_new = jnp.maximum(m, jnp.max(s, axis=-1, keepdims=True))
    alpha = jnp.exp2(m - m_new)
    p = jnp.exp2(s - m_new)
    l = alpha * l + jnp.sum(p, axis=-1, keepdims=True)
    acc = alpha * acc + pv(p.astype(BF16), v)
    return m_new, l, acc


def _dot_nt(a, b):
    return lax.dot_general(a, b, (((1,), (1,)), ((), ())), preferred_element_type=F32)


def _attn_kernel(*refs, diff, sub_bias, diag_bias, bq, bk, n_all, lam_init):
    it = iter(refs)
    lamp_ref = next(it) if diff else None
    g_ref = next(it) if diff else None
    q_ref, ka_ref, va_ref, kd_ref, vd_ref = (next(it) for _ in range(5))
    bs_ref = next(it) if sub_bias else None
    bd_ref = next(it) if diag_bias else None
    bmax_ref = next(it) if diff else None
    o_ref = next(it)
    vaug_ref, kmax_ref, acc_ref = (next(it) for _ in range(3))
    causal = n_all is None
    merged = causal and sub_bias

    qi = pl.program_id(2)
    kt = diff
    key_axis = 2 if kt else 1
    t_all = ka_ref.shape[key_axis]
    bkd = kd_ref.shape[key_axis]

    def k_tile(off, width):
        return ka_ref[0, :, pl.ds(off, width)] if kt else ka_ref[0, pl.ds(off, width), :]

    def v_tile(off, width):
        return va_ref[0, :, pl.ds(off, width)] if kt else va_ref[0, pl.ds(off, width), :]

    def vaug_tile(off, width):
        return vaug_ref[:, pl.ds(off, width)] if kt else vaug_ref[pl.ds(off, width), :]

    def qk(qg, k, kcol):
        if kt:
            return jnp.dot(qg, k.astype(BF16), preferred_element_type=F32)
        return _dot_nt(qg, k[:, kcol * LANES:(kcol + 1) * LANES].astype(BF16))

    def pv(p, v):
        return _dot_nt(p, v.astype(BF16)) if kt else jnp.dot(p, v.astype(BF16), preferred_element_type=F32)
    na = qi * (bq // bk) if causal else n_all
    lane = lax.broadcasted_iota(jnp.int32, (bq, LANES), 1)
    q = q_ref[0]
    zero = jnp.zeros_like(q)
    ones_sq = jnp.ones((LANES, LANES), BF16)

    def vmask(v, n):
        ln = lax.broadcasted_iota(jnp.int32, (n, LANES), 1)
        return (ln >= A_QK * v) & (ln < A_QK * (v + 1))

    if diff:
        qs = jnp.concatenate([jnp.where(vmask(v, bq), q, zero) for v in range(4)], axis=0)
        groups = [(qs, 0, [0, 0, 1, 1])]
    else:
        groups = [(q[:, 0:LANES], 0, [0]), (q[:, LANES:2 * LANES], 1, [0])]
    n_var = sum(len(bi) for _, _, bi in groups)

    def key_norm2(k, var):
        kk = k.astype(BF16).astype(F32)
        kk = kk * kk
        if kt:
            row = lax.broadcasted_iota(jnp.int32, kk.shape, 0)
            kk = jnp.where((row >= A_QK * var) & (row < A_QK * (var + 1)), kk, 0.0)
            n2 = jnp.sum(kk, axis=0, keepdims=True)
            return jnp.broadcast_to(jnp.max(n2, axis=1, keepdims=True), (1, LANES))
        n2 = jnp.dot(kk.astype(BF16), ones_sq, preferred_element_type=F32)
        return jnp.max(n2, axis=0, keepdims=True)

    def var_keys(k, var):
        return k if diff else k[:, var * LANES:(var + 1) * LANES]

    @pl.when(qi == 0)
    def _():
        if kt:
            vaug_ref[0:LANES, :] = va_ref[0].astype(BF16)
            vaug_ref[LANES:, :] = jnp.ones((LANES, t_all), BF16)
        else:
            vaug_ref[:, 0:LANES] = va_ref[0].astype(BF16)
            vaug_ref[:, LANES:] = jnp.ones((t_all, LANES), BF16)
        rows = min(t_all, 1024)
        for var in range(n_var):
            km = jnp.zeros((1, LANES), F32)
            for c in range(t_all // rows):
                km = jnp.maximum(km, key_norm2(var_keys(k_tile(c * rows, rows), var), var))
            kmax_ref[var * 8:(var + 1) * 8, :] = jnp.broadcast_to(km, (8, LANES))

    def stack_bias(ref, bias_idx):
        tiles = [ref[0] if ref.shape[0] == 1 else ref[i] for i in bias_idx]
        return tiles[0] if len(tiles) == 1 else jnp.concatenate(tiles, axis=0)

    kd = kd_ref[0]
    vd = vd_ref[0]
    has_sub = sub_bias
    if has_sub:
        n_far = jnp.maximum(na - 1, 0) if causal else max(na - 1, 0)
        off_s = pl.multiple_of(n_far * bk, bk) if causal else n_far * bk
    else:
        n_far = na

    def diag_biases():
        if merged:
            return [stack_bias(bs_ref, bi)[:, bk:] for _, _, bi in groups]
        return [stack_bias(bd_ref, bi) if diag_bias else None for _, _, bi in groups]

    def sub_biases():
        gate = jnp.where(na >= 1, 0.0, NEG_BIG)
        if merged:
            return [stack_bias(bs_ref, bi)[:, :bk] + gate for _, _, bi in groups]
        return [stack_bias(bs_ref, bi) + gate for _, _, bi in groups]

    refs_g = []
    var = 0
    for qg, kcol, bi in groups:
        qf = qg.astype(F32)
        qn2 = jnp.sum(qf * qf, axis=-1, keepdims=True)
        parts = []
        for i, b_i in enumerate(bi):
            km2 = kmax_ref[var * 8:var * 8 + 1, :]
            if n_all is not None:
                km2 = jnp.maximum(km2, key_norm2(var_keys(kd, var), var))
            r = jnp.sqrt(qn2[i * bq:(i + 1) * bq] * km2) * (1.0 + 2.0 ** -6)
            if diff:
                r = r + bmax_ref[b_i, 0:1, :]
            parts.append(r)
            var += 1
        refs_g.append(parts[0] if len(parts) == 1 else jnp.concatenate(parts, axis=0))

    def probs(k, biases):
        out = []
        for (qg, kcol, _), mref, bias in zip(groups, refs_g, biases):
            s = qk(qg, k, kcol)
            if bias is not None:
                s = s + bias
            n = s.shape[1]
            if n >= LANES:
                cols = [jnp.exp2(s[:, c:c + LANES] - mref) for c in range(0, n, LANES)]
            else:
                cols = [jnp.exp2(s - mref[:, :n])]
            out.append((cols[0] if len(cols) == 1 else jnp.concatenate(cols, axis=1)).astype(BF16))
        return out[0] if len(out) == 1 else jnp.concatenate(out, axis=0)

    none_bias = [None] * len(groups)
    if kt:
        vd_aug = jnp.concatenate([vd, jnp.ones((LANES, bkd), BF16)], axis=0)
    else:
        vd_aug = jnp.concatenate([vd, jnp.ones((bkd, LANES), BF16)], axis=1)
    if merged:
        k_near = jnp.concatenate([k_tile(off_s, bk), kd], axis=key_axis - 1)
        v_near = jnp.concatenate([vaug_tile(off_s, bk), vd_aug], axis=key_axis - 1)
        bias_near = [jnp.concatenate([b_s, b_d], axis=1) for b_s, b_d in zip(sub_biases(), diag_biases())]
        acc_ref[...] = pv(probs(k_near, bias_near), v_near)
    else:
        acc_ref[...] = pv(probs(kd, diag_biases()), vd_aug)
        if has_sub:
            acc_ref[...] += pv(probs(k_tile(off_s, bk), sub_biases()), vaug_tile(off_s, bk))
    if causal:
        def far_step(off, width):
            off = pl.multiple_of(off, bk)
            acc_ref[...] += pv(probs(k_tile(off, width), none_bias), vaug_tile(off, width))

        def far_loop(j, c):
            far_step(j * (4 * bk), 4 * bk)
            return c

        n_wide = n_far // 4
        lax.fori_loop(0, n_wide, far_loop, 0)
        rem = n_far - 4 * n_wide
        base = n_wide * (4 * bk)
        pl.when((rem & 2) != 0)(lambda: far_step(base, 2 * bk))
        pl.when((rem & 1) != 0)(lambda: far_step(base + (rem & 2) * bk, bk))
    else:
        for off in range(0, n_far * bk, 4 * bk):
            width = min(4 * bk, n_far * bk - off)
            acc_ref[...] += pv(probs(k_tile(off, width), none_bias), vaug_tile(off, width))

    def finish(results):
        if diff:
            lp = lamp_ref[...]
            lam = (jnp.exp(jnp.sum(lp[0:1] * lp[1:2], axis=-1, keepdims=True))
                   - jnp.exp(jnp.sum(lp[2:3] * lp[3:4], axis=-1, keepdims=True)) + lam_init)
            outs = []
            for e in range(2):
                d = results[2 * e] - lam * results[2 * e + 1]
                own = (lane >= A_V * e) & (lane < A_V * (e + 1))
                ms = jnp.sum(jnp.where(own, d * d, 0.0), axis=-1, keepdims=True) * (1.0 / A_V)
                outs.append(d * lax.rsqrt(ms + EPS) * g_ref[...] * (1.0 - lam_init))
            out = jnp.where(lane < A_V, outs[0], outs[1])
        else:
            out = jnp.where(lane < C_V, results[0], results[1])
        o_ref[0] = out.astype(o_ref.dtype)

    acc = acc_ref[...]
    den = acc[:, LANES:]
    safe = (jnp.min(den) >= 2.0 ** -80) & (jnp.max(den) <= 2.0 ** 100)
    o_fast = acc[:, 0:LANES] / den
    finish([o_fast[i * bq:(i + 1) * bq] for i in range(n_var)])

    @pl.when(jnp.logical_not(safe))
    def _():
        def step(carries, k, v, biases):
            out = []
            for (qg, kcol, _), c, bias in zip(groups, carries, biases):
                s = qk(qg, k, kcol)
                out.append(_online_update(c, s if bias is None else s + bias, v, pv))
            return tuple(out)

        carries = tuple((jnp.full((len(bi) * bq, 1), NEG_BIG, F32), jnp.zeros((len(bi) * bq, 1), F32),
                         jnp.zeros((len(bi) * bq, LANES), F32)) for _, _, bi in groups)
        carries = step(carries, kd, vd, diag_biases())
        if has_sub:
            carries = step(carries, k_tile(off_s, bk), v_tile(off_s, bk), sub_biases())

        def body(j, cs):
            off = pl.multiple_of(j * bk, bk)
            return step(cs, k_tile(off, bk), v_tile(off, bk), none_bias)

        carries = lax.fori_loop(0, n_far, body, carries)
        results = []
        for (_, _, bi), (m, l, a) in zip(groups, carries):
            o = a / l
            results.extend(o[i * bq:(i + 1) * bq] for i in range(len(bi)))
        finish(results)


def _attention(q, k_all, v_all, k_new, v_new, bias_sub, bias_diag, *, diff, causal_blocks,
               lamp=None, g2=None, bias_max=None, lam_init=0.0, all_batch0=0):
    b, tq, _ = q.shape
    t_all = k_all.shape[2 if diff else 1]
    bk = ATT_BLOCK
    bq = min(CAUSAL_Q_BLOCK if causal_blocks else ATT_BLOCK, tq)
    assert tq % bq == 0 and t_all % bk == 0
    if causal_blocks:
        assert bq % bk == 0 and t_all == tq
    else:
        assert tq == bq
    qw = LANES if diff else 2 * LANES
    n_pairs = A_HEADS // 2
    in_specs, args = [], []
    if diff:
        in_specs += [pl.BlockSpec(lamp.shape, lambda i, p, j: (0, 0)),
                     pl.BlockSpec(g2.shape, lambda i, p, j: (0, 0))]
        args += [lamp, g2]
    in_specs.append(pl.BlockSpec((1, bq, qw), lambda i, p, j: (i, j, p)))
    if diff:
        in_specs += [pl.BlockSpec((1, LANES, t_all), lambda i, p, j: (i + all_batch0, p, 0)),
                     pl.BlockSpec((1, LANES, t_all), lambda i, p, j: (i + all_batch0, p, 0)),
                     pl.BlockSpec((1, LANES, bq), lambda i, p, j: (i, p, j)),
                     pl.BlockSpec((1, LANES, bq), lambda i, p, j: (i, p, j))]
    else:
        in_specs += [pl.BlockSpec((1, t_all, qw), lambda i, p, j: (i, 0, p)),
                     pl.BlockSpec((1, t_all, LANES), lambda i, p, j: (i, 0, p)),
                     pl.BlockSpec((1, bq, qw), lambda i, p, j: (i, j, p)),
                     pl.BlockSpec((1, bq, LANES), lambda i, p, j: (i, j, p))]
    args += [q, k_all, v_all, k_new, v_new]
    for bias in (bias_sub, bias_diag):
        if bias is not None:
            if bias.shape[0] == 1:
                in_specs.append(pl.BlockSpec(bias.shape, lambda i, p, j: (0, 0, 0)))
            else:
                in_specs.append(pl.BlockSpec((2,) + bias.shape[1:], lambda i, p, j: (p, 0, 0)))
            args.append(bias)
    if diff:
        in_specs.append(pl.BlockSpec((2, 8, LANES), lambda i, p, j: (p, 0, 0)))
        args.append(bias_max)
    n_var = 4 if diff else 2
    kern = functools.partial(
        _attn_kernel, diff=diff, sub_bias=bias_sub is not None, diag_bias=bias_diag is not None,
        bq=bq, bk=bk, n_all=None if causal_blocks else t_all // bk, lam_init=lam_init)
    return pl.pallas_call(
        kern,
        grid=(b, n_pairs, tq // bq),
        in_specs=in_specs,
        out_specs=pl.BlockSpec((1, bq, LANES), lambda i, p, j: (i, j, p)),
        out_shape=jax.ShapeDtypeStruct((b, tq, n_pairs * LANES), BF16),
        scratch_shapes=[pltpu.VMEM((2 * LANES, t_all) if diff else (t_all, 2 * LANES), BF16),
                        pltpu.VMEM((n_var * 8, LANES), F32),
                        pltpu.VMEM((n_var * bq, 2 * LANES), F32)],
        compiler_params=_cparams(("arbitrary", "arbitrary", "arbitrary")),
        name="diff_attn" if diff else "mla_attn",
    )(*args)


def _postffn_kernel(a_ref, b_ref, c_ref, x_ref, mod_ref, wo_ref, gffn_ref, wg_ref, wv_ref, cw_ref,
                    cb_ref, wd_ref, hist_ref, gfin_ref, y_ref, conv_ref, carry_ref, act_ref, *, tm, final):
    t = pl.program_id(1)
    ffn = wg_ref.shape[1]
    cat = jnp.concatenate([a_ref[0], b_ref[0], c_ref[0]], axis=-1)
    mix = jnp.dot(cat, wo_ref[...], preferred_element_type=F32)
    gt1 = mod_ref[0, 2:3, :]
    sh2 = mod_ref[0, 3:4, :]
    sc2 = mod_ref[0, 4:5, :]
    gt2 = mod_ref[0, 5:6, :]
    x1 = x_ref[0] + gt1 * mix
    h2 = ((_rms(x1) * gffn_ref[...]) * (1.0 + sc2) + sh2).astype(BF16)

    @pl.when(t == 0)
    def _():
        carry_ref[...] = hist_ref[0]

    for c0 in range(0, ffn, FFN_CHUNK):
        sl = slice(c0, min(c0 + FFN_CHUNK, ffn))
        g = jnp.dot(h2, wg_ref[:, sl], preferred_element_type=F32)
        val = jnp.dot(h2, wv_ref[:, sl], preferred_element_type=F32)
        ext = jnp.concatenate([carry_ref[:, sl], g], axis=0)
        tail = g[tm - 8:, :]
        carry_ref[:, sl] = tail
        conv_ref[0, :, sl] = tail
        gc = (cw_ref[0:1, sl] * pltpu.roll(ext, 2, 0)[8:] + cw_ref[1:2, sl] * pltpu.roll(ext, 1, 0)[8:]
              + cw_ref[2:3, sl] * g + cb_ref[:, sl])
        act_ref[:, sl] = (gc * jax.nn.sigmoid(gc) * val).astype(BF16)
    f = jnp.dot(act_ref[...], wd_ref[...], preferred_element_type=F32)
    x2 = x1 + gt2 * f
    if final:
        x2 = _rms(x2) * gfin_ref[...]
    y_ref[0] = x2


def _postffn(a, bo, c, x, mod, wo, gffn, wg, wv, cw, cb, wd, hist8, gfin, final):
    b, t, d = x.shape
    f = wg.shape[1]
    tm = min(512, t)
    assert t % tm == 0 and tm >= 8
    tile = lambda n: pl.BlockSpec((1, tm, n), lambda i, j: (i, j, 0))

    def resident(shape):
        return pl.BlockSpec(shape, lambda i, j: (0,) * len(shape), pipeline_mode=pl.Buffered(1))

    return pl.pallas_call(
        functools.partial(_postffn_kernel, tm=tm, final=final),
        grid=(b, t // tm),
        in_specs=[tile(A_WIDTH), tile(B_WIDTH), tile(C_WIDTH), tile(d),
                  pl.BlockSpec((1, 6, d), lambda i, j: (i, 0, 0)),
                  resident(wo.shape), resident((1, d)), resident(wg.shape), resident(wv.shape),
                  resident((CONV_W, f)), resident((1, f)), resident(wd.shape),
                  pl.BlockSpec((1, 8, f), lambda i, j: (i, 0, 0)),
                  resident((1, d))],
        out_specs=[tile(d), pl.BlockSpec((1, 8, f), lambda i, j: (i, 0, 0))],
        out_shape=[jax.ShapeDtypeStruct((b, t, d), F32), jax.ShapeDtypeStruct((b, 8, f), F32)],
        scratch_shapes=[pltpu.VMEM((8, f), F32), pltpu.VMEM((tm, f), BF16)],
        compiler_params=_cparams(("arbitrary", "arbitrary")),
        name="postmix_ffn",
    )(a, bo, c, x, mod, wo, gffn, wg, wv, cw, cb, wd, hist8, gfin)


def _rot_cols(w):
    half = C_ROPE // 2
    return jnp.concatenate([-w[..., half:], w[..., :half]], axis=-1)


def _layer_weights(l, w_in, w_q_up, w_kv_up, pool_w, w_out, w_up):
    d = w_in.shape[1]
    wkr = w_in[l][:, OFF_CKR:]
    zpad = jnp.zeros((d, LANES - C_ROPE), F32)
    w1 = jnp.concatenate([w_in[l][:, :A_WIDTH], w_in[l][:, OFF_B:OFF_CKR], wkr, zpad, _rot_cols(wkr), zpad],
                         axis=1).astype(BF16)
    wkvt = w_in[l][:, A_WIDTH:OFF_B].T.astype(BF16)

    wq = w_q_up[l].reshape(C_Q_RANK, C_HEADS, C_NOPE + C_ROPE)
    nope, rope = wq[..., :C_NOPE], wq[..., C_NOPE:]
    zq = jnp.zeros((C_Q_RANK, C_HEADS, LANES - C_NOPE - C_ROPE), F32)
    main = jnp.concatenate([rope, nope, zq], axis=-1).reshape(C_Q_RANK, C_HEADS * LANES)
    rot = jnp.concatenate([_rot_cols(rope), jnp.zeros((C_Q_RANK, C_HEADS, LANES - C_ROPE), F32)],
                          axis=-1).reshape(C_Q_RANK, C_HEADS * LANES)
    wq2 = jnp.concatenate([main, rot], axis=1).astype(BF16)

    wkv = w_kv_up[l].reshape(C_KV_RANK, C_HEADS, C_NOPE + C_V)
    zk = jnp.zeros((C_KV_RANK, C_HEADS, C_ROPE), F32)
    kpart = jnp.concatenate([zk, wkv[..., :C_NOPE], zk], axis=-1).reshape(C_KV_RANK, C_HEADS * LANES)
    vpart = wkv[..., C_NOPE:].reshape(C_KV_RANK, C_WIDTH)
    wkv2 = jnp.concatenate([kpart, vpart], axis=1).astype(BF16)

    wp = jnp.zeros((B_WIDTH, B_WIDTH), F32)
    for g in range(B_GROUPS):
        wp = wp.at[g * B_CH:(g + 1) * B_CH, g * B_CH:(g + 1) * B_CH].set(pool_w[l, g])
    ffn = w_up.shape[2] // 2
    return dict(w1=w1, wkvt=wkvt, wq=wq2, wkv=wkv2, wp=wp.astype(BF16), wo=w_out[l].astype(BF16),
                wg=w_up[l][:, :ffn].astype(BF16), wv=w_up[l][:, ffn:].astype(BF16))


def _rope_tables(pos):
    half = C_ROPE // 2
    inv = 1.0 / (ROPE_BASE ** (jnp.arange(half, dtype=jnp.float32) / half))
    ang = pos.astype(jnp.float32)[:, None] * inv[None, :]
    cos, sin = jnp.cos(ang), jnp.sin(ang)
    n = pos.shape[0]
    z = jnp.zeros((n, LANES - C_ROPE), F32)
    cs = jnp.concatenate([cos, cos, z], axis=1)
    sn = jnp.concatenate([sin, sin, z], axis=1)
    csq = jnp.concatenate([cos, cos, jnp.ones((n, C_NOPE), F32), jnp.zeros((n, LANES - C_ROPE - C_NOPE), F32)], axis=1)
    return cs, sn, csq


def kernel(x_prompt, x_sample, c_prompt, c_sample, cache_a_k, cache_a_v, cache_c_latent, cache_c_krope,
           state_b_pool, state_ffn_conv, w_ada, b_ada, g_mix, w_in, lam_q1, lam_k1, lam_q2, lam_k2,
           a_subln_g, rel_bias, pool_w, pool_scale, c_q_norm_g, w_q_up, c_kv_norm_g, w_kv_up, w_out,
           g_ffn, w_up, conv_w, conv_b, w_down, g_final):
    depth = w_in.shape[0]
    bp, tp, d = x_prompt.shape
    bs, ts, _ = x_sample.shape
    past = cache_a_k.shape[2]
    ffn = conv_w.shape[2]
    blk = ATT_BLOCK

    mod_all = _ada(jnp.concatenate([c_prompt, c_sample], axis=0), w_ada, b_ada)
    mod_all = mod_all.reshape(depth, bp + bs, 6, d)

    r = np.arange(blk)
    rq = np.arange(min(CAUSAL_Q_BLOCK, tp))
    bkt, madd = _tile_bucket_mask(blk + rq, np.arange(blk + len(rq)))
    pb_near, bias_max = _bias_tiles(rel_bias, bkt, madd)
    mla_diag = _tile_bucket_mask(rq, rq)[1][None]
    rs = np.arange(ts)
    bkt, madd = _tile_bucket_mask(past + rs, past - blk + r)
    sb_sub, _ = _bias_tiles(rel_bias, bkt, madd)
    bkt, madd = _tile_bucket_mask(past + rs, past + rs)
    sb_diag, _ = _bias_tiles(rel_bias, bkt, madd)
    mla_sdiag = madd[None]

    tabs_p = _rope_tables(jnp.arange(tp, dtype=jnp.int32))
    tabs_s = _rope_tables(past + jnp.arange(ts, dtype=jnp.int32))
    cache_kt = jnp.transpose(cache_a_k, (0, 1, 3, 4, 2)).reshape(depth * bs, A_WIDTH, past)
    cache_vt = jnp.transpose(cache_a_v, (0, 1, 3, 4, 2)).reshape(depth * bs, A_WIDTH, past)
    zero_pool = jnp.zeros((bp, 16, B_WIDTH), F32)
    zero_conv = jnp.zeros((bp, 8, ffn), F32)

    hp, hs = x_prompt, x_sample
    new_p, new_s = [], []
    for l in range(depth):
        w = _layer_weights(l, w_in, w_q_up, w_kv_up, pool_w, w_out, w_up)
        lam_init = 0.8 - 0.6 * math.exp(-0.3 * l)
        lamp = jnp.stack([lam_q1[l], lam_k1[l], lam_q2[l], lam_k2[l]], axis=0)
        g2 = jnp.concatenate([a_subln_g[l], a_subln_g[l]])[None]
        wd = w_down[l].astype(BF16)
        last = l == depth - 1

        def run(x, mod, tabs, pos0, hist16, hist8, pasts):
            (aq, akt, avt, aktb, avtb, ub, bo, lat, kr, cq, ck, cv) = _premix(
                x, mod, g_mix[l][None], w["w1"], w["wkvt"], *tabs, c_q_norm_g[l][None], w["wq"],
                c_kv_norm_g[l][None], w["wkv"], w["wp"], pool_scale[l][None], hist16, pos0)
            if pasts is None:
                a_out = _attention(aq, aktb, avtb, aktb, avtb, pb_near, None, diff=True, causal_blocks=True,
                                   lamp=lamp, g2=g2, bias_max=bias_max, lam_init=lam_init)
                c_out = _attention(cq, ck, cv, ck, cv, None, mla_diag, diff=False, causal_blocks=True)
            else:
                plat, pkr = pasts
                a_out = _attention(aq, cache_kt, cache_vt, aktb, avtb, sb_sub, sb_diag, diff=True,
                                   causal_blocks=False, lamp=lamp, g2=g2, bias_max=bias_max, lam_init=lam_init,
                                   all_batch0=l * bs)
                ckp, cvp = _kvpast(plat, pkr, w["wkv"])
                c_out = _attention(cq, ckp, cvp, ck, cv, None, mla_sdiag, diff=False, causal_blocks=False)
            y, conv8 = _postffn(a_out, bo, c_out, x, mod, w["wo"], g_ffn[l][None], w["wg"], w["wv"],
                                conv_w[l], conv_b[l][None], wd, hist8, g_final[None], last)
            bsz, tt = x.shape[0], x.shape[1]
            to_cache = lambda a: jnp.transpose(a.reshape(bsz, A_HEADS, A_V, tt), (0, 3, 1, 2))
            state = (to_cache(akt), to_cache(avt), lat, kr,
                     ub[:, tt - POOL_HIST:], conv8[:, 8 - (CONV_W - 1):])
            return y, state

        hp, st = run(hp, mod_all[l, :bp], tabs_p, 0, zero_pool, zero_conv, None)
        new_p.append(st)
        hist16 = jnp.pad(state_b_pool[l], ((0, 0), (1, 0), (0, 0)))
        hist8 = jnp.pad(state_ffn_conv[l], ((0, 0), (8 - (CONV_W - 1), 0), (0, 0)))
        hs, st = run(hs, mod_all[l, bp:], tabs_s, past, hist16, hist8, (cache_c_latent[l], cache_c_krope[l]))
        new_s.append(st)

    outs_p = [jnp.stack(z, axis=0) for z in zip(*new_p)]
    outs_s = [jnp.stack(z, axis=0) for z in zip(*new_s)]
    return (hp, hs, *outs_p, *outs_s)
```

```python
import functools
import math

import numpy as np
import jax
import jax.numpy as jnp
from jax import lax
from jax.experimental import pallas as pl
from jax.experimental.pallas import tpu as pltpu

F32 = jnp.float32
BF16 = jnp.bfloat16

CHUNK = 64
A_HEADS = 6
A_QK = 32
A_V = 64
B_GROUPS = 4
B_CH = 64
POOL_WINDOWS = (2, 4, 8, 16)
POOL_HIST = 15
C_HEADS = 6
C_NOPE = 64
C_ROPE = 32
C_V = 64
C_Q_RANK = 256
C_KV_RANK = 128
ROPE_BASE = 10000.0
REL_BUCKETS = 32
REL_MAX_DIST = 128
CONV_W = 3
EPS = 1e-6

A_WIDTH = A_HEADS * A_V
B_WIDTH = B_GROUPS * B_CH
C_WIDTH = C_HEADS * C_V
OFF_B = 3 * A_WIDTH
OFF_CQ = OFF_B + B_WIDTH
OFF_CKV = OFF_CQ + C_Q_RANK
OFF_CKR = OFF_CKV + C_KV_RANK

LANES = 128
ATT_BLOCK = 256
CAUSAL_Q_BLOCK = 512
FFN_CHUNK = 1024
LOG2E = math.log2(math.e)
NEG_BIG = -1e30
VMEM_LIMIT = 56 * 1024 * 1024

U_AQ = 0
U_B = U_AQ + A_WIDTH
U_CQ = U_B + B_WIDTH
U_CKV = U_CQ + C_Q_RANK
U_KR = U_CKV + C_KV_RANK
N1 = U_KR + 2 * LANES


def _cparams(sem):
    return pltpu.CompilerParams(dimension_semantics=sem, vmem_limit_bytes=VMEM_LIMIT)


def _rms(x):
    return x * lax.rsqrt(jnp.mean(x * x, axis=-1, keepdims=True) + EPS)


def _ada_kernel(c_ref, w_ref, b_ref, o_ref):
    c = c_ref[...]
    s = (c * jax.nn.sigmoid(c)).astype(BF16)
    o_ref[0] = jnp.dot(s, w_ref[0].astype(BF16), preferred_element_type=F32) + b_ref[0]


def _ada(c_all, w_ada, b_ada):
    depth, d, n = w_ada.shape
    nb = c_all.shape[0]
    tn = 1024
    return pl.pallas_call(
        _ada_kernel,
        grid=(depth, n // tn),
        in_specs=[pl.BlockSpec((nb, d), lambda l, j: (0, 0)),
                  pl.BlockSpec((1, d, tn), lambda l, j: (l, 0, j)),
                  pl.BlockSpec((1, 1, tn), lambda l, j: (l, 0, j))],
        out_specs=pl.BlockSpec((1, nb, tn), lambda l, j: (l, 0, j)),
        out_shape=jax.ShapeDtypeStruct((depth, nb, n), F32),
        compiler_params=_cparams(("arbitrary", "arbitrary")),
        name="ada_mod",
    )(c_all, w_ada, b_ada.reshape(depth, 1, n))


def _bias_kernel(rb_ref, bucket_ref, madd_ref, o_ref, mx_ref):
    h = pl.program_id(0)
    bucket = bucket_ref[...]
    far = rb_ref[REL_BUCKETS // 2 - 1, h]
    val = jnp.zeros(bucket.shape, F32)
    top = far - far
    for b in range(REL_BUCKETS):
        val = jnp.where(bucket == b, rb_ref[b, h] - far, val)
        top = jnp.maximum(top, rb_ref[b, h] - far)
    o_ref[0] = val * LOG2E + madd_ref[...]
    mx_ref[0] = jnp.full(mx_ref.shape[1:], top * LOG2E, F32)


def _bias_tiles(rel_bias, bucket, madd):
    r, c = bucket.shape
    return pl.pallas_call(
        _bias_kernel,
        grid=(A_HEADS,),
        in_specs=[pl.BlockSpec(memory_space=pltpu.SMEM),
                  pl.BlockSpec((r, c), lambda h: (0, 0)),
                  pl.BlockSpec((r, c), lambda h: (0, 0))],
        out_specs=[pl.BlockSpec((1, r, c), lambda h: (h, 0, 0)),
                   pl.BlockSpec((1, 8, LANES), lambda h: (h, 0, 0))],
        out_shape=[jax.ShapeDtypeStruct((A_HEADS, r, c), F32),
                   jax.ShapeDtypeStruct((A_HEADS, 8, LANES), F32)],
        compiler_params=_cparams(("arbitrary",)),
        name="rel_bias_tiles",
    )(rel_bias, bucket, madd)


def _t5_bucket(rel):
    half = REL_BUCKETS // 2
    exact = half // 2
    ret = jnp.where(rel > 0, half, 0)
    n = jnp.abs(rel)
    nf = jnp.maximum(n, 1).astype(jnp.float32)
    large = exact + (jnp.log(nf / exact) / math.log(REL_MAX_DIST / exact) * (half - exact)).astype(jnp.int32)
    large = jnp.minimum(large, half - 1)
    return ret + jnp.where(n < exact, n, large)


def _tile_bucket_mask(q_pos, k_pos):
    q_pos = jnp.asarray(q_pos, jnp.int32)
    k_pos = jnp.asarray(k_pos, jnp.int32)
    bucket = _t5_bucket(k_pos[None, :] - q_pos[:, None]).astype(jnp.int32)
    visible = (k_pos[None, :] // CHUNK) <= (q_pos[:, None] // CHUNK)
    return bucket, jnp.where(visible, 0.0, NEG_BIG).astype(F32)


def _premix_kernel(x_ref, mod_ref, gmix_ref, w1_ref, wkvt_ref, cs_ref, sn_ref, csq_ref, gq_ref, wq_ref,
                   gkv_ref, wkv_ref, wp_ref, ps_ref, hist_ref, *rest, tm, pos0, n_prev):
    prev = rest[:3] if n_prev else None
    (aq_ref, akt_ref, avt_ref, aktb_ref, avtb_ref, ub_ref, bo_ref, lat_ref, kr_ref,
     cq_ref, ck_ref, cv_ref, carry_ref) = rest[3:] if n_prev else rest
    t = pl.program_id(1)
    x = x_ref[0]
    sh1 = mod_ref[0, 0:1, :]
    sc1 = mod_ref[0, 1:2, :]
    h = ((_rms(x) * gmix_ref[...]) * (1.0 + sc1) + sh1).astype(BF16)
    u = jnp.dot(h, w1_ref[...], preferred_element_type=F32)
    kvt = lax.dot_general(wkvt_ref[...], h, (((1,), (1,)), ((), ())), preferred_element_type=F32)
    akt = kvt[0:A_WIDTH]
    avt = kvt[A_WIDTH:]
    if n_prev:
        akt_ref[0:n_prev, 0] = prev[0][:, 0]
        avt_ref[0:n_prev, 0] = prev[1][:, 0]
        lat_ref[0:n_prev, 0] = prev[2][:, 0]
    akt_ref[n_prev, 0] = akt
    avt_ref[n_prev, 0] = avt
    aktb_ref[0] = akt.astype(BF16)
    avtb_ref[0] = avt.astype(BF16)
    aq_ref[0] = (u[:, U_AQ:U_B] * (A_QK ** -0.5 * LOG2E)).astype(BF16)

    ub = u[:, U_B:U_CQ]
    ub_ref[0] = ub

    @pl.when(t == 0)
    def _():
        carry_ref[...] = hist_ref[0]

    ext = jnp.concatenate([carry_ref[...], ub], axis=0)
    carry_ref[...] = ub[tm - 16:, :]
    s2 = ext + pltpu.roll(ext, 1, 0)
    s4 = s2 + pltpu.roll(s2, 2, 0)
    s8 = s4 + pltpu.roll(s4, 4, 0)
    s16 = s8 + pltpu.roll(s8, 8, 0)
    lane = lax.broadcasted_iota(jnp.int32, (tm, B_WIDTH), 1)
    grp = lane // B_CH
    tot = jnp.where(grp == 0, s2[16:], jnp.where(grp == 1, s4[16:], jnp.where(grp == 2, s8[16:], s16[16:])))
    win = jnp.where(grp == 0, 2, jnp.where(grp == 1, 4, jnp.where(grp == 2, 8, 16)))
    pos = pos0 + t * tm + lax.broadcasted_iota(jnp.int32, (tm, B_WIDTH), 0)
    cnt = jnp.minimum(pos + 1, win).astype(F32)
    m = tot / cnt - ub
    y = jnp.dot(m.astype(BF16), wp_ref[...], preferred_element_type=F32) * ps_ref[...]
    bo_ref[0] = y.astype(BF16)

    cs = cs_ref[...]
    sn = sn_ref[...]
    csq = csq_ref[...]
    qn = (_rms(u[:, U_CQ:U_CKV]) * gq_ref[...]).astype(BF16)
    q2 = jnp.dot(qn, wq_ref[...], preferred_element_type=F32)
    qscale = (C_NOPE + C_ROPE) ** -0.5 * LOG2E
    lat = _rms(u[:, U_CKV:U_KR]) * gkv_ref[...]
    lat_ref[n_prev, 0] = lat
    krp = u[:, U_KR:U_KR + LANES] * cs + u[:, U_KR + LANES:N1] * sn
    kr_ref[0] = krp[:, 0:C_ROPE]
    kv = jnp.dot(lat.astype(BF16), wkv_ref[...], preferred_element_type=F32)
    for hh in range(C_HEADS):
        sl = slice(hh * LANES, (hh + 1) * LANES)
        sr = slice((C_HEADS + hh) * LANES, (C_HEADS + hh + 1) * LANES)
        cq_ref[0, :, sl] = ((q2[:, sl] * csq + q2[:, sr] * sn) * qscale).astype(BF16)
        ck_ref[0, :, sl] = (kv[:, sl] + krp).astype(BF16)
    cv_ref[0] = kv[:, C_HEADS * LANES:].astype(BF16)


def _premix(x, mod, gmix, w1, wkvt, cs, sn, csq, gq, wq, gkv, wkv, wp, ps, hist16, pos0, prev=None):
    b, t, d = x.shape
    n_prev = 0 if prev is None else prev[0].shape[0]
    nl = n_prev + 1
    tm = min(512 if n_prev else 1024, t)
    assert t % tm == 0 and tm >= 16
    full = lambda shape: pl.BlockSpec(shape, lambda i, j: (0,) * len(shape))
    tile = lambda n: pl.BlockSpec((1, tm, n), lambda i, j: (i, j, 0))
    tile_t = pl.BlockSpec((1, A_WIDTH, tm), lambda i, j: (i, 0, j))
    layers_t = lambda n: pl.BlockSpec((n, 1, A_WIDTH, tm), lambda i, j: (0, i, 0, j))
    layers = lambda n, w: pl.BlockSpec((n, 1, tm, w), lambda i, j: (0, i, j, 0))
    tab = pl.BlockSpec((tm, LANES), lambda i, j: (j, 0))
    outs = [(A_WIDTH, BF16), (None, F32), (None, F32), (None, BF16), (None, BF16),
            (B_WIDTH, F32), (B_WIDTH, BF16), (C_KV_RANK, F32), (C_ROPE, F32),
            (C_HEADS * LANES, BF16), (C_HEADS * LANES, BF16), (C_WIDTH, BF16)]
    out_specs = [tile_t if n is None else tile(n) for n, _ in outs]
    out_shape = [jax.ShapeDtypeStruct((b, A_WIDTH, t) if n is None else (b, t, n), dt) for n, dt in outs]
    for idx in (1, 2):
        out_specs[idx] = layers_t(nl)
        out_shape[idx] = jax.ShapeDtypeStruct((nl, b, A_WIDTH, t), F32)
    out_specs[7] = layers(nl, C_KV_RANK)
    out_shape[7] = jax.ShapeDtypeStruct((nl, b, t, C_KV_RANK), F32)
    prev_specs = [layers_t(n_prev), layers_t(n_prev), layers(n_prev, C_KV_RANK)] if n_prev else []
    return pl.pallas_call(
        functools.partial(_premix_kernel, tm=tm, pos0=pos0, n_prev=n_prev),
        grid=(b, t // tm),
        in_specs=[tile(d),
                  pl.BlockSpec((1, 6, d), lambda i, j: (i, 0, 0)),
                  full((1, d)), full(w1.shape), full(wkvt.shape), tab, tab, tab,
                  full((1, C_Q_RANK)), full(wq.shape), full((1, C_KV_RANK)), full(wkv.shape),
                  full(wp.shape), full((1, B_WIDTH)),
                  pl.BlockSpec((1, 16, B_WIDTH), lambda i, j: (i, 0, 0))] + prev_specs,
        out_specs=out_specs,
        out_shape=out_shape,
        scratch_shapes=[pltpu.VMEM((16, B_WIDTH), F32)],
        compiler_params=_cparams(("arbitrary", "arbitrary")),
        name="premix",
    )(x, mod, gmix, w1, wkvt, cs, sn, csq, gq, wq, gkv, wkv, wp, ps, hist16, *(prev or ()))


def _kvpast_kernel(lat_ref, kr_ref, wkv_ref, e_ref, ck_ref, cv_ref):
    kv = jnp.dot(lat_ref[0].astype(BF16), wkv_ref[...], preferred_element_type=F32)
    krp = jnp.dot(kr_ref[0].astype(BF16), e_ref[...], preferred_element_type=F32)
    for hh in range(C_HEADS):
        sl = slice(hh * LANES, (hh + 1) * LANES)
        ck_ref[0, :, sl] = (kv[:, sl] + krp).astype(BF16)
    cv_ref[0] = kv[:, C_HEADS * LANES:].astype(BF16)


def _kvpast(lat, kr, wkv):
    b, p, _ = lat.shape
    tm = min(512, p)
    assert p % tm == 0
    e = jnp.eye(C_ROPE, LANES, dtype=BF16)
    return pl.pallas_call(
        _kvpast_kernel,
        grid=(b, p // tm),
        in_specs=[pl.BlockSpec((1, tm, C_KV_RANK), lambda i, j: (i, j, 0)),
                  pl.BlockSpec((1, tm, C_ROPE), lambda i, j: (i, j, 0)),
                  pl.BlockSpec(wkv.shape, lambda i, j: (0, 0)),
                  pl.BlockSpec(e.shape, lambda i, j: (0, 0))],
        out_specs=[pl.BlockSpec((1, tm, C_HEADS * LANES), lambda i, j: (i, j, 0)),
                   pl.BlockSpec((1, tm, C_WIDTH), lambda i, j: (i, j, 0))],
        out_shape=[jax.ShapeDtypeStruct((b, p, C_HEADS * LANES), BF16),
                   jax.ShapeDtypeStruct((b, p, C_WIDTH), BF16)],
        compiler_params=_cparams(("arbitrary", "arbitrary")),
        name="mla_kv_past",
    )(lat, kr, wkv, e)


def _online_update(carry, s, v, pv):
    m, l, acc = carry
    m_new = jnp.maximum(m, jnp.max(s, axis=-1, keepdims=True))
    alpha = jnp.exp2(m - m_new)
    p = jnp.exp2(s - m_new)
    l = alpha * l + jnp.sum(p, axis=-1, keepdims=True)
    acc = alpha * acc + pv(p.astype(BF16), v)
    return m_new, l, acc


def _dot_nt(a, b):
    return lax.dot_general(a, b, (((1,), (1,)), ((), ())), preferred_element_type=F32)


def _attn_kernel(*refs, diff, sub_bias, diag_bias, bq, bk, n_all, lam_init):
    it = iter(refs)
    lamp_ref = next(it) if diff else None
    g_ref = next(it) if diff else None
    q_ref, ka_ref, va_ref, kd_ref, vd_ref = (next(it) for _ in range(5))
    bs_ref = next(it) if sub_bias else None
    bd_ref = next(it) if diag_bias else None
    bmax_ref = next(it) if diff else None
    o_ref = next(it)
    vaug_ref, kmax_ref, acc_ref = (next(it) for _ in range(3))
    causal = n_all is None
    merged = causal and sub_bias

    qi = pl.program_id(2)
    kt = diff
    key_axis = 2 if kt else 1
    t_all = ka_ref.shape[key_axis]
    bkd = kd_ref.shape[key_axis]

    def k_tile(off, width):
        return ka_ref[0, :, pl.ds(off, width)] if kt else ka_ref[0, pl.ds(off, width), :]

    def v_tile(off, width):
        return va_ref[0, :, pl.ds(off, width)] if kt else va_ref[0, pl.ds(off, width), :]

    def vaug_tile(off, width):
        return vaug_ref[:, pl.ds(off, width)] if kt else vaug_ref[pl.ds(off, width), :]

    def qk(qg, k, kcol):
        if kt:
            return jnp.dot(qg, k.astype(BF16), preferred_element_type=F32)
        return _dot_nt(qg, k[:, kcol * LANES:(kcol + 1) * LANES].astype(BF16))

    def pv(p, v):
        return _dot_nt(p, v.astype(BF16)) if kt else jnp.dot(p, v.astype(BF16), preferred_element_type=F32)
    na = qi * (bq // bk) if causal else n_all
    lane = lax.broadcasted_iota(jnp.int32, (bq, LANES), 1)
    q = q_ref[0]
    zero = jnp.zeros_like(q)
    ones_sq = jnp.ones((LANES, LANES), BF16)

    def vmask(v, n):
        ln = lax.broadcasted_iota(jnp.int32, (n, LANES), 1)
        return (ln >= A_QK * v) & (ln < A_QK * (v + 1))

    if diff:
        qs = jnp.concatenate([jnp.where(vmask(v, bq), q, zero) for v in range(4)], axis=0)
        groups = [(qs, 0, [0, 0, 1, 1])]
    else:
        groups = [(q[:, 0:LANES], 0, [0]), (q[:, LANES:2 * LANES], 1, [0])]
    n_var = sum(len(bi) for _, _, bi in groups)

    def key_norm2(k, var):
        kk = k.astype(BF16).astype(F32)
        kk = kk * kk
        if kt:
            row = lax.broadcasted_iota(jnp.int32, kk.shape, 0)
            kk = jnp.where((row >= A_QK * var) & (row < A_QK * (var + 1)), kk, 0.0)
            n2 = jnp.sum(kk, axis=0, keepdims=True)
            return jnp.broadcast_to(jnp.max(n2, axis=1, keepdims=True), (1, LANES))
        n2 = jnp.dot(kk.astype(BF16), ones_sq, preferred_element_type=F32)
        return jnp.max(n2, axis=0, keepdims=True)

    def var_keys(k, var):
        return k if diff else k[:, var * LANES:(var + 1) * LANES]

    @pl.when(qi == 0)
    def _():
        if kt:
            vaug_ref[0:LANES, :] = va_ref[0].astype(BF16)
            vaug_ref[LANES:, :] = jnp.ones((LANES, t_all), BF16)
        else:
            vaug_ref[:, 0:LANES] = va_ref[0].astype(BF16)
            vaug_ref[:, LANES:] = jnp.ones((t_all, LANES), BF16)
        rows = min(t_all, 1024)
        for var in range(n_var):
            km = jnp.zeros((1, LANES), F32)
            for c in range(t_all // rows):
                km = jnp.maximum(km, key_norm2(var_keys(k_tile(c * rows, rows), var), var))
            kmax_ref[var * 8:(var + 1) * 8, :] = jnp.broadcast_to(km, (8, LANES))

    def stack_bias(ref, bias_idx):
        tiles = [ref[0] if ref.shape[0] == 1 else ref[i] for i in bias_idx]
        return tiles[0] if len(tiles) == 1 else jnp.concatenate(tiles, axis=0)

    kd = kd_ref[0]
    vd = vd_ref[0]
    has_sub = sub_bias
    if has_sub:
        n_far = jnp.maximum(na - 1, 0) if causal else max(na - 1, 0)
        off_s = pl.multiple_of(n_far * bk, bk) if causal else n_far * bk
    else:
        n_far = na

    def diag_biases():
        if merged:
            return [stack_bias(bs_ref, bi)[:, bk:] for _, _, bi in groups]
        return [stack_bias(bd_ref, bi) if diag_bias else None for _, _, bi in groups]

    def sub_biases():
        gate = jnp.where(na >= 1, 0.0, NEG_BIG)
        if merged:
            return [stack_bias(bs_ref, bi)[:, :bk] + gate for _, _, bi in groups]
        return [stack_bias(bs_ref, bi) + gate for _, _, bi in groups]

    refs_g = []
    var = 0
    for qg, kcol, bi in groups:
        qf = qg.astype(F32)
        qn2 = jnp.sum(qf * qf, axis=-1, keepdims=True)
        parts = []
        for i, b_i in enumerate(bi):
            km2 = kmax_ref[var * 8:var * 8 + 1, :]
            if n_all is not None:
                km2 = jnp.maximum(km2, key_norm2(var_keys(kd, var), var))
            r = jnp.sqrt(qn2[i * bq:(i + 1) * bq] * km2) * (1.0 + 2.0 ** -6)
            if diff:
                r = r + bmax_ref[b_i, 0:1, :]
            parts.append(r)
            var += 1
        refs_g.append(parts[0] if len(parts) == 1 else jnp.concatenate(parts, axis=0))

    def probs(k, biases):
        out = []
        for (qg, kcol, _), mref, bias in zip(groups, refs_g, biases):
            s = qk(qg, k, kcol)
            if bias is not None:
                s = s + bias
            n = s.shape[1]
            if n >= LANES:
                cols = [jnp.exp2(s[:, c:c + LANES] - mref) for c in range(0, n, LANES)]
            else:
                cols = [jnp.exp2(s - mref[:, :n])]
            out.append((cols[0] if len(cols) == 1 else jnp.concatenate(cols, axis=1)).astype(BF16))
        return out[0] if len(out) == 1 else jnp.concatenate(out, axis=0)

    none_bias = [None] * len(groups)
    if kt:
        vd_aug = jnp.concatenate([vd, jnp.ones((LANES, bkd), BF16)], axis=0)
    else:
        vd_aug = jnp.concatenate([vd, jnp.ones((bkd, LANES), BF16)], axis=1)
    if merged:
        k_near = jnp.concatenate([k_tile(off_s, bk), kd], axis=key_axis - 1)
        v_near = jnp.concatenate([vaug_tile(off_s, bk), vd_aug], axis=key_axis - 1)
        bias_near = [jnp.concatenate([b_s, b_d], axis=1) for b_s, b_d in zip(sub_biases(), diag_biases())]
        acc_ref[...] = pv(probs(k_near, bias_near), v_near)
    else:
        acc_ref[...] = pv(probs(kd, diag_biases()), vd_aug)
        if has_sub:
            acc_ref[...] += pv(probs(k_tile(off_s, bk), sub_biases()), vaug_tile(off_s, bk))
    if causal:
        def far_step(off, width):
            off = pl.multiple_of(off, bk)
            acc_ref[...] += pv(probs(k_tile(off, width), none_bias), vaug_tile(off, width))

        def far_loop(j, c):
            far_step(j * (4 * bk), 4 * bk)
            return c

        n_wide = n_far // 4
        lax.fori_loop(0, n_wide, far_loop, 0)
        rem = n_far - 4 * n_wide
        base = n_wide * (4 * bk)
        pl.when((rem & 2) != 0)(lambda: far_step(base, 2 * bk))
        pl.when((rem & 1) != 0)(lambda: far_step(base + (rem & 2) * bk, bk))
    else:
        for off in range(0, n_far * bk, 4 * bk):
            width = min(4 * bk, n_far * bk - off)
            acc_ref[...] += pv(probs(k_tile(off, width), none_bias), vaug_tile(off, width))

    def finish(results):
        if diff:
            lp = lamp_ref[...]
            lam = (jnp.exp(jnp.sum(lp[0:1] * lp[1:2], axis=-1, keepdims=True))
                   - jnp.exp(jnp.sum(lp[2:3] * lp[3:4], axis=-1, keepdims=True)) + lam_init)
            outs = []
            for e in range(2):
                d = results[2 * e] - lam * results[2 * e + 1]
                own = (lane >= A_V * e) & (lane < A_V * (e + 1))
                ms = jnp.sum(jnp.where(own, d * d, 0.0), axis=-1, keepdims=True) * (1.0 / A_V)
                outs.append(d * lax.rsqrt(ms + EPS) * g_ref[...] * (1.0 - lam_init))
            out = jnp.where(lane < A_V, outs[0], outs[1])
        else:
            out = jnp.where(lane < C_V, results[0], results[1])
        o_ref[0] = out.astype(o_ref.dtype)

    acc = acc_ref[...]
    den = acc[:, LANES:]
    safe = (jnp.min(den) >= 2.0 ** -80) & (jnp.max(den) <= 2.0 ** 100)
    o_fast = acc[:, 0:LANES] / den
    finish([o_fast[i * bq:(i + 1) * bq] for i in range(n_var)])

    @pl.when(jnp.logical_not(safe))
    def _():
        def step(carries, k, v, biases):
            out = []
            for (qg, kcol, _), c, bias in zip(groups, carries, biases):
                s = qk(qg, k, kcol)
                out.append(_online_update(c, s if bias is None else s + bias, v, pv))
            return tuple(out)

        carries = tuple((jnp.full((len(bi) * bq, 1), NEG_BIG, F32), jnp.zeros((len(bi) * bq, 1), F32),
                         jnp.zeros((len(bi) * bq, LANES), F32)) for _, _, bi in groups)
        carries = step(carries, kd, vd, diag_biases())
        if has_sub:
            carries = step(carries, k_tile(off_s, bk), v_tile(off_s, bk), sub_biases())

        def body(j, cs):
            off = pl.multiple_of(j * bk, bk)
            return step(cs, k_tile(off, bk), v_tile(off, bk), none_bias)

        carries = lax.fori_loop(0, n_far, body, carries)
        results = []
        for (_, _, bi), (m, l, a) in zip(groups, carries):
            o = a / l
            results.extend(o[i * bq:(i + 1) * bq] for i in range(len(bi)))
        finish(results)


def _attention(q, k_all, v_all, k_new, v_new, bias_sub, bias_diag, *, diff, causal_blocks,
               lamp=None, g2=None, bias_max=None, lam_init=0.0, all_batch0=0):
    b, tq, _ = q.shape
    t_all = k_all.shape[2 if diff else 1]
    bk = ATT_BLOCK
    bq = min(CAUSAL_Q_BLOCK if causal_blocks else ATT_BLOCK, tq)
    assert tq % bq == 0 and t_all % bk == 0
    if causal_blocks:
        assert bq % bk == 0 and t_all == tq
    else:
        assert tq == bq
    qw = LANES if diff else 2 * LANES
    n_pairs = A_HEADS // 2
    in_specs, args = [], []
    if diff:
        in_specs += [pl.BlockSpec(lamp.shape, lambda i, p, j: (0, 0)),
                     pl.BlockSpec(g2.shape, lambda i, p, j: (0, 0))]
        args += [lamp, g2]
    in_specs.append(pl.BlockSpec((1, bq, qw), lambda i, p, j: (i, j, p)))
    if diff:
        in_specs += [pl.BlockSpec((1, LANES, t_all), lambda i, p, j: (i + all_batch0, p, 0)),
                     pl.BlockSpec((1, LANES, t_all), lambda i, p, j: (i + all_batch0, p, 0)),
                     pl.BlockSpec((1, LANES, bq), lambda i, p, j: (i, p, j)),
                     pl.BlockSpec((1, LANES, bq), lambda i, p, j: (i, p, j))]
    else:
        in_specs += [pl.BlockSpec((1, t_all, qw), lambda i, p, j: (i, 0, p)),
                     pl.BlockSpec((1, t_all, LANES), lambda i, p, j: (i, 0, p)),
                     pl.BlockSpec((1, bq, qw), lambda i, p, j: (i, j, p)),
                     pl.BlockSpec((1, bq, LANES), lambda i, p, j: (i, j, p))]
    args += [q, k_all, v_all, k_new, v_new]
    for bias in (bias_sub, bias_diag):
        if bias is not None:
            if bias.shape[0] == 1:
                in_specs.append(pl.BlockSpec(bias.shape, lambda i, p, j: (0, 0, 0)))
            else:
                in_specs.append(pl.BlockSpec((2,) + bias.shape[1:], lambda i, p, j: (p, 0, 0)))
            args.append(bias)
    if diff:
        in_specs.append(pl.BlockSpec((2, 8, LANES), lambda i, p, j: (p, 0, 0)))
        args.append(bias_max)
    n_var = 4 if diff else 2
    kern = functools.partial(
        _attn_kernel, diff=diff, sub_bias=bias_sub is not None, diag_bias=bias_diag is not None,
        bq=bq, bk=bk, n_all=None if causal_blocks else t_all // bk, lam_init=lam_init)
    return pl.pallas_call(
        kern,
        grid=(b, n_pairs, tq // bq),
        in_specs=in_specs,
        out_specs=pl.BlockSpec((1, bq, LANES), lambda i, p, j: (i, j, p)),
        out_shape=jax.ShapeDtypeStruct((b, tq, n_pairs * LANES), BF16),
        scratch_shapes=[pltpu.VMEM((2 * LANES, t_all) if diff else (t_all, 2 * LANES), BF16),
                        pltpu.VMEM((n_var * 8, LANES), F32),
                        pltpu.VMEM((n_var * bq, 2 * LANES), F32)],
        compiler_params=_cparams(("arbitrary", "arbitrary", "arbitrary")),
        name="diff_attn" if diff else "mla_attn",
    )(*args)


def _postffn_kernel(a_ref, b_ref, c_ref, x_ref, mod_ref, wo_ref, gffn_ref, wg_ref, wv_ref, cw_ref,
                    cb_ref, wd_ref, hist_ref, gfin_ref, y_ref, conv_ref, carry_ref, act_ref, *, tm, final):
    t = pl.program_id(1)
    ffn = wg_ref.shape[1]
    cat = jnp.concatenate([a_ref[0], b_ref[0], c_ref[0]], axis=-1)
    mix = jnp.dot(cat, wo_ref[...], preferred_element_type=F32)
    gt1 = mod_ref[0, 2:3, :]
    sh2 = mod_ref[0, 3:4, :]
    sc2 = mod_ref[0, 4:5, :]
    gt2 = mod_ref[0, 5:6, :]
    x1 = x_ref[0] + gt1 * mix
    h2 = ((_rms(x1) * gffn_ref[...]) * (1.0 + sc2) + sh2).astype(BF16)

    @pl.when(t == 0)
    def _():
        carry_ref[...] = hist_ref[0]

    for c0 in range(0, ffn, FFN_CHUNK):
        sl = slice(c0, min(c0 + FFN_CHUNK, ffn))
        g = jnp.dot(h2, wg_ref[:, sl], preferred_element_type=F32)
        val = jnp.dot(h2, wv_ref[:, sl], preferred_element_type=F32)
        ext = jnp.concatenate([carry_ref[:, sl], g], axis=0)
        tail = g[tm - 8:, :]
        carry_ref[:, sl] = tail
        conv_ref[0, :, sl] = tail
        gc = (cw_ref[0:1, sl] * pltpu.roll(ext, 2, 0)[8:] + cw_ref[1:2, sl] * pltpu.roll(ext, 1, 0)[8:]
              + cw_ref[2:3, sl] * g + cb_ref[:, sl])
        act_ref[:, sl] = (gc * jax.nn.sigmoid(gc) * val).astype(BF16)
    f = jnp.dot(act_ref[...], wd_ref[...], preferred_element_type=F32)
    x2 = x1 + gt2 * f
    if final:
        x2 = _rms(x2) * gfin_ref[...]
    y_ref[0] = x2


def _postffn(a, bo, c, x, mod, wo, gffn, wg, wv, cw, cb, wd, hist8, gfin, final):
    b, t, d = x.shape
    f = wg.shape[1]
    tm = min(512, t)
    assert t % tm == 0 and tm >= 8
    tile = lambda n: pl.BlockSpec((1, tm, n), lambda i, j: (i, j, 0))

    def resident(shape):
        return pl.BlockSpec(shape, lambda i, j: (0,) * len(shape), pipeline_mode=pl.Buffered(1))

    return pl.pallas_call(
        functools.partial(_postffn_kernel, tm=tm, final=final),
        grid=(b, t // tm),
        in_specs=[tile(A_WIDTH), tile(B_WIDTH), tile(C_WIDTH), tile(d),
                  pl.BlockSpec((1, 6, d), lambda i, j: (i, 0, 0)),
                  resident(wo.shape), resident((1, d)), resident(wg.shape), resident(wv.shape),
                  resident((CONV_W, f)), resident((1, f)), resident(wd.shape),
                  pl.BlockSpec((1, 8, f), lambda i, j: (i, 0, 0)),
                  resident((1, d))],
        out_specs=[tile(d), pl.BlockSpec((1, 8, f), lambda i, j: (i, 0, 0))],
        out_shape=[jax.ShapeDtypeStruct((b, t, d), F32), jax.ShapeDtypeStruct((b, 8, f), F32)],
        scratch_shapes=[pltpu.VMEM((8, f), F32), pltpu.VMEM((tm, f), BF16)],
        compiler_params=_cparams(("arbitrary", "arbitrary")),
        name="postmix_ffn",
    )(a, bo, c, x, mod, wo, gffn, wg, wv, cw, cb, wd, hist8, gfin)


def _rot_cols(w):
    half = C_ROPE // 2
    return jnp.concatenate([-w[..., half:], w[..., :half]], axis=-1)


def _layer_weights(l, w_in, w_q_up, w_kv_up, pool_w, w_out, w_up):
    d = w_in.shape[1]
    wkr = w_in[l][:, OFF_CKR:]
    zpad = jnp.zeros((d, LANES - C_ROPE), F32)
    w1 = jnp.concatenate([w_in[l][:, :A_WIDTH], w_in[l][:, OFF_B:OFF_CKR], wkr, zpad, _rot_cols(wkr), zpad],
                         axis=1).astype(BF16)
    wkvt = w_in[l][:, A_WIDTH:OFF_B].T.astype(BF16)

    wq = w_q_up[l].reshape(C_Q_RANK, C_HEADS, C_NOPE + C_ROPE)
    nope, rope = wq[..., :C_NOPE], wq[..., C_NOPE:]
    zq = jnp.zeros((C_Q_RANK, C_HEADS, LANES - C_NOPE - C_ROPE), F32)
    main = jnp.concatenate([rope, nope, zq], axis=-1).reshape(C_Q_RANK, C_HEADS * LANES)
    rot = jnp.concatenate([_rot_cols(rope), jnp.zeros((C_Q_RANK, C_HEADS, LANES - C_ROPE), F32)],
                          axis=-1).reshape(C_Q_RANK, C_HEADS * LANES)
    wq2 = jnp.concatenate([main, rot], axis=1).astype(BF16)

    wkv = w_kv_up[l].reshape(C_KV_RANK, C_HEADS, C_NOPE + C_V)
    zk = jnp.zeros((C_KV_RANK, C_HEADS, C_ROPE), F32)
    kpart = jnp.concatenate([zk, wkv[..., :C_NOPE], zk], axis=-1).reshape(C_KV_RANK, C_HEADS * LANES)
    vpart = wkv[..., C_NOPE:].reshape(C_KV_RANK, C_WIDTH)
    wkv2 = jnp.concatenate([kpart, vpart], axis=1).astype(BF16)

    wp = jnp.zeros((B_WIDTH, B_WIDTH), F32)
    for g in range(B_GROUPS):
        wp = wp.at[g * B_CH:(g + 1) * B_CH, g * B_CH:(g + 1) * B_CH].set(pool_w[l, g])
    ffn = w_up.shape[2] // 2
    return dict(w1=w1, wkvt=wkvt, wq=wq2, wkv=wkv2, wp=wp.astype(BF16), wo=w_out[l].astype(BF16),
                wg=w_up[l][:, :ffn].astype(BF16), wv=w_up[l][:, ffn:].astype(BF16))


def _rope_tables(pos):
    half = C_ROPE // 2
    inv = 1.0 / (ROPE_BASE ** (jnp.arange(half, dtype=jnp.float32) / half))
    ang = pos.astype(jnp.float32)[:, None] * inv[None, :]
    cos, sin = jnp.cos(ang), jnp.sin(ang)
    n = pos.shape[0]
    z = jnp.zeros((n, LANES - C_ROPE), F32)
    cs = jnp.concatenate([cos, cos, z], axis=1)
    sn = jnp.concatenate([sin, sin, z], axis=1)
    csq = jnp.concatenate([cos, cos, jnp.ones((n, C_NOPE), F32), jnp.zeros((n, LANES - C_ROPE - C_NOPE), F32)], axis=1)
    return cs, sn, csq


def kernel(x_prompt, x_sample, c_prompt, c_sample, cache_a_k, cache_a_v, cache_c_latent, cache_c_krope,
           state_b_pool, state_ffn_conv, w_ada, b_ada, g_mix, w_in, lam_q1, lam_k1, lam_q2, lam_k2,
           a_subln_g, rel_bias, pool_w, pool_scale, c_q_norm_g, w_q_up, c_kv_norm_g, w_kv_up, w_out,
           g_ffn, w_up, conv_w, conv_b, w_down, g_final):
    depth = w_in.shape[0]
    bp, tp, d = x_prompt.shape
    bs, ts, _ = x_sample.shape
    past = cache_a_k.shape[2]
    ffn = conv_w.shape[2]
    blk = ATT_BLOCK

    mod_all = _ada(jnp.concatenate([c_prompt, c_sample], axis=0), w_ada, b_ada)
    mod_all = mod_all.reshape(depth, bp + bs, 6, d)

    r = np.arange(blk)
    rq = np.arange(min(CAUSAL_Q_BLOCK, tp))
    bkt, madd = _tile_bucket_mask(blk + rq, np.arange(blk + len(rq)))
    pb_near, bias_max = _bias_tiles(rel_bias, bkt, madd)
    mla_diag = _tile_bucket_mask(rq, rq)[1][None]
    rs = np.arange(ts)
    bkt, madd = _tile_bucket_mask(past + rs, past - blk + r)
    sb_sub, _ = _bias_tiles(rel_bias, bkt, madd)
    bkt, madd = _tile_bucket_mask(past + rs, past + rs)
    sb_diag, _ = _bias_tiles(rel_bias, bkt, madd)
    mla_sdiag = madd[None]

    tabs_p = _rope_tables(jnp.arange(tp, dtype=jnp.int32))
    tabs_s = _rope_tables(past + jnp.arange(ts, dtype=jnp.int32))
    cache_kt = jnp.transpose(cache_a_k, (0, 1, 3, 4, 2)).reshape(depth * bs, A_WIDTH, past)
    cache_vt = jnp.transpose(cache_a_v, (0, 1, 3, 4, 2)).reshape(depth * bs, A_WIDTH, past)
    zero_pool = jnp.zeros((bp, 16, B_WIDTH), F32)
    zero_conv = jnp.zeros((bp, 8, ffn), F32)

    hp, hs = x_prompt, x_sample
    new_p, new_s = [], []
    stack_p = stack_s = None
    for l in range(depth):
        w = _layer_weights(l, w_in, w_q_up, w_kv_up, pool_w, w_out, w_up)
        lam_init = 0.8 - 0.6 * math.exp(-0.3 * l)
        lamp = jnp.stack([lam_q1[l], lam_k1[l], lam_q2[l], lam_k2[l]], axis=0)
        g2 = jnp.concatenate([a_subln_g[l], a_subln_g[l]])[None]
        wd = w_down[l].astype(BF16)
        last = l == depth - 1

        def run(x, mod, tabs, pos0, hist16, hist8, pasts, prev):
            (aq, akt, avt, aktb, avtb, ub, bo, lat, kr, cq, ck, cv) = _premix(
                x, mod, g_mix[l][None], w["w1"], w["wkvt"], *tabs, c_q_norm_g[l][None], w["wq"],
                c_kv_norm_g[l][None], w["wkv"], w["wp"], pool_scale[l][None], hist16, pos0, prev)
            if pasts is None:
                a_out = _attention(aq, aktb, avtb, aktb, avtb, pb_near, None, diff=True, causal_blocks=True,
                                   lamp=lamp, g2=g2, bias_max=bias_max, lam_init=lam_init)
                c_out = _attention(cq, ck, cv, ck, cv, None, mla_diag, diff=False, causal_blocks=True)
            else:
                plat, pkr = pasts
                a_out = _attention(aq, cache_kt, cache_vt, aktb, avtb, sb_sub, sb_diag, diff=True,
                                   causal_blocks=False, lamp=lamp, g2=g2, bias_max=bias_max, lam_init=lam_init,
                                   all_batch0=l * bs)
                ckp, cvp = _kvpast(plat, pkr, w["wkv"])
                c_out = _attention(cq, ckp, cvp, ck, cv, None, mla_sdiag, diff=False, causal_blocks=False)
            y, conv8 = _postffn(a_out, bo, c_out, x, mod, w["wo"], g_ffn[l][None], w["wg"], w["wv"],
                                conv_w[l], conv_b[l][None], wd, hist8, g_final[None], last)
            tt = x.shape[1]
            return y, (akt, avt, lat), (kr, ub[:, tt - POOL_HIST:], conv8[:, 8 - (CONV_W - 1):])

        hp, stack_p, st = run(hp, mod_all[l, :bp], tabs_p, 0, zero_pool, zero_conv, None, stack_p)
        new_p.append(st)
        hist16 = jnp.pad(state_b_pool[l], ((0, 0), (1, 0), (0, 0)))
        hist8 = jnp.pad(state_ffn_conv[l], ((0, 0), (8 - (CONV_W - 1), 0), (0, 0)))
        hs, stack_s, st = run(hs, mod_all[l, bp:], tabs_s, past, hist16, hist8,
                              (cache_c_latent[l], cache_c_krope[l]), stack_s)
        new_s.append(st)

    def assemble(stack, small):
        akt, avt, lat = stack
        to_cache = lambda a: jnp.transpose(a.reshape(a.shape[:2] + (A_HEADS, A_V, a.shape[3])), (0, 1, 4, 2, 3))
        kr, pool, conv = [jnp.stack(z, axis=0) for z in zip(*small)]
        return to_cache(akt), to_cache(avt), lat, kr, pool, conv

    return (hp, hs, *assemble(stack_p, new_p), *assemble(stack_s, new_s))
```

```python
import functools
import math

import numpy as np
import jax
import jax.numpy as jnp
from jax import lax
from jax.experimental import pallas as pl
from jax.experimental.pallas import tpu as pltpu

F32 = jnp.float32
BF16 = jnp.bfloat16

CHUNK = 64
A_HEADS = 6
A_QK = 32
A_V = 64
B_GROUPS = 4
B_CH = 64
POOL_WINDOWS = (2, 4, 8, 16)
POOL_HIST = 15
C_HEADS = 6
C_NOPE = 64
C_ROPE = 32
C_V = 64
C_Q_RANK = 256
C_KV_RANK = 128
ROPE_BASE = 10000.0
REL_BUCKETS = 32
REL_MAX_DIST = 128
CONV_W = 3
EPS = 1e-6

A_WIDTH = A_HEADS * A_V
B_WIDTH = B_GROUPS * B_CH
C_WIDTH = C_HEADS * C_V
OFF_B = 3 * A_WIDTH
OFF_CQ = OFF_B + B_WIDTH
OFF_CKV = OFF_CQ + C_Q_RANK
OFF_CKR = OFF_CKV + C_KV_RANK

LANES = 128
ATT_BLOCK = 256
CAUSAL_Q_BLOCK = 512
FFN_CHUNK = 1024
LOG2E = math.log2(math.e)
NEG_BIG = -1e30
VMEM_LIMIT = 58 * 1024 * 1024
ADA_COLS = 1024
KEY_NORM_ROWS = 1024
REF_SLACK = 1.0 + 2.0 ** -6
DEN_MIN = 2.0 ** -80
DEN_MAX = 2.0 ** 100

U_AQ = 0
U_B = U_AQ + A_WIDTH
U_CQ = U_B + B_WIDTH
U_CKV = U_CQ + C_Q_RANK
U_KR = U_CKV + C_KV_RANK
N1 = U_KR + 2 * LANES


def _cparams(sem):
    return pltpu.CompilerParams(dimension_semantics=sem, vmem_limit_bytes=VMEM_LIMIT)


def _rms(x):
    return x * lax.rsqrt(jnp.mean(x * x, axis=-1, keepdims=True) + EPS)


def _ada_kernel(c_ref, w_ref, b_ref, o_ref):
    c = c_ref[...]
    s = (c * jax.nn.sigmoid(c)).astype(BF16)
    o_ref[0] = jnp.dot(s, w_ref[0].astype(BF16), preferred_element_type=F32) + b_ref[0]


def _ada(c_all, w_ada, b_ada):
    depth, d, n = w_ada.shape
    nb = c_all.shape[0]
    tn = ADA_COLS
    assert n % tn == 0
    return pl.pallas_call(
        _ada_kernel,
        grid=(depth, n // tn),
        in_specs=[pl.BlockSpec((nb, d), lambda l, j: (0, 0)),
                  pl.BlockSpec((1, d, tn), lambda l, j: (l, 0, j)),
                  pl.BlockSpec((1, 1, tn), lambda l, j: (l, 0, j))],
        out_specs=pl.BlockSpec((1, nb, tn), lambda l, j: (l, 0, j)),
        out_shape=jax.ShapeDtypeStruct((depth, nb, n), F32),
        compiler_params=_cparams(("arbitrary", "arbitrary")),
        name="ada_mod",
    )(c_all, w_ada, b_ada.reshape(depth, 1, n))


def _bias_kernel(rb_ref, bucket_ref, madd_ref, o_ref, mx_ref):
    h = pl.program_id(0)
    bucket = bucket_ref[...]
    far = rb_ref[REL_BUCKETS // 2 - 1, h]
    val = jnp.zeros(bucket.shape, F32)
    top = far - far
    for b in range(REL_BUCKETS):
        val = jnp.where(bucket == b, rb_ref[b, h] - far, val)
        top = jnp.maximum(top, rb_ref[b, h] - far)
    o_ref[0] = val * LOG2E + madd_ref[...]
    mx_ref[0] = jnp.full(mx_ref.shape[1:], top * LOG2E, F32)


def _bias_tiles(rel_bias, bucket, madd):
    r, c = bucket.shape
    return pl.pallas_call(
        _bias_kernel,
        grid=(A_HEADS,),
        in_specs=[pl.BlockSpec(memory_space=pltpu.SMEM),
                  pl.BlockSpec((r, c), lambda h: (0, 0)),
                  pl.BlockSpec((r, c), lambda h: (0, 0))],
        out_specs=[pl.BlockSpec((1, r, c), lambda h: (h, 0, 0)),
                   pl.BlockSpec((1, 8, LANES), lambda h: (h, 0, 0))],
        out_shape=[jax.ShapeDtypeStruct((A_HEADS, r, c), F32),
                   jax.ShapeDtypeStruct((A_HEADS, 8, LANES), F32)],
        compiler_params=_cparams(("arbitrary",)),
        name="rel_bias_tiles",
    )(rel_bias, bucket, madd)


def _t5_bucket(rel):
    half = REL_BUCKETS // 2
    exact = half // 2
    ret = jnp.where(rel > 0, half, 0)
    n = jnp.abs(rel)
    nf = jnp.maximum(n, 1).astype(jnp.float32)
    large = exact + (jnp.log(nf / exact) / math.log(REL_MAX_DIST / exact) * (half - exact)).astype(jnp.int32)
    large = jnp.minimum(large, half - 1)
    return ret + jnp.where(n < exact, n, large)


def _tile_bucket_mask(q_pos, k_pos):
    q_pos = jnp.asarray(q_pos, jnp.int32)
    k_pos = jnp.asarray(k_pos, jnp.int32)
    bucket = _t5_bucket(k_pos[None, :] - q_pos[:, None]).astype(jnp.int32)
    visible = (k_pos[None, :] // CHUNK) <= (q_pos[:, None] // CHUNK)
    return bucket, jnp.where(visible, 0.0, NEG_BIG).astype(F32)


def _premix_kernel(x_ref, mod_ref, gmix_ref, w1_ref, wkvt_ref, cs_ref, sn_ref, csq_ref, gq_ref, wq_ref,
                   gkv_ref, wkv_ref, wp_ref, ps_ref, hist_ref, *rest, tm, pos0, n_prev):
    prev = rest[:3] if n_prev else None
    (aq_ref, akt_ref, avt_ref, ub_ref, bo_ref, lat_ref, kr_ref,
     cq_ref, ck_ref, cv_ref, carry_ref) = rest[3:] if n_prev else rest
    t = pl.program_id(1)
    x = x_ref[0]
    sh1 = mod_ref[0, 0:1, :]
    sc1 = mod_ref[0, 1:2, :]
    h = ((_rms(x) * gmix_ref[...]) * (1.0 + sc1) + sh1).astype(BF16)
    u = jnp.dot(h, w1_ref[...], preferred_element_type=F32)
    kvt = lax.dot_general(wkvt_ref[...], h, (((1,), (1,)), ((), ())), preferred_element_type=F32)
    akt = kvt[0:A_WIDTH]
    avt = kvt[A_WIDTH:]
    if n_prev:
        akt_ref[0:n_prev, 0] = prev[0][:, 0]
        avt_ref[0:n_prev, 0] = prev[1][:, 0]
        lat_ref[0:n_prev, 0] = prev[2][:, 0]
    akt_ref[n_prev, 0] = akt
    avt_ref[n_prev, 0] = avt
    aq_ref[0] = (u[:, U_AQ:U_B] * (A_QK ** -0.5 * LOG2E)).astype(BF16)

    ub = u[:, U_B:U_CQ]
    ub_ref[0] = ub

    @pl.when(t == 0)
    def _():
        carry_ref[...] = hist_ref[0]

    ext = jnp.concatenate([carry_ref[...], ub], axis=0)
    carry_ref[...] = ub[tm - 16:, :]
    s2 = ext + pltpu.roll(ext, 1, 0)
    s4 = s2 + pltpu.roll(s2, 2, 0)
    s8 = s4 + pltpu.roll(s4, 4, 0)
    s16 = s8 + pltpu.roll(s8, 8, 0)
    lane = lax.broadcasted_iota(jnp.int32, (tm, B_WIDTH), 1)
    grp = lane // B_CH
    tot = jnp.where(grp == 0, s2[16:], jnp.where(grp == 1, s4[16:], jnp.where(grp == 2, s8[16:], s16[16:])))
    win = jnp.where(grp == 0, 2, jnp.where(grp == 1, 4, jnp.where(grp == 2, 8, 16)))
    pos = pos0 + t * tm + lax.broadcasted_iota(jnp.int32, (tm, B_WIDTH), 0)
    cnt = jnp.minimum(pos + 1, win).astype(F32)
    m = tot / cnt - ub
    y = jnp.dot(m.astype(BF16), wp_ref[...], preferred_element_type=F32) * ps_ref[...]
    bo_ref[0] = y.astype(BF16)

    cs = cs_ref[...]
    sn = sn_ref[...]
    csq = csq_ref[...]
    qn = (_rms(u[:, U_CQ:U_CKV]) * gq_ref[...]).astype(BF16)
    q2 = jnp.dot(qn, wq_ref[...], preferred_element_type=F32)
    qscale = (C_NOPE + C_ROPE) ** -0.5 * LOG2E
    lat = _rms(u[:, U_CKV:U_KR]) * gkv_ref[...]
    lat_ref[n_prev, 0] = lat
    krp = u[:, U_KR:U_KR + LANES] * cs + u[:, U_KR + LANES:N1] * sn
    kr_ref[0] = krp[:, 0:C_ROPE]
    kv = jnp.dot(lat.astype(BF16), wkv_ref[...], preferred_element_type=F32)
    for hh in range(C_HEADS):
        sl = slice(hh * LANES, (hh + 1) * LANES)
        sr = slice((C_HEADS + hh) * LANES, (C_HEADS + hh + 1) * LANES)
        cq_ref[0, :, sl] = ((q2[:, sl] * csq + q2[:, sr] * sn) * qscale).astype(BF16)
        ck_ref[0, :, sl] = (kv[:, sl] + krp).astype(BF16)
    cv_ref[0] = kv[:, C_HEADS * LANES:].astype(BF16)


def _premix(x, mod, gmix, w1, wkvt, cs, sn, csq, gq, wq, gkv, wkv, wp, ps, hist16, pos0, prev=None):
    b, t, d = x.shape
    n_prev = 0 if prev is None else prev[0].shape[0]
    nl = n_prev + 1
    tm = min(1024, t)
    assert t % tm == 0 and tm >= 16
    full = lambda shape: pl.BlockSpec(shape, lambda i, j: (0,) * len(shape))
    tile = lambda n: pl.BlockSpec((1, tm, n), lambda i, j: (i, j, 0))
    layers_t = lambda n: pl.BlockSpec((n, 1, A_WIDTH, tm), lambda i, j: (0, i, 0, j))
    layers = lambda n, w: pl.BlockSpec((n, 1, tm, w), lambda i, j: (0, i, j, 0))
    tab = pl.BlockSpec((tm, LANES), lambda i, j: (j, 0))
    outs = [(A_WIDTH, BF16), (None, F32), (None, F32),
            (B_WIDTH, F32), (B_WIDTH, BF16), (C_KV_RANK, F32), (C_ROPE, F32),
            (C_HEADS * LANES, BF16), (C_HEADS * LANES, BF16), (C_WIDTH, BF16)]
    out_specs = [None if n is None else tile(n) for n, _ in outs]
    out_shape = [None if n is None else jax.ShapeDtypeStruct((b, t, n), dt) for n, dt in outs]
    for idx in (1, 2):
        out_specs[idx] = layers_t(nl)
        out_shape[idx] = jax.ShapeDtypeStruct((nl, b, A_WIDTH, t), F32)
    out_specs[5] = layers(nl, C_KV_RANK)
    out_shape[5] = jax.ShapeDtypeStruct((nl, b, t, C_KV_RANK), F32)
    prev_specs = [layers_t(n_prev), layers_t(n_prev), layers(n_prev, C_KV_RANK)] if n_prev else []
    return pl.pallas_call(
        functools.partial(_premix_kernel, tm=tm, pos0=pos0, n_prev=n_prev),
        grid=(b, t // tm),
        in_specs=[tile(d),
                  pl.BlockSpec((1, 6, d), lambda i, j: (i, 0, 0)),
                  full((1, d)), full(w1.shape), full(wkvt.shape), tab, tab, tab,
                  full((1, C_Q_RANK)), full(wq.shape), full((1, C_KV_RANK)), full(wkv.shape),
                  full(wp.shape), full((1, B_WIDTH)),
                  pl.BlockSpec((1, 16, B_WIDTH), lambda i, j: (i, 0, 0))] + prev_specs,
        out_specs=out_specs,
        out_shape=out_shape,
        scratch_shapes=[pltpu.VMEM((16, B_WIDTH), F32)],
        compiler_params=_cparams(("arbitrary", "arbitrary")),
        name="premix",
    )(x, mod, gmix, w1, wkvt, cs, sn, csq, gq, wq, gkv, wkv, wp, ps, hist16, *(prev or ()))


def _kvpast_kernel(lat_ref, kr_ref, wkv_ref, e_ref, ck_ref, cv_ref):
    kv = jnp.dot(lat_ref[0].astype(BF16), wkv_ref[...], preferred_element_type=F32)
    krp = jnp.dot(kr_ref[0].astype(BF16), e_ref[...], preferred_element_type=F32)
    for hh in range(C_HEADS):
        sl = slice(hh * LANES, (hh + 1) * LANES)
        ck_ref[0, :, sl] = (kv[:, sl] + krp).astype(BF16)
    cv_ref[0] = kv[:, C_HEADS * LANES:].astype(BF16)


def _kvpast(lat, kr, wkv):
    b, p, _ = lat.shape
    tm = min(2048, p)
    assert p % tm == 0
    e = jnp.eye(C_ROPE, LANES, dtype=BF16)
    return pl.pallas_call(
        _kvpast_kernel,
        grid=(b, p // tm),
        in_specs=[pl.BlockSpec((1, tm, C_KV_RANK), lambda i, j: (i, j, 0)),
                  pl.BlockSpec((1, tm, C_ROPE), lambda i, j: (i, j, 0)),
                  pl.BlockSpec(wkv.shape, lambda i, j: (0, 0)),
                  pl.BlockSpec(e.shape, lambda i, j: (0, 0))],
        out_specs=[pl.BlockSpec((1, tm, C_HEADS * LANES), lambda i, j: (i, j, 0)),
                   pl.BlockSpec((1, tm, C_WIDTH), lambda i, j: (i, j, 0))],
        out_shape=[jax.ShapeDtypeStruct((b, p, C_HEADS * LANES), BF16),
                   jax.ShapeDtypeStruct((b, p, C_WIDTH), BF16)],
        compiler_params=_cparams(("arbitrary", "arbitrary")),
        name="mla_kv_past",
    )(lat, kr, wkv, e)


def _online_update(carry, s, v, pv):
    m, l, acc = carry
    m_new = jnp.maximum(m, jnp.max(s, axis=-1, keepdims=True))
    alpha = jnp.exp2(m - m_new)
    p = jnp.exp2(s - m_new)
    l = alpha * l + jnp.sum(p, axis=-1, keepdims=True)
    acc = alpha * acc + pv(p.astype(BF16), v)
    return m_new, l, acc


def _dot_nt(a, b):
    return lax.dot_general(a, b, (((1,), (1,)), ((), ())), preferred_element_type=F32)


def _attn_kernel(*refs, diff, sub_bias, diag_bias, bq, bk, n_all, lam_init):
    it = iter(refs)
    lamp_ref = next(it) if diff else None
    g_ref = next(it) if diff else None
    q_ref, ka_ref, va_ref, kd_ref, vd_ref = (next(it) for _ in range(5))
    bs_ref = next(it) if sub_bias else None
    bd_ref = next(it) if diag_bias else None
    bmax_ref = next(it) if diff else None
    o_ref = next(it)
    vaug_ref, kmax_ref, acc_ref = (next(it) for _ in range(3))
    causal = n_all is None
    merged = causal and sub_bias

    qi = pl.program_id(2)
    kt = diff
    key_axis = 2 if kt else 1
    t_all = ka_ref.shape[key_axis]
    bkd = kd_ref.shape[key_axis]

    def k_tile(off, width):
        return (ka_ref[0, :, pl.ds(off, width)] if kt else ka_ref[0, pl.ds(off, width), :]).astype(BF16)

    def v_tile(off, width):
        return (va_ref[0, :, pl.ds(off, width)] if kt else va_ref[0, pl.ds(off, width), :]).astype(BF16)

    def vaug_tile(off, width):
        return vaug_ref[:, pl.ds(off, width)] if kt else vaug_ref[pl.ds(off, width), :]

    def qk(qg, k, kcol):
        if kt:
            return jnp.dot(qg, k.astype(BF16), preferred_element_type=F32)
        return _dot_nt(qg, k[:, kcol * LANES:(kcol + 1) * LANES].astype(BF16))

    def pv(p, v):
        return _dot_nt(p, v.astype(BF16)) if kt else jnp.dot(p, v.astype(BF16), preferred_element_type=F32)
    na = qi * (bq // bk) if causal else n_all
    lane = lax.broadcasted_iota(jnp.int32, (bq, LANES), 1)
    q = q_ref[0]
    zero = jnp.zeros_like(q)
    ones_sq = jnp.ones((LANES, LANES), BF16)

    def vmask(v, n):
        ln = lax.broadcasted_iota(jnp.int32, (n, LANES), 1)
        return (ln >= A_QK * v) & (ln < A_QK * (v + 1))

    if diff:
        qs = jnp.concatenate([jnp.where(vmask(v, bq), q, zero) for v in range(4)], axis=0)
        groups = [(qs, 0, [0, 0, 1, 1])]
    else:
        groups = [(q[:, 0:LANES], 0, [0]), (q[:, LANES:2 * LANES], 1, [0])]
    n_var = sum(len(bi) for _, _, bi in groups)

    def key_norm2(k, var):
        kk = k.astype(BF16).astype(F32)
        kk = kk * kk
        if kt:
            row = lax.broadcasted_iota(jnp.int32, kk.shape, 0)
            kk = jnp.where((row >= A_QK * var) & (row < A_QK * (var + 1)), kk, 0.0)
            n2 = jnp.sum(kk, axis=0, keepdims=True)
            return jnp.broadcast_to(jnp.max(n2, axis=1, keepdims=True), (1, LANES))
        n2 = jnp.dot(kk.astype(BF16), ones_sq, preferred_element_type=F32)
        return jnp.max(n2, axis=0, keepdims=True)

    def var_keys(k, var):
        return k if diff else k[:, var * LANES:(var + 1) * LANES]

    @pl.when(qi == 0)
    def _():
        if kt:
            vaug_ref[0:LANES, :] = va_ref[0].astype(BF16)
            vaug_ref[LANES:, :] = jnp.ones((LANES, t_all), BF16)
        else:
            vaug_ref[:, 0:LANES] = va_ref[0].astype(BF16)
            vaug_ref[:, LANES:] = jnp.ones((t_all, LANES), BF16)
        rows = min(t_all, KEY_NORM_ROWS)
        for var in range(n_var):
            km = jnp.zeros((1, LANES), F32)
            for c in range(t_all // rows):
                km = jnp.maximum(km, key_norm2(var_keys(k_tile(c * rows, rows), var), var))
            kmax_ref[var * 8:(var + 1) * 8, :] = jnp.broadcast_to(km, (8, LANES))

    def stack_bias(ref, bias_idx):
        tiles = [ref[0] if ref.shape[0] == 1 else ref[i] for i in bias_idx]
        return tiles[0] if len(tiles) == 1 else jnp.concatenate(tiles, axis=0)

    kd = kd_ref[0].astype(BF16)
    vd = vd_ref[0].astype(BF16)
    has_sub = sub_bias
    if has_sub:
        n_far = jnp.maximum(na - 1, 0) if causal else max(na - 1, 0)
        off_s = pl.multiple_of(n_far * bk, bk) if causal else n_far * bk
    else:
        n_far = na

    def diag_biases():
        if merged:
            return [stack_bias(bs_ref, bi)[:, bk:] for _, _, bi in groups]
        return [stack_bias(bd_ref, bi) if diag_bias else None for _, _, bi in groups]

    def sub_biases():
        gate = jnp.where(na >= 1, 0.0, NEG_BIG)
        if merged:
            return [stack_bias(bs_ref, bi)[:, :bk] + gate for _, _, bi in groups]
        return [stack_bias(bs_ref, bi) + gate for _, _, bi in groups]

    refs_g = []
    var = 0
    for qg, kcol, bi in groups:
        qf = qg.astype(F32)
        qn2 = jnp.sum(qf * qf, axis=-1, keepdims=True)
        parts = []
        for i, b_i in enumerate(bi):
            km2 = kmax_ref[var * 8:var * 8 + 1, :]
            if n_all is not None:
                km2 = jnp.maximum(km2, key_norm2(var_keys(kd, var), var))
            r = jnp.sqrt(qn2[i * bq:(i + 1) * bq] * km2) * REF_SLACK
            if diff:
                r = r + bmax_ref[b_i, 0:1, :]
            parts.append(r)
            var += 1
        refs_g.append(parts[0] if len(parts) == 1 else jnp.concatenate(parts, axis=0))

    def probs(k, biases):
        out = []
        for (qg, kcol, _), mref, bias in zip(groups, refs_g, biases):
            s = qk(qg, k, kcol)
            if bias is not None:
                s = s + bias
            n = s.shape[1]
            if n >= LANES:
                cols = [jnp.exp2(s[:, c:c + LANES] - mref) for c in range(0, n, LANES)]
            else:
                cols = [jnp.exp2(s - mref[:, :n])]
            out.append((cols[0] if len(cols) == 1 else jnp.concatenate(cols, axis=1)).astype(BF16))
        return out[0] if len(out) == 1 else jnp.concatenate(out, axis=0)

    none_bias = [None] * len(groups)
    if kt:
        vd_aug = jnp.concatenate([vd, jnp.ones((LANES, bkd), BF16)], axis=0)
    else:
        vd_aug = jnp.concatenate([vd, jnp.ones((bkd, LANES), BF16)], axis=1)
    if merged:
        k_near = jnp.concatenate([k_tile(off_s, bk), kd], axis=key_axis - 1)
        v_near = jnp.concatenate([vaug_tile(off_s, bk), vd_aug], axis=key_axis - 1)
        bias_near = [jnp.concatenate([b_s, b_d], axis=1) for b_s, b_d in zip(sub_biases(), diag_biases())]
        acc_ref[...] = pv(probs(k_near, bias_near), v_near)
    else:
        acc_ref[...] = pv(probs(kd, diag_biases()), vd_aug)
        if has_sub:
            acc_ref[...] += pv(probs(k_tile(off_s, bk), sub_biases()), vaug_tile(off_s, bk))
    if causal:
        def far_step(off, width):
            off = pl.multiple_of(off, bk)
            acc_ref[...] += pv(probs(k_tile(off, width), none_bias), vaug_tile(off, width))

        def far_loop(j, c):
            far_step(j * (4 * bk), 4 * bk)
            return c

        n_wide = n_far // 4
        lax.fori_loop(0, n_wide, far_loop, 0)
        rem = n_far - 4 * n_wide
        base = n_wide * (4 * bk)
        pl.when((rem & 2) != 0)(lambda: far_step(base, 2 * bk))
        pl.when((rem & 1) != 0)(lambda: far_step(base + (rem & 2) * bk, bk))
    else:
        for off in range(0, n_far * bk, 4 * bk):
            width = min(4 * bk, n_far * bk - off)
            acc_ref[...] += pv(probs(k_tile(off, width), none_bias), vaug_tile(off, width))

    def finish(results):
        if diff:
            lp = lamp_ref[...]
            lam = (jnp.exp(jnp.sum(lp[0:1] * lp[1:2], axis=-1, keepdims=True))
                   - jnp.exp(jnp.sum(lp[2:3] * lp[3:4], axis=-1, keepdims=True)) + lam_init)
            outs = []
            for e in range(2):
                d = results[2 * e] - lam * results[2 * e + 1]
                own = (lane >= A_V * e) & (lane < A_V * (e + 1))
                ms = jnp.sum(jnp.where(own, d * d, 0.0), axis=-1, keepdims=True) * (1.0 / A_V)
                outs.append(d * lax.rsqrt(ms + EPS) * g_ref[...] * (1.0 - lam_init))
            out = jnp.where(lane < A_V, outs[0], outs[1])
        else:
            out = jnp.where(lane < C_V, results[0], results[1])
        o_ref[0] = out.astype(o_ref.dtype)

    acc = acc_ref[...]
    den = acc[:, LANES:]
    safe = (jnp.min(den) >= DEN_MIN) & (jnp.max(den) <= DEN_MAX)
    o_fast = acc[:, 0:LANES] / den
    finish([o_fast[i * bq:(i + 1) * bq] for i in range(n_var)])

    @pl.when(jnp.logical_not(safe))
    def _():
        def step(carries, k, v, biases):
            out = []
            for (qg, kcol, _), c, bias in zip(groups, carries, biases):
                s = qk(qg, k, kcol)
                out.append(_online_update(c, s if bias is None else s + bias, v, pv))
            return tuple(out)

        carries = tuple((jnp.full((len(bi) * bq, 1), NEG_BIG, F32), jnp.zeros((len(bi) * bq, 1), F32),
                         jnp.zeros((len(bi) * bq, LANES), F32)) for _, _, bi in groups)
        carries = step(carries, kd, vd, diag_biases())
        if has_sub:
            carries = step(carries, k_tile(off_s, bk), v_tile(off_s, bk), sub_biases())

        def body(j, cs):
            off = pl.multiple_of(j * bk, bk)
            return step(cs, k_tile(off, bk), v_tile(off, bk), none_bias)

        carries = lax.fori_loop(0, n_far, body, carries)
        results = []
        for (_, _, bi), (m, l, a) in zip(groups, carries):
            o = a / l
            results.extend(o[i * bq:(i + 1) * bq] for i in range(len(bi)))
        finish(results)


def _attention(q, k_all, v_all, k_new, v_new, bias_sub, bias_diag, *, diff, causal_blocks,
               lamp=None, g2=None, bias_max=None, lam_init=0.0, all_batch0=0, new_batch0=0):
    b, tq, _ = q.shape
    t_all = k_all.shape[2 if diff else 1]
    bk = ATT_BLOCK
    bq = min(CAUSAL_Q_BLOCK if causal_blocks else ATT_BLOCK, tq)
    assert tq % bq == 0 and t_all % bk == 0
    if causal_blocks:
        assert bq % bk == 0 and t_all == tq
    else:
        assert tq == bq
    qw = LANES if diff else 2 * LANES
    n_pairs = A_HEADS // 2
    in_specs, args = [], []
    if diff:
        in_specs += [pl.BlockSpec(lamp.shape, lambda i, p, j: (0, 0)),
                     pl.BlockSpec(g2.shape, lambda i, p, j: (0, 0))]
        args += [lamp, g2]
    in_specs.append(pl.BlockSpec((1, bq, qw), lambda i, p, j: (i, j, p)))
    if diff:
        in_specs += [pl.BlockSpec((1, LANES, t_all), lambda i, p, j: (i + all_batch0, p, 0)),
                     pl.BlockSpec((1, LANES, t_all), lambda i, p, j: (i + all_batch0, p, 0)),
                     pl.BlockSpec((1, LANES, bq), lambda i, p, j: (i + new_batch0, p, j)),
                     pl.BlockSpec((1, LANES, bq), lambda i, p, j: (i + new_batch0, p, j))]
    else:
        in_specs += [pl.BlockSpec((1, t_all, qw), lambda i, p, j: (i, 0, p)),
                     pl.BlockSpec((1, t_all, LANES), lambda i, p, j: (i, 0, p)),
                     pl.BlockSpec((1, bq, qw), lambda i, p, j: (i, j, p)),
                     pl.BlockSpec((1, bq, LANES), lambda i, p, j: (i, j, p))]
    args += [q, k_all, v_all, k_new, v_new]
    for bias in (bias_sub, bias_diag):
        if bias is not None:
            if bias.shape[0] == 1:
                in_specs.append(pl.BlockSpec(bias.shape, lambda i, p, j: (0, 0, 0)))
            else:
                in_specs.append(pl.BlockSpec((2,) + bias.shape[1:], lambda i, p, j: (p, 0, 0)))
            args.append(bias)
    if diff:
        in_specs.append(pl.BlockSpec((2, 8, LANES), lambda i, p, j: (p, 0, 0)))
        args.append(bias_max)
    n_var = 4 if diff else 2
    kern = functools.partial(
        _attn_kernel, diff=diff, sub_bias=bias_sub is not None, diag_bias=bias_diag is not None,
        bq=bq, bk=bk, n_all=None if causal_blocks else t_all // bk, lam_init=lam_init)
    return pl.pallas_call(
        kern,
        grid=(b, n_pairs, tq // bq),
        in_specs=in_specs,
        out_specs=pl.BlockSpec((1, bq, LANES), lambda i, p, j: (i, j, p)),
        out_shape=jax.ShapeDtypeStruct((b, tq, n_pairs * LANES), BF16),
        scratch_shapes=[pltpu.VMEM((2 * LANES, t_all) if diff else (t_all, 2 * LANES), BF16),
                        pltpu.VMEM((n_var * 8, LANES), F32),
                        pltpu.VMEM((n_var * bq, 2 * LANES), F32)],
        compiler_params=_cparams(("arbitrary", "arbitrary", "arbitrary")),
        name="diff_attn" if diff else "mla_attn",
    )(*args)


def _postffn_kernel(a_ref, b_ref, c_ref, x_ref, mod_ref, wo_ref, gffn_ref, wg_ref, wv_ref, cw_ref,
                    cb_ref, wd_ref, hist_ref, gfin_ref, y_ref, conv_ref, carry_ref, act_ref, *, tm, final):
    t = pl.program_id(1)
    ffn = wg_ref.shape[1]
    cat = jnp.concatenate([a_ref[0], b_ref[0], c_ref[0]], axis=-1)
    mix = jnp.dot(cat, wo_ref[...], preferred_element_type=F32)
    gt1 = mod_ref[0, 2:3, :]
    sh2 = mod_ref[0, 3:4, :]
    sc2 = mod_ref[0, 4:5, :]
    gt2 = mod_ref[0, 5:6, :]
    x1 = x_ref[0] + gt1 * mix
    h2 = ((_rms(x1) * gffn_ref[...]) * (1.0 + sc2) + sh2).astype(BF16)

    @pl.when(t == 0)
    def _():
        carry_ref[...] = hist_ref[0]

    for c0 in range(0, ffn, FFN_CHUNK):
        sl = slice(c0, min(c0 + FFN_CHUNK, ffn))
        g = jnp.dot(h2, wg_ref[:, sl], preferred_element_type=F32)
        val = jnp.dot(h2, wv_ref[:, sl], preferred_element_type=F32)
        ext = jnp.concatenate([carry_ref[:, sl], g], axis=0)
        tail = g[tm - 8:, :]
        carry_ref[:, sl] = tail
        conv_ref[0, :, sl] = tail
        gc = (cw_ref[0:1, sl] * pltpu.roll(ext, 2, 0)[8:] + cw_ref[1:2, sl] * pltpu.roll(ext, 1, 0)[8:]
              + cw_ref[2:3, sl] * g + cb_ref[:, sl])
        act_ref[:, sl] = (gc * jax.nn.sigmoid(gc) * val).astype(BF16)
    f = jnp.dot(act_ref[...], wd_ref[...], preferred_element_type=F32)
    x2 = x1 + gt2 * f
    if final:
        x2 = _rms(x2) * gfin_ref[...]
    y_ref[0] = x2


def _postffn(a, bo, c, x, mod, wo, gffn, wg, wv, cw, cb, wd, hist8, gfin, final):
    b, t, d = x.shape
    f = wg.shape[1]
    tm = min(512, t)
    assert t % tm == 0 and tm >= 8
    tile = lambda n: pl.BlockSpec((1, tm, n), lambda i, j: (i, j, 0))

    def resident(shape):
        return pl.BlockSpec(shape, lambda i, j: (0,) * len(shape), pipeline_mode=pl.Buffered(1))

    return pl.pallas_call(
        functools.partial(_postffn_kernel, tm=tm, final=final),
        grid=(b, t // tm),
        in_specs=[tile(A_WIDTH), tile(B_WIDTH), tile(C_WIDTH), tile(d),
                  pl.BlockSpec((1, 6, d), lambda i, j: (i, 0, 0)),
                  resident(wo.shape), resident((1, d)), resident(wg.shape), resident(wv.shape),
                  resident((CONV_W, f)), resident((1, f)), resident(wd.shape),
                  pl.BlockSpec((1, 8, f), lambda i, j: (i, 0, 0)),
                  resident((1, d))],
        out_specs=[tile(d), pl.BlockSpec((1, 8, f), lambda i, j: (i, 0, 0))],
        out_shape=[jax.ShapeDtypeStruct((b, t, d), F32), jax.ShapeDtypeStruct((b, 8, f), F32)],
        scratch_shapes=[pltpu.VMEM((8, f), F32), pltpu.VMEM((tm, f), BF16)],
        compiler_params=_cparams(("arbitrary", "arbitrary")),
        name="postmix_ffn",
    )(a, bo, c, x, mod, wo, gffn, wg, wv, cw, cb, wd, hist8, gfin)


def _rot_cols(w):
    half = C_ROPE // 2
    return jnp.concatenate([-w[..., half:], w[..., :half]], axis=-1)


def _layer_weights(l, w_in, w_q_up, w_kv_up, pool_w, w_out, w_up):
    d = w_in.shape[1]
    wkr = w_in[l][:, OFF_CKR:]
    zpad = jnp.zeros((d, LANES - C_ROPE), F32)
    w1 = jnp.concatenate([w_in[l][:, :A_WIDTH], w_in[l][:, OFF_B:OFF_CKR], wkr, zpad, _rot_cols(wkr), zpad],
                         axis=1).astype(BF16)
    wkvt = w_in[l][:, A_WIDTH:OFF_B].T.astype(BF16)

    wq = w_q_up[l].reshape(C_Q_RANK, C_HEADS, C_NOPE + C_ROPE)
    nope, rope = wq[..., :C_NOPE], wq[..., C_NOPE:]
    zq = jnp.zeros((C_Q_RANK, C_HEADS, LANES - C_NOPE - C_ROPE), F32)
    main = jnp.concatenate([rope, nope, zq], axis=-1).reshape(C_Q_RANK, C_HEADS * LANES)
    rot = jnp.concatenate([_rot_cols(rope), jnp.zeros((C_Q_RANK, C_HEADS, LANES - C_ROPE), F32)],
                          axis=-1).reshape(C_Q_RANK, C_HEADS * LANES)
    wq2 = jnp.concatenate([main, rot], axis=1).astype(BF16)

    wkv = w_kv_up[l].reshape(C_KV_RANK, C_HEADS, C_NOPE + C_V)
    zk = jnp.zeros((C_KV_RANK, C_HEADS, C_ROPE), F32)
    kpart = jnp.concatenate([zk, wkv[..., :C_NOPE], zk], axis=-1).reshape(C_KV_RANK, C_HEADS * LANES)
    vpart = wkv[..., C_NOPE:].reshape(C_KV_RANK, C_WIDTH)
    wkv2 = jnp.concatenate([kpart, vpart], axis=1).astype(BF16)

    wp = jnp.zeros((B_WIDTH, B_WIDTH), F32)
    for g in range(B_GROUPS):
        wp = wp.at[g * B_CH:(g + 1) * B_CH, g * B_CH:(g + 1) * B_CH].set(pool_w[l, g])
    ffn = w_up.shape[2] // 2
    return dict(w1=w1, wkvt=wkvt, wq=wq2, wkv=wkv2, wp=wp.astype(BF16), wo=w_out[l].astype(BF16),
                wg=w_up[l][:, :ffn].astype(BF16), wv=w_up[l][:, ffn:].astype(BF16))


def _rope_tables(pos):
    half = C_ROPE // 2
    inv = 1.0 / (ROPE_BASE ** (jnp.arange(half, dtype=jnp.float32) / half))
    ang = pos.astype(jnp.float32)[:, None] * inv[None, :]
    cos, sin = jnp.cos(ang), jnp.sin(ang)
    n = pos.shape[0]
    z = jnp.zeros((n, LANES - C_ROPE), F32)
    cs = jnp.concatenate([cos, cos, z], axis=1)
    sn = jnp.concatenate([sin, sin, z], axis=1)
    csq = jnp.concatenate([cos, cos, jnp.ones((n, C_NOPE), F32), jnp.zeros((n, LANES - C_ROPE - C_NOPE), F32)], axis=1)
    return cs, sn, csq


def kernel(x_prompt, x_sample, c_prompt, c_sample, cache_a_k, cache_a_v, cache_c_latent, cache_c_krope,
           state_b_pool, state_ffn_conv, w_ada, b_ada, g_mix, w_in, lam_q1, lam_k1, lam_q2, lam_k2,
           a_subln_g, rel_bias, pool_w, pool_scale, c_q_norm_g, w_q_up, c_kv_norm_g, w_kv_up, w_out,
           g_ffn, w_up, conv_w, conv_b, w_down, g_final):
    depth = w_in.shape[0]
    bp, tp, d = x_prompt.shape
    bs, ts, _ = x_sample.shape
    past = cache_a_k.shape[2]
    ffn = conv_w.shape[2]
    blk = ATT_BLOCK

    mod_all = _ada(jnp.concatenate([c_prompt, c_sample], axis=0), w_ada, b_ada)
    mod_all = mod_all.reshape(depth, bp + bs, 6, d)

    r = np.arange(blk)
    rq = np.arange(min(CAUSAL_Q_BLOCK, tp))
    bkt, madd = _tile_bucket_mask(blk + rq, np.arange(blk + len(rq)))
    pb_near, bias_max = _bias_tiles(rel_bias, bkt, madd)
    mla_diag = _tile_bucket_mask(rq, rq)[1][None]
    rs = np.arange(ts)
    bkt, madd = _tile_bucket_mask(past + rs, past - blk + r)
    sb_sub, _ = _bias_tiles(rel_bias, bkt, madd)
    bkt, madd = _tile_bucket_mask(past + rs, past + rs)
    sb_diag, _ = _bias_tiles(rel_bias, bkt, madd)
    mla_sdiag = madd[None]

    tabs_p = _rope_tables(jnp.arange(tp, dtype=jnp.int32))
    tabs_s = _rope_tables(past + jnp.arange(ts, dtype=jnp.int32))
    cache_kt = jnp.transpose(cache_a_k, (0, 1, 3, 4, 2)).reshape(depth * bs, A_WIDTH, past)
    cache_vt = jnp.transpose(cache_a_v, (0, 1, 3, 4, 2)).reshape(depth * bs, A_WIDTH, past)
    zero_pool = jnp.zeros((bp, 16, B_WIDTH), F32)
    zero_conv = jnp.zeros((bp, 8, ffn), F32)

    hp, hs = x_prompt, x_sample
    new_p, new_s = [], []
    stack_p = stack_s = None
    for l in range(depth):
        w = _layer_weights(l, w_in, w_q_up, w_kv_up, pool_w, w_out, w_up)
        lam_init = 0.8 - 0.6 * math.exp(-0.3 * l)
        lamp = jnp.stack([lam_q1[l], lam_k1[l], lam_q2[l], lam_k2[l]], axis=0)
        g2 = jnp.concatenate([a_subln_g[l], a_subln_g[l]])[None]
        wd = w_down[l].astype(BF16)
        last = l == depth - 1

        def run(x, mod, tabs, pos0, hist16, hist8, pasts, prev):
            (aq, akt, avt, ub, bo, lat, kr, cq, ck, cv) = _premix(
                x, mod, g_mix[l][None], w["w1"], w["wkvt"], *tabs, c_q_norm_g[l][None], w["wq"],
                c_kv_norm_g[l][None], w["wkv"], w["wp"], pool_scale[l][None], hist16, pos0, prev)
            bsz = x.shape[0]
            kt_new = akt.reshape((-1,) + akt.shape[2:])
            vt_new = avt.reshape((-1,) + avt.shape[2:])
            new0 = kt_new.shape[0] - bsz
            if pasts is None:
                a_out = _attention(aq, kt_new, vt_new, kt_new, vt_new, pb_near, None, diff=True,
                                   causal_blocks=True, lamp=lamp, g2=g2, bias_max=bias_max, lam_init=lam_init,
                                   all_batch0=new0, new_batch0=new0)
                c_out = _attention(cq, ck, cv, ck, cv, None, mla_diag, diff=False, causal_blocks=True)
            else:
                plat, pkr = pasts
                a_out = _attention(aq, cache_kt, cache_vt, kt_new, vt_new, sb_sub, sb_diag, diff=True,
                                   causal_blocks=False, lamp=lamp, g2=g2, bias_max=bias_max, lam_init=lam_init,
                                   all_batch0=l * bs, new_batch0=new0)
                ckp, cvp = _kvpast(plat, pkr, w["wkv"])
                c_out = _attention(cq, ckp, cvp, ck, cv, None, mla_sdiag, diff=False, causal_blocks=False)
            y, conv8 = _postffn(a_out, bo, c_out, x, mod, w["wo"], g_ffn[l][None], w["wg"], w["wv"],
                                conv_w[l], conv_b[l][None], wd, hist8, g_final[None], last)
            tt = x.shape[1]
            return y, (akt, avt, lat), (kr, ub[:, tt - POOL_HIST:], conv8[:, 8 - (CONV_W - 1):])

        hp, stack_p, st = run(hp, mod_all[l, :bp], tabs_p, 0, zero_pool, zero_conv, None, stack_p)
        new_p.append(st)
        hist16 = jnp.pad(state_b_pool[l], ((0, 0), (1, 0), (0, 0)))
        hist8 = jnp.pad(state_ffn_conv[l], ((0, 0), (8 - (CONV_W - 1), 0), (0, 0)))
        hs, stack_s, st = run(hs, mod_all[l, bp:], tabs_s, past, hist16, hist8,
                              (cache_c_latent[l], cache_c_krope[l]), stack_s)
        new_s.append(st)

    def assemble(stack, small):
        akt, avt, lat = stack
        to_cache = lambda a: jnp.transpose(a.reshape(a.shape[:2] + (A_HEADS, A_V, a.shape[3])), (0, 1, 4, 2, 3))
        kr, pool, conv = [jnp.stack(z, axis=0) for z in zip(*small)]
        return to_cache(akt), to_cache(avt), lat, kr, pool, conv

    return (hp, hs, *assemble(stack_p, new_p), *assemble(stack_s, new_s))
```

```python
import functools
import math

import numpy as np
import jax
import jax.numpy as jnp
from jax import lax
from jax.experimental import pallas as pl
from jax.experimental.pallas import tpu as pltpu

F32 = jnp.float32
BF16 = jnp.bfloat16

CHUNK = 64
A_HEADS = 6
A_QK = 32
A_V = 64
B_GROUPS = 4
B_CH = 64
POOL_WINDOWS = (2, 4, 8, 16)
POOL_HIST = 15
C_HEADS = 6
C_NOPE = 64
C_ROPE = 32
C_V = 64
C_Q_RANK = 256
C_KV_RANK = 128
ROPE_BASE = 10000.0
REL_BUCKETS = 32
REL_MAX_DIST = 128
CONV_W = 3
EPS = 1e-6

A_WIDTH = A_HEADS * A_V
B_WIDTH = B_GROUPS * B_CH
C_WIDTH = C_HEADS * C_V
OFF_B = 3 * A_WIDTH
OFF_CQ = OFF_B + B_WIDTH
OFF_CKV = OFF_CQ + C_Q_RANK
OFF_CKR = OFF_CKV + C_KV_RANK

LANES = 128
ATT_BLOCK = 256
CAUSAL_Q_BLOCK = 512
FFN_CHUNK = 1024
LOG2E = math.log2(math.e)
NEG_BIG = -1e30
VMEM_LIMIT = 58 * 1024 * 1024
ADA_COLS = 1024
KEY_NORM_ROWS = 1024
REF_SLACK = 1.0 + 2.0 ** -6
DEN_MIN = 2.0 ** -80
DEN_MAX = 2.0 ** 100

U_AQ = 0
U_B = U_AQ + A_WIDTH
U_CQ = U_B + B_WIDTH
U_CKV = U_CQ + C_Q_RANK
U_KR = U_CKV + C_KV_RANK
N1 = U_KR + 2 * LANES


def _cparams(sem):
    return pltpu.CompilerParams(dimension_semantics=sem, vmem_limit_bytes=VMEM_LIMIT)


def _rms(x):
    return x * lax.rsqrt(jnp.mean(x * x, axis=-1, keepdims=True) + EPS)


def _ada_kernel(c_ref, w_ref, b_ref, o_ref):
    c = c_ref[...]
    s = (c * jax.nn.sigmoid(c)).astype(BF16)
    o_ref[0] = jnp.dot(s, w_ref[0].astype(BF16), preferred_element_type=F32) + b_ref[0]


def _ada(c_all, w_ada, b_ada):
    depth, d, n = w_ada.shape
    nb = c_all.shape[0]
    tn = ADA_COLS
    assert n % tn == 0
    return pl.pallas_call(
        _ada_kernel,
        grid=(depth, n // tn),
        in_specs=[pl.BlockSpec((nb, d), lambda l, j: (0, 0)),
                  pl.BlockSpec((1, d, tn), lambda l, j: (l, 0, j)),
                  pl.BlockSpec((1, 1, tn), lambda l, j: (l, 0, j))],
        out_specs=pl.BlockSpec((1, nb, tn), lambda l, j: (l, 0, j)),
        out_shape=jax.ShapeDtypeStruct((depth, nb, n), F32),
        compiler_params=_cparams(("arbitrary", "arbitrary")),
        name="ada_mod",
    )(c_all, w_ada, b_ada.reshape(depth, 1, n))


def _bias_kernel(rb_ref, bucket_ref, madd_ref, o_ref, mx_ref):
    h = pl.program_id(0)
    bucket = bucket_ref[...]
    far = rb_ref[REL_BUCKETS // 2 - 1, h]
    val = jnp.zeros(bucket.shape, F32)
    top = far - far
    for b in range(REL_BUCKETS):
        val = jnp.where(bucket == b, rb_ref[b, h] - far, val)
        top = jnp.maximum(top, rb_ref[b, h] - far)
    o_ref[0] = val * LOG2E + madd_ref[...]
    mx_ref[0] = jnp.full(mx_ref.shape[1:], top * LOG2E, F32)


def _bias_tiles(rel_bias, bucket, madd):
    r, c = bucket.shape
    return pl.pallas_call(
        _bias_kernel,
        grid=(A_HEADS,),
        in_specs=[pl.BlockSpec(memory_space=pltpu.SMEM),
                  pl.BlockSpec((r, c), lambda h: (0, 0)),
                  pl.BlockSpec((r, c), lambda h: (0, 0))],
        out_specs=[pl.BlockSpec((1, r, c), lambda h: (h, 0, 0)),
                   pl.BlockSpec((1, 8, LANES), lambda h: (h, 0, 0))],
        out_shape=[jax.ShapeDtypeStruct((A_HEADS, r, c), F32),
                   jax.ShapeDtypeStruct((A_HEADS, 8, LANES), F32)],
        compiler_params=_cparams(("arbitrary",)),
        name="rel_bias_tiles",
    )(rel_bias, bucket, madd)


def _t5_bucket(rel):
    half = REL_BUCKETS // 2
    exact = half // 2
    ret = jnp.where(rel > 0, half, 0)
    n = jnp.abs(rel)
    nf = jnp.maximum(n, 1).astype(jnp.float32)
    large = exact + (jnp.log(nf / exact) / math.log(REL_MAX_DIST / exact) * (half - exact)).astype(jnp.int32)
    large = jnp.minimum(large, half - 1)
    return ret + jnp.where(n < exact, n, large)


def _tile_bucket_mask(q_pos, k_pos):
    q_pos = jnp.asarray(q_pos, jnp.int32)
    k_pos = jnp.asarray(k_pos, jnp.int32)
    bucket = _t5_bucket(k_pos[None, :] - q_pos[:, None]).astype(jnp.int32)
    visible = (k_pos[None, :] // CHUNK) <= (q_pos[:, None] // CHUNK)
    return bucket, jnp.where(visible, 0.0, NEG_BIG).astype(F32)


def _premix_kernel(x_ref, mod_ref, gmix_ref, w1_ref, wkvt_ref, cs_ref, sn_ref, csq_ref, gq_ref, wq_ref,
                   gkv_ref, wkv_ref, wp_ref, ps_ref, hist_ref, *rest, tm, pos0, n_prev):
    prev = rest[:3] if n_prev else None
    (aq_ref, akt_ref, avt_ref, ub_ref, bo_ref, lat_ref, kr_ref,
     cq_ref, ck_ref, cv_ref, carry_ref) = rest[3:] if n_prev else rest
    t = pl.program_id(1)
    x = x_ref[0]
    sh1 = mod_ref[0, 0:1, :]
    sc1 = mod_ref[0, 1:2, :]
    h = ((_rms(x) * gmix_ref[...]) * (1.0 + sc1) + sh1).astype(BF16)
    u = jnp.dot(h, w1_ref[...], preferred_element_type=F32)
    kvt = lax.dot_general(wkvt_ref[...], h, (((1,), (1,)), ((), ())), preferred_element_type=F32)
    akt = kvt[0:A_WIDTH]
    avt = kvt[A_WIDTH:]
    if n_prev:
        akt_ref[0:n_prev, 0] = prev[0][:, 0]
        avt_ref[0:n_prev, 0] = prev[1][:, 0]
        lat_ref[0:n_prev, 0] = prev[2][:, 0]
    akt_ref[n_prev, 0] = akt
    avt_ref[n_prev, 0] = avt
    aq_ref[0] = (u[:, U_AQ:U_B] * (A_QK ** -0.5 * LOG2E)).astype(BF16)

    ub = u[:, U_B:U_CQ]
    ub_ref[0] = ub

    @pl.when(t == 0)
    def _():
        carry_ref[...] = hist_ref[0]

    ext = jnp.concatenate([carry_ref[...], ub], axis=0)
    carry_ref[...] = ub[tm - 16:, :]
    s2 = ext + pltpu.roll(ext, 1, 0)
    s4 = s2 + pltpu.roll(s2, 2, 0)
    s8 = s4 + pltpu.roll(s4, 4, 0)
    s16 = s8 + pltpu.roll(s8, 8, 0)
    lane = lax.broadcasted_iota(jnp.int32, (tm, B_WIDTH), 1)
    grp = lane // B_CH
    tot = jnp.where(grp == 0, s2[16:], jnp.where(grp == 1, s4[16:], jnp.where(grp == 2, s8[16:], s16[16:])))
    win = jnp.where(grp == 0, 2, jnp.where(grp == 1, 4, jnp.where(grp == 2, 8, 16)))
    pos = pos0 + t * tm + lax.broadcasted_iota(jnp.int32, (tm, B_WIDTH), 0)
    cnt = jnp.minimum(pos + 1, win).astype(F32)
    m = tot / cnt - ub
    y = jnp.dot(m.astype(BF16), wp_ref[...], preferred_element_type=F32) * ps_ref[...]
    bo_ref[0] = y.astype(BF16)

    cs = cs_ref[...]
    sn = sn_ref[...]
    csq = csq_ref[...]
    qn = (_rms(u[:, U_CQ:U_CKV]) * gq_ref[...]).astype(BF16)
    q2 = jnp.dot(qn, wq_ref[...], preferred_element_type=F32)
    qscale = (C_NOPE + C_ROPE) ** -0.5 * LOG2E
    lat = _rms(u[:, U_CKV:U_KR]) * gkv_ref[...]
    lat_ref[n_prev, 0] = lat
    krp = u[:, U_KR:U_KR + LANES] * cs + u[:, U_KR + LANES:N1] * sn
    kr_ref[0] = krp[:, 0:C_ROPE]
    kv = jnp.dot(lat.astype(BF16), wkv_ref[...], preferred_element_type=F32)
    for hh in range(C_HEADS):
        sl = slice(hh * LANES, (hh + 1) * LANES)
        sr = slice((C_HEADS + hh) * LANES, (C_HEADS + hh + 1) * LANES)
        cq_ref[0, :, sl] = ((q2[:, sl] * csq + q2[:, sr] * sn) * qscale).astype(BF16)
        ck_ref[0, :, sl] = (kv[:, sl] + krp).astype(BF16)
    cv_ref[0] = kv[:, C_HEADS * LANES:].astype(BF16)


def _premix(x, mod, gmix, w1, wkvt, cs, sn, csq, gq, wq, gkv, wkv, wp, ps, hist16, pos0, prev=None):
    b, t, d = x.shape
    n_prev = 0 if prev is None else prev[0].shape[0]
    nl = n_prev + 1
    tm = min(1024, t)
    assert t % tm == 0 and tm >= 16
    full = lambda shape: pl.BlockSpec(shape, lambda i, j: (0,) * len(shape))
    tile = lambda n: pl.BlockSpec((1, tm, n), lambda i, j: (i, j, 0))
    layers_t = lambda n: pl.BlockSpec((n, 1, A_WIDTH, tm), lambda i, j: (0, i, 0, j))
    layers = lambda n, w: pl.BlockSpec((n, 1, tm, w), lambda i, j: (0, i, j, 0))
    tab = pl.BlockSpec((tm, LANES), lambda i, j: (j, 0))
    outs = [(A_WIDTH, BF16), (None, F32), (None, F32),
            (B_WIDTH, F32), (B_WIDTH, BF16), (C_KV_RANK, F32), (C_ROPE, F32),
            (C_HEADS * LANES, BF16), (C_HEADS * LANES, BF16), (C_WIDTH, BF16)]
    out_specs = [None if n is None else tile(n) for n, _ in outs]
    out_shape = [None if n is None else jax.ShapeDtypeStruct((b, t, n), dt) for n, dt in outs]
    for idx in (1, 2):
        out_specs[idx] = layers_t(nl)
        out_shape[idx] = jax.ShapeDtypeStruct((nl, b, A_WIDTH, t), F32)
    out_specs[5] = layers(nl, C_KV_RANK)
    out_shape[5] = jax.ShapeDtypeStruct((nl, b, t, C_KV_RANK), F32)
    prev_specs = [layers_t(n_prev), layers_t(n_prev), layers(n_prev, C_KV_RANK)] if n_prev else []
    return pl.pallas_call(
        functools.partial(_premix_kernel, tm=tm, pos0=pos0, n_prev=n_prev),
        grid=(b, t // tm),
        in_specs=[tile(d),
                  pl.BlockSpec((1, 6, d), lambda i, j: (i, 0, 0)),
                  full((1, d)), full(w1.shape), full(wkvt.shape), tab, tab, tab,
                  full((1, C_Q_RANK)), full(wq.shape), full((1, C_KV_RANK)), full(wkv.shape),
                  full(wp.shape), full((1, B_WIDTH)),
                  pl.BlockSpec((1, 16, B_WIDTH), lambda i, j: (i, 0, 0))] + prev_specs,
        out_specs=out_specs,
        out_shape=out_shape,
        scratch_shapes=[pltpu.VMEM((16, B_WIDTH), F32)],
        compiler_params=_cparams(("arbitrary", "arbitrary")),
        name="premix",
    )(x, mod, gmix, w1, wkvt, cs, sn, csq, gq, wq, gkv, wkv, wp, ps, hist16, *(prev or ()))


def _kvpast_kernel(lat_ref, kr_ref, wkv_ref, e_ref, ck_ref, cv_ref):
    kv = jnp.dot(lat_ref[0].astype(BF16), wkv_ref[...], preferred_element_type=F32)
    krp = jnp.dot(kr_ref[0].astype(BF16), e_ref[...], preferred_element_type=F32)
    for hh in range(C_HEADS):
        sl = slice(hh * LANES, (hh + 1) * LANES)
        ck_ref[0, :, sl] = (kv[:, sl] + krp).astype(BF16)
    cv_ref[0] = kv[:, C_HEADS * LANES:].astype(BF16)


def _kvpast(lat, kr, wkv):
    b, p, _ = lat.shape
    tm = min(2048, p)
    assert p % tm == 0
    e = jnp.eye(C_ROPE, LANES, dtype=BF16)
    return pl.pallas_call(
        _kvpast_kernel,
        grid=(b, p // tm),
        in_specs=[pl.BlockSpec((1, tm, C_KV_RANK), lambda i, j: (i, j, 0)),
                  pl.BlockSpec((1, tm, C_ROPE), lambda i, j: (i, j, 0)),
                  pl.BlockSpec(wkv.shape, lambda i, j: (0, 0)),
                  pl.BlockSpec(e.shape, lambda i, j: (0, 0))],
        out_specs=[pl.BlockSpec((1, tm, C_HEADS * LANES), lambda i, j: (i, j, 0)),
                   pl.BlockSpec((1, tm, C_WIDTH), lambda i, j: (i, j, 0))],
        out_shape=[jax.ShapeDtypeStruct((b, p, C_HEADS * LANES), BF16),
                   jax.ShapeDtypeStruct((b, p, C_WIDTH), BF16)],
        compiler_params=_cparams(("arbitrary", "arbitrary")),
        name="mla_kv_past",
    )(lat, kr, wkv, e)


def _online_update(carry, s, v, pv):
    m, l, acc = carry
    m_new = jnp.maximum(m, jnp.max(s, axis=-1, keepdims=True))
    alpha = jnp.exp2(m - m_new)
    p = jnp.exp2(s - m_new)
    l = alpha * l + jnp.sum(p, axis=-1, keepdims=True)
    acc = alpha * acc + pv(p.astype(BF16), v)
    return m_new, l, acc


def _dot_nt(a, b):
    return lax.dot_general(a, b, (((1,), (1,)), ((), ())), preferred_element_type=F32)


def _attn_kernel(*refs, diff, sub_bias, diag_bias, bq, bk, n_all, lam_init):
    it = iter(refs)
    lamp_ref = next(it) if diff else None
    g_ref = next(it) if diff else None
    q_ref, ka_ref, va_ref, kd_ref, vd_ref = (next(it) for _ in range(5))
    bs_ref = next(it) if sub_bias else None
    bd_ref = next(it) if diag_bias else None
    bmax_ref = next(it) if diff else None
    o_ref = next(it)
    vaug_ref, kmax_ref, acc_ref = (next(it) for _ in range(3))
    kbf_ref = next(it) if diff else None
    causal = n_all is None
    merged = causal and sub_bias

    qi = pl.program_id(2)
    kt = diff
    key_axis = 2 if kt else 1
    t_all = ka_ref.shape[key_axis]
    bkd = kd_ref.shape[key_axis]

    def k_tile(off, width):
        return kbf_ref[:, pl.ds(off, width)] if kt else ka_ref[0, pl.ds(off, width), :]

    def v_tile(off, width):
        return (va_ref[0, :, pl.ds(off, width)] if kt else va_ref[0, pl.ds(off, width), :]).astype(BF16)

    def vaug_tile(off, width):
        return vaug_ref[:, pl.ds(off, width)] if kt else vaug_ref[pl.ds(off, width), :]

    def qk(qg, k, kcol):
        if kt:
            return jnp.dot(qg, k.astype(BF16), preferred_element_type=F32)
        return _dot_nt(qg, k[:, kcol * LANES:(kcol + 1) * LANES].astype(BF16))

    def pv(p, v):
        return _dot_nt(p, v.astype(BF16)) if kt else jnp.dot(p, v.astype(BF16), preferred_element_type=F32)
    na = qi * (bq // bk) if causal else n_all
    lane = lax.broadcasted_iota(jnp.int32, (bq, LANES), 1)
    q = q_ref[0]
    zero = jnp.zeros_like(q)
    ones_sq = jnp.ones((LANES, LANES), BF16)

    def vmask(v, n):
        ln = lax.broadcasted_iota(jnp.int32, (n, LANES), 1)
        return (ln >= A_QK * v) & (ln < A_QK * (v + 1))

    if diff:
        qs = jnp.concatenate([jnp.where(vmask(v, bq), q, zero) for v in range(4)], axis=0)
        groups = [(qs, 0, [0, 0, 1, 1])]
    else:
        groups = [(q[:, 0:LANES], 0, [0]), (q[:, LANES:2 * LANES], 1, [0])]
    n_var = sum(len(bi) for _, _, bi in groups)

    def key_norm2(k, var):
        kk = k.astype(BF16).astype(F32)
        kk = kk * kk
        if kt:
            row = lax.broadcasted_iota(jnp.int32, kk.shape, 0)
            kk = jnp.where((row >= A_QK * var) & (row < A_QK * (var + 1)), kk, 0.0)
            n2 = jnp.sum(kk, axis=0, keepdims=True)
            return jnp.broadcast_to(jnp.max(n2, axis=1, keepdims=True), (1, LANES))
        n2 = jnp.dot(kk.astype(BF16), ones_sq, preferred_element_type=F32)
        return jnp.max(n2, axis=0, keepdims=True)

    def var_keys(k, var):
        return k if diff else k[:, var * LANES:(var + 1) * LANES]

    @pl.when(qi == 0)
    def _():
        if kt:
            kbf_ref[...] = ka_ref[0].astype(BF16)
            vaug_ref[0:LANES, :] = va_ref[0].astype(BF16)
            vaug_ref[LANES:, :] = jnp.ones((LANES, t_all), BF16)
        else:
            vaug_ref[:, 0:LANES] = va_ref[0].astype(BF16)
            vaug_ref[:, LANES:] = jnp.ones((t_all, LANES), BF16)
        rows = min(t_all, KEY_NORM_ROWS)
        for var in range(n_var):
            km = jnp.zeros((1, LANES), F32)
            for c in range(t_all // rows):
                km = jnp.maximum(km, key_norm2(var_keys(k_tile(c * rows, rows), var), var))
            kmax_ref[var * 8:(var + 1) * 8, :] = jnp.broadcast_to(km, (8, LANES))

    def stack_bias(ref, bias_idx):
        tiles = [ref[0] if ref.shape[0] == 1 else ref[i] for i in bias_idx]
        return tiles[0] if len(tiles) == 1 else jnp.concatenate(tiles, axis=0)

    kd = kd_ref[0].astype(BF16)
    vd = vd_ref[0].astype(BF16)
    has_sub = sub_bias
    if has_sub:
        n_far = jnp.maximum(na - 1, 0) if causal else max(na - 1, 0)
        off_s = pl.multiple_of(n_far * bk, bk) if causal else n_far * bk
    else:
        n_far = na

    def diag_biases():
        if merged:
            return [stack_bias(bs_ref, bi)[:, bk:] for _, _, bi in groups]
        return [stack_bias(bd_ref, bi) if diag_bias else None for _, _, bi in groups]

    def sub_biases():
        gate = jnp.where(na >= 1, 0.0, NEG_BIG)
        if merged:
            return [stack_bias(bs_ref, bi)[:, :bk] + gate for _, _, bi in groups]
        return [stack_bias(bs_ref, bi) + gate for _, _, bi in groups]

    refs_g = []
    var = 0
    for qg, kcol, bi in groups:
        qf = qg.astype(F32)
        qn2 = jnp.sum(qf * qf, axis=-1, keepdims=True)
        parts = []
        for i, b_i in enumerate(bi):
            km2 = kmax_ref[var * 8:var * 8 + 1, :]
            if n_all is not None:
                km2 = jnp.maximum(km2, key_norm2(var_keys(kd, var), var))
            r = jnp.sqrt(qn2[i * bq:(i + 1) * bq] * km2) * REF_SLACK
            if diff:
                r = r + bmax_ref[b_i, 0:1, :]
            parts.append(r)
            var += 1
        refs_g.append(parts[0] if len(parts) == 1 else jnp.concatenate(parts, axis=0))

    def probs(k, biases):
        out = []
        for (qg, kcol, _), mref, bias in zip(groups, refs_g, biases):
            s = qk(qg, k, kcol)
            if bias is not None:
                s = s + bias
            n = s.shape[1]
            if n >= LANES:
                cols = [jnp.exp2(s[:, c:c + LANES] - mref) for c in range(0, n, LANES)]
            else:
                cols = [jnp.exp2(s - mref[:, :n])]
            out.append((cols[0] if len(cols) == 1 else jnp.concatenate(cols, axis=1)).astype(BF16))
        return out[0] if len(out) == 1 else jnp.concatenate(out, axis=0)

    none_bias = [None] * len(groups)
    if kt:
        vd_aug = jnp.concatenate([vd, jnp.ones((LANES, bkd), BF16)], axis=0)
    else:
        vd_aug = jnp.concatenate([vd, jnp.ones((bkd, LANES), BF16)], axis=1)
    if merged:
        k_near = jnp.concatenate([k_tile(off_s, bk), kd], axis=key_axis - 1)
        v_near = jnp.concatenate([vaug_tile(off_s, bk), vd_aug], axis=key_axis - 1)
        bias_near = [jnp.concatenate([b_s, b_d], axis=1) for b_s, b_d in zip(sub_biases(), diag_biases())]
        acc_ref[...] = pv(probs(k_near, bias_near), v_near)
    else:
        acc_ref[...] = pv(probs(kd, diag_biases()), vd_aug)
        if has_sub:
            acc_ref[...] += pv(probs(k_tile(off_s, bk), sub_biases()), vaug_tile(off_s, bk))
    if causal:
        def far_step(off, width):
            off = pl.multiple_of(off, bk)
            acc_ref[...] += pv(probs(k_tile(off, width), none_bias), vaug_tile(off, width))

        def far_loop(j, c):
            far_step(j * (4 * bk), 4 * bk)
            return c

        n_wide = n_far // 4
        lax.fori_loop(0, n_wide, far_loop, 0)
        rem = n_far - 4 * n_wide
        base = n_wide * (4 * bk)
        pl.when((rem & 2) != 0)(lambda: far_step(base, 2 * bk))
        pl.when((rem & 1) != 0)(lambda: far_step(base + (rem & 2) * bk, bk))
    else:
        for off in range(0, n_far * bk, 4 * bk):
            width = min(4 * bk, n_far * bk - off)
            acc_ref[...] += pv(probs(k_tile(off, width), none_bias), vaug_tile(off, width))

    def finish(results):
        if diff:
            lp = lamp_ref[...]
            lam = (jnp.exp(jnp.sum(lp[0:1] * lp[1:2], axis=-1, keepdims=True))
                   - jnp.exp(jnp.sum(lp[2:3] * lp[3:4], axis=-1, keepdims=True)) + lam_init)
            outs = []
            for e in range(2):
                d = results[2 * e] - lam * results[2 * e + 1]
                own = (lane >= A_V * e) & (lane < A_V * (e + 1))
                ms = jnp.sum(jnp.where(own, d * d, 0.0), axis=-1, keepdims=True) * (1.0 / A_V)
                outs.append(d * lax.rsqrt(ms + EPS) * g_ref[...] * (1.0 - lam_init))
            out = jnp.where(lane < A_V, outs[0], outs[1])
        else:
            out = jnp.where(lane < C_V, results[0], results[1])
        o_ref[0] = out.astype(o_ref.dtype)

    acc = acc_ref[...]
    den = acc[:, LANES:]
    safe = (jnp.min(den) >= DEN_MIN) & (jnp.max(den) <= DEN_MAX)
    o_fast = acc[:, 0:LANES] / den
    finish([o_fast[i * bq:(i + 1) * bq] for i in range(n_var)])

    @pl.when(jnp.logical_not(safe))
    def _():
        def step(carries, k, v, biases):
            out = []
            for (qg, kcol, _), c, bias in zip(groups, carries, biases):
                s = qk(qg, k, kcol)
                out.append(_online_update(c, s if bias is None else s + bias, v, pv))
            return tuple(out)

        carries = tuple((jnp.full((len(bi) * bq, 1), NEG_BIG, F32), jnp.zeros((len(bi) * bq, 1), F32),
                         jnp.zeros((len(bi) * bq, LANES), F32)) for _, _, bi in groups)
        carries = step(carries, kd, vd, diag_biases())
        if has_sub:
            carries = step(carries, k_tile(off_s, bk), v_tile(off_s, bk), sub_biases())

        def body(j, cs):
            off = pl.multiple_of(j * bk, bk)
            return step(cs, k_tile(off, bk), v_tile(off, bk), none_bias)

        carries = lax.fori_loop(0, n_far, body, carries)
        results = []
        for (_, _, bi), (m, l, a) in zip(groups, carries):
            o = a / l
            results.extend(o[i * bq:(i + 1) * bq] for i in range(len(bi)))
        finish(results)


def _attention(q, k_all, v_all, k_new, v_new, bias_sub, bias_diag, *, diff, causal_blocks,
               lamp=None, g2=None, bias_max=None, lam_init=0.0, all_batch0=0, new_batch0=0):
    b, tq, _ = q.shape
    t_all = k_all.shape[2 if diff else 1]
    bk = ATT_BLOCK
    bq = min(CAUSAL_Q_BLOCK if causal_blocks else ATT_BLOCK, tq)
    assert tq % bq == 0 and t_all % bk == 0
    if causal_blocks:
        assert bq % bk == 0 and t_all == tq
    else:
        assert tq == bq
    qw = LANES if diff else 2 * LANES
    n_pairs = A_HEADS // 2
    in_specs, args = [], []
    if diff:
        in_specs += [pl.BlockSpec(lamp.shape, lambda i, p, j: (0, 0)),
                     pl.BlockSpec(g2.shape, lambda i, p, j: (0, 0))]
        args += [lamp, g2]
    in_specs.append(pl.BlockSpec((1, bq, qw), lambda i, p, j: (i, j, p)))
    if diff:
        in_specs += [pl.BlockSpec((1, LANES, t_all), lambda i, p, j: (i + all_batch0, p, 0)),
                     pl.BlockSpec((1, LANES, t_all), lambda i, p, j: (i + all_batch0, p, 0)),
                     pl.BlockSpec((1, LANES, bq), lambda i, p, j: (i + new_batch0, p, j)),
                     pl.BlockSpec((1, LANES, bq), lambda i, p, j: (i + new_batch0, p, j))]
    else:
        in_specs += [pl.BlockSpec((1, t_all, qw), lambda i, p, j: (i, 0, p)),
                     pl.BlockSpec((1, t_all, LANES), lambda i, p, j: (i, 0, p)),
                     pl.BlockSpec((1, bq, qw), lambda i, p, j: (i, j, p)),
                     pl.BlockSpec((1, bq, LANES), lambda i, p, j: (i, j, p))]
    args += [q, k_all, v_all, k_new, v_new]
    for bias in (bias_sub, bias_diag):
        if bias is not None:
            if bias.shape[0] == 1:
                in_specs.append(pl.BlockSpec(bias.shape, lambda i, p, j: (0, 0, 0)))
            else:
                in_specs.append(pl.BlockSpec((2,) + bias.shape[1:], lambda i, p, j: (p, 0, 0)))
            args.append(bias)
    if diff:
        in_specs.append(pl.BlockSpec((2, 8, LANES), lambda i, p, j: (p, 0, 0)))
        args.append(bias_max)
    n_var = 4 if diff else 2
    kern = functools.partial(
        _attn_kernel, diff=diff, sub_bias=bias_sub is not None, diag_bias=bias_diag is not None,
        bq=bq, bk=bk, n_all=None if causal_blocks else t_all // bk, lam_init=lam_init)
    return pl.pallas_call(
        kern,
        grid=(b, n_pairs, tq // bq),
        in_specs=in_specs,
        out_specs=pl.BlockSpec((1, bq, LANES), lambda i, p, j: (i, j, p)),
        out_shape=jax.ShapeDtypeStruct((b, tq, n_pairs * LANES), BF16),
        scratch_shapes=[pltpu.VMEM((2 * LANES, t_all) if diff else (t_all, 2 * LANES), BF16),
                        pltpu.VMEM((n_var * 8, LANES), F32),
                        pltpu.VMEM((n_var * bq, 2 * LANES), F32)]
        + ([pltpu.VMEM((LANES, t_all), BF16)] if diff else []),
        compiler_params=_cparams(("arbitrary", "arbitrary", "arbitrary")),
        name="diff_attn" if diff else "mla_attn",
    )(*args)


def _postffn_kernel(a_ref, b_ref, c_ref, x_ref, mod_ref, wo_ref, gffn_ref, wg_ref, wv_ref, cw_ref,
                    cb_ref, wd_ref, hist_ref, gfin_ref, y_ref, conv_ref, carry_ref, act_ref, *, tm, final):
    t = pl.program_id(1)
    ffn = wg_ref.shape[1]
    cat = jnp.concatenate([a_ref[0], b_ref[0], c_ref[0]], axis=-1)
    mix = jnp.dot(cat, wo_ref[...], preferred_element_type=F32)
    gt1 = mod_ref[0, 2:3, :]
    sh2 = mod_ref[0, 3:4, :]
    sc2 = mod_ref[0, 4:5, :]
    gt2 = mod_ref[0, 5:6, :]
    x1 = x_ref[0] + gt1 * mix
    h2 = ((_rms(x1) * gffn_ref[...]) * (1.0 + sc2) + sh2).astype(BF16)

    @pl.when(t == 0)
    def _():
        carry_ref[...] = hist_ref[0]

    for c0 in range(0, ffn, FFN_CHUNK):
        sl = slice(c0, min(c0 + FFN_CHUNK, ffn))
        g = jnp.dot(h2, wg_ref[:, sl], preferred_element_type=F32)
        val = jnp.dot(h2, wv_ref[:, sl], preferred_element_type=F32)
        ext = jnp.concatenate([carry_ref[:, sl], g], axis=0)
        tail = g[tm - 8:, :]
        carry_ref[:, sl] = tail
        conv_ref[0, :, sl] = tail
        gc = (cw_ref[0:1, sl] * pltpu.roll(ext, 2, 0)[8:] + cw_ref[1:2, sl] * pltpu.roll(ext, 1, 0)[8:]
              + cw_ref[2:3, sl] * g + cb_ref[:, sl])
        act_ref[:, sl] = (gc * jax.nn.sigmoid(gc) * val).astype(BF16)
    f = jnp.dot(act_ref[...], wd_ref[...], preferred_element_type=F32)
    x2 = x1 + gt2 * f
    if final:
        x2 = _rms(x2) * gfin_ref[...]
    y_ref[0] = x2


def _postffn(a, bo, c, x, mod, wo, gffn, wg, wv, cw, cb, wd, hist8, gfin, final):
    b, t, d = x.shape
    f = wg.shape[1]
    tm = min(512, t)
    assert t % tm == 0 and tm >= 8
    tile = lambda n: pl.BlockSpec((1, tm, n), lambda i, j: (i, j, 0))

    def resident(shape):
        return pl.BlockSpec(shape, lambda i, j: (0,) * len(shape), pipeline_mode=pl.Buffered(1))

    return pl.pallas_call(
        functools.partial(_postffn_kernel, tm=tm, final=final),
        grid=(b, t // tm),
        in_specs=[tile(A_WIDTH), tile(B_WIDTH), tile(C_WIDTH), tile(d),
                  pl.BlockSpec((1, 6, d), lambda i, j: (i, 0, 0)),
                  resident(wo.shape), resident((1, d)), resident(wg.shape), resident(wv.shape),
                  resident((CONV_W, f)), resident((1, f)), resident(wd.shape),
                  pl.BlockSpec((1, 8, f), lambda i, j: (i, 0, 0)),
                  resident((1, d))],
        out_specs=[tile(d), pl.BlockSpec((1, 8, f), lambda i, j: (i, 0, 0))],
        out_shape=[jax.ShapeDtypeStruct((b, t, d), F32), jax.ShapeDtypeStruct((b, 8, f), F32)],
        scratch_shapes=[pltpu.VMEM((8, f), F32), pltpu.VMEM((tm, f), BF16)],
        compiler_params=_cparams(("arbitrary", "arbitrary")),
        name="postmix_ffn",
    )(a, bo, c, x, mod, wo, gffn, wg, wv, cw, cb, wd, hist8, gfin)


def _rot_cols(w):
    half = C_ROPE // 2
    return jnp.concatenate([-w[..., half:], w[..., :half]], axis=-1)


def _layer_weights(l, w_in, w_q_up, w_kv_up, pool_w, w_out, w_up):
    d = w_in.shape[1]
    wkr = w_in[l][:, OFF_CKR:]
    zpad = jnp.zeros((d, LANES - C_ROPE), F32)
    w1 = jnp.concatenate([w_in[l][:, :A_WIDTH], w_in[l][:, OFF_B:OFF_CKR], wkr, zpad, _rot_cols(wkr), zpad],
                         axis=1).astype(BF16)
    wkvt = w_in[l][:, A_WIDTH:OFF_B].T.astype(BF16)

    wq = w_q_up[l].reshape(C_Q_RANK, C_HEADS, C_NOPE + C_ROPE)
    nope, rope = wq[..., :C_NOPE], wq[..., C_NOPE:]
    zq = jnp.zeros((C_Q_RANK, C_HEADS, LANES - C_NOPE - C_ROPE), F32)
    main = jnp.concatenate([rope, nope, zq], axis=-1).reshape(C_Q_RANK, C_HEADS * LANES)
    rot = jnp.concatenate([_rot_cols(rope), jnp.zeros((C_Q_RANK, C_HEADS, LANES - C_ROPE), F32)],
                          axis=-1).reshape(C_Q_RANK, C_HEADS * LANES)
    wq2 = jnp.concatenate([main, rot], axis=1).astype(BF16)

    wkv = w_kv_up[l].reshape(C_KV_RANK, C_HEADS, C_NOPE + C_V)
    zk = jnp.zeros((C_KV_RANK, C_HEADS, C_ROPE), F32)
    kpart = jnp.concatenate([zk, wkv[..., :C_NOPE], zk], axis=-1).reshape(C_KV_RANK, C_HEADS * LANES)
    vpart = wkv[..., C_NOPE:].reshape(C_KV_RANK, C_WIDTH)
    wkv2 = jnp.concatenate([kpart, vpart], axis=1).astype(BF16)

    wp = jnp.zeros((B_WIDTH, B_WIDTH), F32)
    for g in range(B_GROUPS):
        wp = wp.at[g * B_CH:(g + 1) * B_CH, g * B_CH:(g + 1) * B_CH].set(pool_w[l, g])
    ffn = w_up.shape[2] // 2
    return dict(w1=w1, wkvt=wkvt, wq=wq2, wkv=wkv2, wp=wp.astype(BF16), wo=w_out[l].astype(BF16),
                wg=w_up[l][:, :ffn].astype(BF16), wv=w_up[l][:, ffn:].astype(BF16))


def _rope_tables(pos):
    half = C_ROPE // 2
    inv = 1.0 / (ROPE_BASE ** (jnp.arange(half, dtype=jnp.float32) / half))
    ang = pos.astype(jnp.float32)[:, None] * inv[None, :]
    cos, sin = jnp.cos(ang), jnp.sin(ang)
    n = pos.shape[0]
    z = jnp.zeros((n, LANES - C_ROPE), F32)
    cs = jnp.concatenate([cos, cos, z], axis=1)
    sn = jnp.concatenate([sin, sin, z], axis=1)
    csq = jnp.concatenate([cos, cos, jnp.ones((n, C_NOPE), F32), jnp.zeros((n, LANES - C_ROPE - C_NOPE), F32)], axis=1)
    return cs, sn, csq


def kernel(x_prompt, x_sample, c_prompt, c_sample, cache_a_k, cache_a_v, cache_c_latent, cache_c_krope,
           state_b_pool, state_ffn_conv, w_ada, b_ada, g_mix, w_in, lam_q1, lam_k1, lam_q2, lam_k2,
           a_subln_g, rel_bias, pool_w, pool_scale, c_q_norm_g, w_q_up, c_kv_norm_g, w_kv_up, w_out,
           g_ffn, w_up, conv_w, conv_b, w_down, g_final):
    depth = w_in.shape[0]
    bp, tp, d = x_prompt.shape
    bs, ts, _ = x_sample.shape
    past = cache_a_k.shape[2]
    ffn = conv_w.shape[2]
    blk = ATT_BLOCK

    mod_all = _ada(jnp.concatenate([c_prompt, c_sample], axis=0), w_ada, b_ada)
    mod_all = mod_all.reshape(depth, bp + bs, 6, d)

    r = np.arange(blk)
    rq = np.arange(min(CAUSAL_Q_BLOCK, tp))
    bkt, madd = _tile_bucket_mask(blk + rq, np.arange(blk + len(rq)))
    pb_near, bias_max = _bias_tiles(rel_bias, bkt, madd)
    mla_diag = _tile_bucket_mask(rq, rq)[1][None]
    rs = np.arange(ts)
    bkt, madd = _tile_bucket_mask(past + rs, past - blk + r)
    sb_sub, _ = _bias_tiles(rel_bias, bkt, madd)
    bkt, madd = _tile_bucket_mask(past + rs, past + rs)
    sb_diag, _ = _bias_tiles(rel_bias, bkt, madd)
    mla_sdiag = madd[None]

    tabs_p = _rope_tables(jnp.arange(tp, dtype=jnp.int32))
    tabs_s = _rope_tables(past + jnp.arange(ts, dtype=jnp.int32))
    cache_kt = jnp.transpose(cache_a_k, (0, 1, 3, 4, 2)).reshape(depth * bs, A_WIDTH, past)
    cache_vt = jnp.transpose(cache_a_v, (0, 1, 3, 4, 2)).reshape(depth * bs, A_WIDTH, past)
    zero_pool = jnp.zeros((bp, 16, B_WIDTH), F32)
    zero_conv = jnp.zeros((bp, 8, ffn), F32)

    hp, hs = x_prompt, x_sample
    new_p, new_s = [], []
    stack_p = stack_s = None
    for l in range(depth):
        w = _layer_weights(l, w_in, w_q_up, w_kv_up, pool_w, w_out, w_up)
        lam_init = 0.8 - 0.6 * math.exp(-0.3 * l)
        lamp = jnp.stack([lam_q1[l], lam_k1[l], lam_q2[l], lam_k2[l]], axis=0)
        g2 = jnp.concatenate([a_subln_g[l], a_subln_g[l]])[None]
        wd = w_down[l].astype(BF16)
        last = l == depth - 1

        def run(x, mod, tabs, pos0, hist16, hist8, pasts, prev):
            (aq, akt, avt, ub, bo, lat, kr, cq, ck, cv) = _premix(
                x, mod, g_mix[l][None], w["w1"], w["wkvt"], *tabs, c_q_norm_g[l][None], w["wq"],
                c_kv_norm_g[l][None], w["wkv"], w["wp"], pool_scale[l][None], hist16, pos0, prev)
            bsz = x.shape[0]
            kt_new = akt.reshape((-1,) + akt.shape[2:])
            vt_new = avt.reshape((-1,) + avt.shape[2:])
            new0 = kt_new.shape[0] - bsz
            if pasts is None:
                a_out = _attention(aq, kt_new, vt_new, kt_new, vt_new, pb_near, None, diff=True,
                                   causal_blocks=True, lamp=lamp, g2=g2, bias_max=bias_max, lam_init=lam_init,
                                   all_batch0=new0, new_batch0=new0)
                c_out = _attention(cq, ck, cv, ck, cv, None, mla_diag, diff=False, causal_blocks=True)
            else:
                plat, pkr = pasts
                a_out = _attention(aq, cache_kt, cache_vt, kt_new, vt_new, sb_sub, sb_diag, diff=True,
                                   causal_blocks=False, lamp=lamp, g2=g2, bias_max=bias_max, lam_init=lam_init,
                                   all_batch0=l * bs, new_batch0=new0)
                ckp, cvp = _kvpast(plat, pkr, w["wkv"])
                c_out = _attention(cq, ckp, cvp, ck, cv, None, mla_sdiag, diff=False, causal_blocks=False)
            y, conv8 = _postffn(a_out, bo, c_out, x, mod, w["wo"], g_ffn[l][None], w["wg"], w["wv"],
                                conv_w[l], conv_b[l][None], wd, hist8, g_final[None], last)
            tt = x.shape[1]
            return y, (akt, avt, lat), (kr, ub[:, tt - POOL_HIST:], conv8[:, 8 - (CONV_W - 1):])

        hp, stack_p, st = run(hp, mod_all[l, :bp], tabs_p, 0, zero_pool, zero_conv, None, stack_p)
        new_p.append(st)
        hist16 = jnp.pad(state_b_pool[l], ((0, 0), (1, 0), (0, 0)))
        hist8 = jnp.pad(state_ffn_conv[l], ((0, 0), (8 - (CONV_W - 1), 0), (0, 0)))
        hs, stack_s, st = run(hs, mod_all[l, bp:], tabs_s, past, hist16, hist8,
                              (cache_c_latent[l], cache_c_krope[l]), stack_s)
        new_s.append(st)

    def assemble(stack, small):
        akt, avt, lat = stack
        to_cache = lambda a: jnp.transpose(a.reshape(a.shape[:2] + (A_HEADS, A_V, a.shape[3])), (0, 1, 4, 2, 3))
        kr, pool, conv = [jnp.stack(z, axis=0) for z in zip(*small)]
        return to_cache(akt), to_cache(avt), lat, kr, pool, conv

    return (hp, hs, *assemble(stack_p, new_p), *assemble(stack_s, new_s))
```

```python
import functools
import math

import numpy as np
import jax
import jax.numpy as jnp
from jax import lax
from jax.experimental import pallas as pl
from jax.experimental.pallas import tpu as pltpu

F32 = jnp.float32
BF16 = jnp.bfloat16

CHUNK = 64
A_HEADS = 6
A_QK = 32
A_V = 64
B_GROUPS = 4
B_CH = 64
POOL_WINDOWS = (2, 4, 8, 16)
POOL_HIST = 15
C_HEADS = 6
C_NOPE = 64
C_ROPE = 32
C_V = 64
C_Q_RANK = 256
C_KV_RANK = 128
ROPE_BASE = 10000.0
REL_BUCKETS = 32
REL_MAX_DIST = 128
CONV_W = 3
EPS = 1e-6

A_WIDTH = A_HEADS * A_V
B_WIDTH = B_GROUPS * B_CH
C_WIDTH = C_HEADS * C_V
OFF_B = 3 * A_WIDTH
OFF_CQ = OFF_B + B_WIDTH
OFF_CKV = OFF_CQ + C_Q_RANK
OFF_CKR = OFF_CKV + C_KV_RANK

LANES = 128
ATT_BLOCK = 256
CAUSAL_Q_BLOCK = 512
FFN_CHUNK = 1024
SCORE_TILE_ELEMS = 2048 * 1024
LOG2E = math.log2(math.e)
NEG_BIG = -1e30
VMEM_LIMIT = 58 * 1024 * 1024
ADA_COLS = 1024
KEY_NORM_ROWS = 1024
REF_SLACK = 1.0 + 2.0 ** -6
DEN_MIN = 2.0 ** -80
DEN_MAX = 2.0 ** 100

U_AQ = 0
U_B = U_AQ + A_WIDTH
U_CQ = U_B + B_WIDTH
U_CKV = U_CQ + C_Q_RANK
U_KR = U_CKV + C_KV_RANK
N1 = U_KR + 2 * LANES


def _cparams(sem):
    return pltpu.CompilerParams(dimension_semantics=sem, vmem_limit_bytes=VMEM_LIMIT)


def _rms(x):
    return x * lax.rsqrt(jnp.mean(x * x, axis=-1, keepdims=True) + EPS)


def _ada_kernel(c_ref, w_ref, b_ref, o_ref):
    c = c_ref[...]
    s = (c * jax.nn.sigmoid(c)).astype(BF16)
    o_ref[0] = jnp.dot(s, w_ref[0].astype(BF16), preferred_element_type=F32) + b_ref[0]


def _ada(c_all, w_ada, b_ada):
    depth, d, n = w_ada.shape
    nb = c_all.shape[0]
    tn = ADA_COLS
    assert n % tn == 0
    return pl.pallas_call(
        _ada_kernel,
        grid=(depth, n // tn),
        in_specs=[pl.BlockSpec((nb, d), lambda l, j: (0, 0)),
                  pl.BlockSpec((1, d, tn), lambda l, j: (l, 0, j)),
                  pl.BlockSpec((1, 1, tn), lambda l, j: (l, 0, j))],
        out_specs=pl.BlockSpec((1, nb, tn), lambda l, j: (l, 0, j)),
        out_shape=jax.ShapeDtypeStruct((depth, nb, n), F32),
        compiler_params=_cparams(("arbitrary", "arbitrary")),
        name="ada_mod",
    )(c_all, w_ada, b_ada.reshape(depth, 1, n))


def _bias_kernel(rb_ref, bucket_ref, madd_ref, o_ref, mx_ref):
    h = pl.program_id(0)
    bucket = bucket_ref[...]
    far = rb_ref[REL_BUCKETS // 2 - 1, h]
    val = jnp.zeros(bucket.shape, F32)
    top = far - far
    for b in range(REL_BUCKETS):
        val = jnp.where(bucket == b, rb_ref[b, h] - far, val)
        top = jnp.maximum(top, rb_ref[b, h] - far)
    o_ref[0] = val * LOG2E + madd_ref[...]
    mx_ref[0] = jnp.full(mx_ref.shape[1:], top * LOG2E, F32)


def _bias_tiles(rel_bias, bucket, madd):
    r, c = bucket.shape
    return pl.pallas_call(
        _bias_kernel,
        grid=(A_HEADS,),
        in_specs=[pl.BlockSpec(memory_space=pltpu.SMEM),
                  pl.BlockSpec((r, c), lambda h: (0, 0)),
                  pl.BlockSpec((r, c), lambda h: (0, 0))],
        out_specs=[pl.BlockSpec((1, r, c), lambda h: (h, 0, 0)),
                   pl.BlockSpec((1, 8, LANES), lambda h: (h, 0, 0))],
        out_shape=[jax.ShapeDtypeStruct((A_HEADS, r, c), F32),
                   jax.ShapeDtypeStruct((A_HEADS, 8, LANES), F32)],
        compiler_params=_cparams(("arbitrary",)),
        name="rel_bias_tiles",
    )(rel_bias, bucket, madd)


def _t5_bucket(rel):
    half = REL_BUCKETS // 2
    exact = half // 2
    ret = jnp.where(rel > 0, half, 0)
    n = jnp.abs(rel)
    nf = jnp.maximum(n, 1).astype(jnp.float32)
    large = exact + (jnp.log(nf / exact) / math.log(REL_MAX_DIST / exact) * (half - exact)).astype(jnp.int32)
    large = jnp.minimum(large, half - 1)
    return ret + jnp.where(n < exact, n, large)


def _tile_bucket_mask(q_pos, k_pos):
    q_pos = jnp.asarray(q_pos, jnp.int32)
    k_pos = jnp.asarray(k_pos, jnp.int32)
    bucket = _t5_bucket(k_pos[None, :] - q_pos[:, None]).astype(jnp.int32)
    visible = (k_pos[None, :] // CHUNK) <= (q_pos[:, None] // CHUNK)
    return bucket, jnp.where(visible, 0.0, NEG_BIG).astype(F32)


def _premix_kernel(x_ref, mod_ref, gmix_ref, w1_ref, wkvt_ref, cs_ref, sn_ref, csq_ref, gq_ref, wq_ref,
                   gkv_ref, wkv_ref, wp_ref, ps_ref, hist_ref, *rest, tm, pos0, n_prev):
    prev = rest[:3] if n_prev else None
    (aq_ref, akt_ref, avt_ref, ub_ref, bo_ref, lat_ref, kr_ref,
     cq_ref, ck_ref, cv_ref, carry_ref) = rest[3:] if n_prev else rest
    t = pl.program_id(1)
    x = x_ref[0]
    sh1 = mod_ref[0, 0:1, :]
    sc1 = mod_ref[0, 1:2, :]
    h = ((_rms(x) * gmix_ref[...]) * (1.0 + sc1) + sh1).astype(BF16)
    u = jnp.dot(h, w1_ref[...], preferred_element_type=F32)
    kvt = lax.dot_general(wkvt_ref[...], h, (((1,), (1,)), ((), ())), preferred_element_type=F32)
    akt = kvt[0:A_WIDTH]
    avt = kvt[A_WIDTH:]
    if n_prev:
        akt_ref[0:n_prev, 0] = prev[0][:, 0]
        avt_ref[0:n_prev, 0] = prev[1][:, 0]
        lat_ref[0:n_prev, 0] = prev[2][:, 0]
    akt_ref[n_prev, 0] = akt
    avt_ref[n_prev, 0] = avt
    aq_ref[0] = (u[:, U_AQ:U_B] * (A_QK ** -0.5 * LOG2E)).astype(BF16)

    ub = u[:, U_B:U_CQ]
    ub_ref[0] = ub

    @pl.when(t == 0)
    def _():
        carry_ref[...] = hist_ref[0]

    ext = jnp.concatenate([carry_ref[...], ub], axis=0)
    carry_ref[...] = ub[tm - 16:, :]
    s2 = ext + pltpu.roll(ext, 1, 0)
    s4 = s2 + pltpu.roll(s2, 2, 0)
    s8 = s4 + pltpu.roll(s4, 4, 0)
    s16 = s8 + pltpu.roll(s8, 8, 0)
    lane = lax.broadcasted_iota(jnp.int32, (tm, B_WIDTH), 1)
    grp = lane // B_CH
    tot = jnp.where(grp == 0, s2[16:], jnp.where(grp == 1, s4[16:], jnp.where(grp == 2, s8[16:], s16[16:])))
    win = jnp.where(grp == 0, 2, jnp.where(grp == 1, 4, jnp.where(grp == 2, 8, 16)))
    pos = pos0 + t * tm + lax.broadcasted_iota(jnp.int32, (tm, B_WIDTH), 0)
    cnt = jnp.minimum(pos + 1, win).astype(F32)
    m = tot / cnt - ub
    y = jnp.dot(m.astype(BF16), wp_ref[...], preferred_element_type=F32) * ps_ref[...]
    bo_ref[0] = y.astype(BF16)

    cs = cs_ref[...]
    sn = sn_ref[...]
    csq = csq_ref[...]
    qn = (_rms(u[:, U_CQ:U_CKV]) * gq_ref[...]).astype(BF16)
    q2 = jnp.dot(qn, wq_ref[...], preferred_element_type=F32)
    qscale = (C_NOPE + C_ROPE) ** -0.5 * LOG2E
    lat = _rms(u[:, U_CKV:U_KR]) * gkv_ref[...]
    lat_ref[n_prev, 0] = lat
    krp = u[:, U_KR:U_KR + LANES] * cs + u[:, U_KR + LANES:N1] * sn
    kr_ref[0] = krp[:, 0:C_ROPE]
    kv = jnp.dot(lat.astype(BF16), wkv_ref[...], preferred_element_type=F32)
    for hh in range(C_HEADS):
        sl = slice(hh * LANES, (hh + 1) * LANES)
        sr = slice((C_HEADS + hh) * LANES, (C_HEADS + hh + 1) * LANES)
        cq_ref[0, :, sl] = ((q2[:, sl] * csq + q2[:, sr] * sn) * qscale).astype(BF16)
        ck_ref[0, :, sl] = (kv[:, sl] + krp).astype(BF16)
    cv_ref[0] = kv[:, C_HEADS * LANES:].astype(BF16)


def _premix(x, mod, gmix, w1, wkvt, cs, sn, csq, gq, wq, gkv, wkv, wp, ps, hist16, pos0, prev=None):
    b, t, d = x.shape
    n_prev = 0 if prev is None else prev[0].shape[0]
    nl = n_prev + 1
    tm = min(1024, t)
    assert t % tm == 0 and tm >= 16
    full = lambda shape: pl.BlockSpec(shape, lambda i, j: (0,) * len(shape))
    tile = lambda n: pl.BlockSpec((1, tm, n), lambda i, j: (i, j, 0))
    layers_t = lambda n: pl.BlockSpec((n, 1, A_WIDTH, tm), lambda i, j: (0, i, 0, j))
    layers = lambda n, w: pl.BlockSpec((n, 1, tm, w), lambda i, j: (0, i, j, 0))
    tab = pl.BlockSpec((tm, LANES), lambda i, j: (j, 0))
    outs = [(A_WIDTH, BF16), (None, F32), (None, F32),
            (B_WIDTH, F32), (B_WIDTH, BF16), (C_KV_RANK, F32), (C_ROPE, F32),
            (C_HEADS * LANES, BF16), (C_HEADS * LANES, BF16), (C_WIDTH, BF16)]
    out_specs = [None if n is None else tile(n) for n, _ in outs]
    out_shape = [None if n is None else jax.ShapeDtypeStruct((b, t, n), dt) for n, dt in outs]
    for idx in (1, 2):
        out_specs[idx] = layers_t(nl)
        out_shape[idx] = jax.ShapeDtypeStruct((nl, b, A_WIDTH, t), F32)
    out_specs[5] = layers(nl, C_KV_RANK)
    out_shape[5] = jax.ShapeDtypeStruct((nl, b, t, C_KV_RANK), F32)
    prev_specs = [layers_t(n_prev), layers_t(n_prev), layers(n_prev, C_KV_RANK)] if n_prev else []
    return pl.pallas_call(
        functools.partial(_premix_kernel, tm=tm, pos0=pos0, n_prev=n_prev),
        grid=(b, t // tm),
        in_specs=[tile(d),
                  pl.BlockSpec((1, 6, d), lambda i, j: (i, 0, 0)),
                  full((1, d)), full(w1.shape), full(wkvt.shape), tab, tab, tab,
                  full((1, C_Q_RANK)), full(wq.shape), full((1, C_KV_RANK)), full(wkv.shape),
                  full(wp.shape), full((1, B_WIDTH)),
                  pl.BlockSpec((1, 16, B_WIDTH), lambda i, j: (i, 0, 0))] + prev_specs,
        out_specs=out_specs,
        out_shape=out_shape,
        scratch_shapes=[pltpu.VMEM((16, B_WIDTH), F32)],
        compiler_params=_cparams(("arbitrary", "arbitrary")),
        name="premix",
    )(x, mod, gmix, w1, wkvt, cs, sn, csq, gq, wq, gkv, wkv, wp, ps, hist16, *(prev or ()))


def _kvpast_kernel(lat_ref, kr_ref, wkv_ref, e_ref, ck_ref, cv_ref):
    kv = jnp.dot(lat_ref[0].astype(BF16), wkv_ref[...], preferred_element_type=F32)
    krp = jnp.dot(kr_ref[0].astype(BF16), e_ref[...], preferred_element_type=F32)
    for hh in range(C_HEADS):
        sl = slice(hh * LANES, (hh + 1) * LANES)
        ck_ref[0, :, sl] = (kv[:, sl] + krp).astype(BF16)
    cv_ref[0] = kv[:, C_HEADS * LANES:].astype(BF16)


def _kvpast(lat, kr, wkv):
    b, p, _ = lat.shape
    tm = min(2048, p)
    assert p % tm == 0
    e = jnp.eye(C_ROPE, LANES, dtype=BF16)
    return pl.pallas_call(
        _kvpast_kernel,
        grid=(b, p // tm),
        in_specs=[pl.BlockSpec((1, tm, C_KV_RANK), lambda i, j: (i, j, 0)),
                  pl.BlockSpec((1, tm, C_ROPE), lambda i, j: (i, j, 0)),
                  pl.BlockSpec(wkv.shape, lambda i, j: (0, 0)),
                  pl.BlockSpec(e.shape, lambda i, j: (0, 0))],
        out_specs=[pl.BlockSpec((1, tm, C_HEADS * LANES), lambda i, j: (i, j, 0)),
                   pl.BlockSpec((1, tm, C_WIDTH), lambda i, j: (i, j, 0))],
        out_shape=[jax.ShapeDtypeStruct((b, p, C_HEADS * LANES), BF16),
                   jax.ShapeDtypeStruct((b, p, C_WIDTH), BF16)],
        compiler_params=_cparams(("arbitrary", "arbitrary")),
        name="mla_kv_past",
    )(lat, kr, wkv, e)


def _online_update(carry, s, v, pv):
    m, l, acc = carry
    m_new = jnp.maximum(m, jnp.max(s, axis=-1, keepdims=True))
    alpha = jnp.exp2(m - m_new)
    p = jnp.exp2(s - m_new)
    l = alpha * l + jnp.sum(p, axis=-1, keepdims=True)
    acc = alpha * acc + pv(p.astype(BF16), v)
    return m_new, l, acc


def _dot_nt(a, b):
    return lax.dot_general(a, b, (((1,), (1,)), ((), ())), preferred_element_type=F32)


def _attn_kernel(*refs, diff, sub_bias, diag_bias, bq, bk, n_all, lam_init):
    it = iter(refs)
    lamp_ref = next(it) if diff else None
    g_ref = next(it) if diff else None
    q_ref, ka_ref, va_ref, kd_ref, vd_ref = (next(it) for _ in range(5))
    bs_ref = next(it) if sub_bias else None
    bd_ref = next(it) if diag_bias else None
    bmax_ref = next(it) if diff else None
    o_ref = next(it)
    vaug_ref, kmax_ref, acc_ref = (next(it) for _ in range(3))
    kbf_ref = next(it) if diff else None
    causal = n_all is None
    merged = causal and sub_bias

    qi = pl.program_id(2)
    kt = diff
    key_axis = 2 if kt else 1
    t_all = ka_ref.shape[key_axis]
    bkd = kd_ref.shape[key_axis]

    def k_tile(off, width):
        return kbf_ref[:, pl.ds(off, width)] if kt else ka_ref[0, pl.ds(off, width), :]

    def v_tile(off, width):
        return (va_ref[0, :, pl.ds(off, width)] if kt else va_ref[0, pl.ds(off, width), :]).astype(BF16)

    def vaug_tile(off, width):
        return vaug_ref[:, pl.ds(off, width)] if kt else vaug_ref[pl.ds(off, width), :]

    def qk(qg, k, kcol):
        if kt:
            return jnp.dot(qg, k.astype(BF16), preferred_element_type=F32)
        return _dot_nt(qg, k[:, kcol * LANES:(kcol + 1) * LANES].astype(BF16))

    def pv(p, v):
        return _dot_nt(p, v.astype(BF16)) if kt else jnp.dot(p, v.astype(BF16), preferred_element_type=F32)
    na = qi * (bq // bk) if causal else n_all
    lane = lax.broadcasted_iota(jnp.int32, (bq, LANES), 1)
    q = q_ref[0]
    zero = jnp.zeros_like(q)
    ones_sq = jnp.ones((LANES, LANES), BF16)

    def vmask(v, n):
        ln = lax.broadcasted_iota(jnp.int32, (n, LANES), 1)
        return (ln >= A_QK * v) & (ln < A_QK * (v + 1))

    if diff:
        qs = jnp.concatenate([jnp.where(vmask(v, bq), q, zero) for v in range(4)], axis=0)
        groups = [(qs, 0, [0, 0, 1, 1])]
    else:
        groups = [(q[:, 0:LANES], 0, [0]), (q[:, LANES:2 * LANES], 1, [0])]
    n_var = sum(len(bi) for _, _, bi in groups)

    def key_norm2(k, var):
        kk = k.astype(BF16).astype(F32)
        kk = kk * kk
        if kt:
            row = lax.broadcasted_iota(jnp.int32, kk.shape, 0)
            kk = jnp.where((row >= A_QK * var) & (row < A_QK * (var + 1)), kk, 0.0)
            n2 = jnp.sum(kk, axis=0, keepdims=True)
            return jnp.broadcast_to(jnp.max(n2, axis=1, keepdims=True), (1, LANES))
        n2 = jnp.dot(kk.astype(BF16), ones_sq, preferred_element_type=F32)
        return jnp.max(n2, axis=0, keepdims=True)

    def var_keys(k, var):
        return k if diff else k[:, var * LANES:(var + 1) * LANES]

    @pl.when(qi == 0)
    def _():
        if kt:
            kbf_ref[...] = ka_ref[0].astype(BF16)
            vaug_ref[0:LANES, :] = va_ref[0].astype(BF16)
            vaug_ref[LANES:, :] = jnp.ones((LANES, t_all), BF16)
        else:
            vaug_ref[:, 0:LANES] = va_ref[0].astype(BF16)
            vaug_ref[:, LANES:] = jnp.ones((t_all, LANES), BF16)
        rows = min(t_all, KEY_NORM_ROWS)
        for var in range(n_var):
            km = jnp.zeros((1, LANES), F32)
            for c in range(t_all // rows):
                km = jnp.maximum(km, key_norm2(var_keys(k_tile(c * rows, rows), var), var))
            kmax_ref[var * 8:(var + 1) * 8, :] = jnp.broadcast_to(km, (8, LANES))

    def stack_bias(ref, bias_idx):
        tiles = [ref[0] if ref.shape[0] == 1 else ref[i] for i in bias_idx]
        return tiles[0] if len(tiles) == 1 else jnp.concatenate(tiles, axis=0)

    kd = kd_ref[0].astype(BF16)
    vd = vd_ref[0].astype(BF16)
    has_sub = sub_bias
    if has_sub:
        n_far = jnp.maximum(na - 1, 0) if causal else max(na - 1, 0)
        off_s = pl.multiple_of(n_far * bk, bk) if causal else n_far * bk
    else:
        n_far = na

    def diag_biases():
        if merged:
            return [stack_bias(bs_ref, bi)[:, bk:] for _, _, bi in groups]
        return [stack_bias(bd_ref, bi) if diag_bias else None for _, _, bi in groups]

    def sub_biases():
        gate = jnp.where(na >= 1, 0.0, NEG_BIG)
        if merged:
            return [stack_bias(bs_ref, bi)[:, :bk] + gate for _, _, bi in groups]
        return [stack_bias(bs_ref, bi) + gate for _, _, bi in groups]

    refs_g = []
    var = 0
    for qg, kcol, bi in groups:
        qf = qg.astype(F32)
        qn2 = jnp.sum(qf * qf, axis=-1, keepdims=True)
        parts = []
        for i, b_i in enumerate(bi):
            km2 = kmax_ref[var * 8:var * 8 + 1, :]
            if n_all is not None:
                km2 = jnp.maximum(km2, key_norm2(var_keys(kd, var), var))
            r = jnp.sqrt(qn2[i * bq:(i + 1) * bq] * km2) * REF_SLACK
            if diff:
                r = r + bmax_ref[b_i, 0:1, :]
            parts.append(r)
            var += 1
        refs_g.append(parts[0] if len(parts) == 1 else jnp.concatenate(parts, axis=0))

    def probs(k, biases):
        out = []
        for (qg, kcol, _), mref, bias in zip(groups, refs_g, biases):
            s = qk(qg, k, kcol)
            if bias is not None:
                s = s + bias
            n = s.shape[1]
            if n >= LANES:
                cols = [jnp.exp2(s[:, c:c + LANES] - mref) for c in range(0, n, LANES)]
            else:
                cols = [jnp.exp2(s - mref[:, :n])]
            out.append((cols[0] if len(cols) == 1 else jnp.concatenate(cols, axis=1)).astype(BF16))
        return out[0] if len(out) == 1 else jnp.concatenate(out, axis=0)

    none_bias = [None] * len(groups)
    if kt:
        vd_aug = jnp.concatenate([vd, jnp.ones((LANES, bkd), BF16)], axis=0)
    else:
        vd_aug = jnp.concatenate([vd, jnp.ones((bkd, LANES), BF16)], axis=1)
    if merged:
        k_near = jnp.concatenate([k_tile(off_s, bk), kd], axis=key_axis - 1)
        v_near = jnp.concatenate([vaug_tile(off_s, bk), vd_aug], axis=key_axis - 1)
        bias_near = [jnp.concatenate([b_s, b_d], axis=1) for b_s, b_d in zip(sub_biases(), diag_biases())]
        acc_ref[...] = pv(probs(k_near, bias_near), v_near)
    else:
        acc_ref[...] = pv(probs(kd, diag_biases()), vd_aug)
        if has_sub:
            acc_ref[...] += pv(probs(k_tile(off_s, bk), sub_biases()), vaug_tile(off_s, bk))
    step_tiles = max(1, SCORE_TILE_ELEMS // (n_var * bq * bk))
    if causal:
        def far_step(off, width):
            off = pl.multiple_of(off, bk)
            acc_ref[...] += pv(probs(k_tile(off, width), none_bias), vaug_tile(off, width))

        def far_loop(j, c):
            far_step(j * (step_tiles * bk), step_tiles * bk)
            return c

        n_full = n_far // step_tiles
        lax.fori_loop(0, n_full, far_loop, 0)
        rem = n_far - step_tiles * n_full
        off = n_full * (step_tiles * bk)
        tiles = step_tiles // 2
        while tiles >= 1:
            pl.when((rem & tiles) != 0)(functools.partial(far_step, off, tiles * bk))
            off = off + (rem & tiles) * bk
            tiles //= 2
    else:
        for off in range(0, n_far * bk, step_tiles * bk):
            width = min(step_tiles * bk, n_far * bk - off)
            acc_ref[...] += pv(probs(k_tile(off, width), none_bias), vaug_tile(off, width))

    def finish(results):
        if diff:
            lp = lamp_ref[...]
            lam = (jnp.exp(jnp.sum(lp[0:1] * lp[1:2], axis=-1, keepdims=True))
                   - jnp.exp(jnp.sum(lp[2:3] * lp[3:4], axis=-1, keepdims=True)) + lam_init)
            outs = []
            gain = g_ref[...] * (1.0 - lam_init)
            for e in range(2):
                d = results[2 * e] - lam * results[2 * e + 1]
                own = (lane >= A_V * e) & (lane < A_V * (e + 1))
                ms = jnp.sum(jnp.where(own, d * d, 0.0), axis=-1, keepdims=True) * (1.0 / A_V)
                outs.append(d * (lax.rsqrt(ms + EPS) * gain))
            out = jnp.where(lane < A_V, outs[0], outs[1])
        else:
            out = jnp.where(lane < C_V, results[0], results[1])
        o_ref[0] = out.astype(o_ref.dtype)

    acc = acc_ref[...]
    den = acc[:, LANES:]
    safe = (jnp.min(den) >= DEN_MIN) & (jnp.max(den) <= DEN_MAX)
    o_fast = acc[:, 0:LANES] / den
    finish([o_fast[i * bq:(i + 1) * bq] for i in range(n_var)])

    @pl.when(jnp.logical_not(safe))
    def _():
        def step(carries, k, v, biases):
            out = []
            for (qg, kcol, _), c, bias in zip(groups, carries, biases):
                s = qk(qg, k, kcol)
                out.append(_online_update(c, s if bias is None else s + bias, v, pv))
            return tuple(out)

        carries = tuple((jnp.full((len(bi) * bq, 1), NEG_BIG, F32), jnp.zeros((len(bi) * bq, 1), F32),
                         jnp.zeros((len(bi) * bq, LANES), F32)) for _, _, bi in groups)
        carries = step(carries, kd, vd, diag_biases())
        if has_sub:
            carries = step(carries, k_tile(off_s, bk), v_tile(off_s, bk), sub_biases())

        def body(j, cs):
            off = pl.multiple_of(j * bk, bk)
            return step(cs, k_tile(off, bk), v_tile(off, bk), none_bias)

        carries = lax.fori_loop(0, n_far, body, carries)
        results = []
        for (_, _, bi), (m, l, a) in zip(groups, carries):
            o = a / l
            results.extend(o[i * bq:(i + 1) * bq] for i in range(len(bi)))
        finish(results)


def _attention(q, k_all, v_all, k_new, v_new, bias_sub, bias_diag, *, diff, causal_blocks,
               lamp=None, g2=None, bias_max=None, lam_init=0.0, all_batch0=0, new_batch0=0):
    b, tq, _ = q.shape
    t_all = k_all.shape[2 if diff else 1]
    bk = ATT_BLOCK
    bq = min(CAUSAL_Q_BLOCK if causal_blocks else ATT_BLOCK, tq)
    assert tq % bq == 0 and t_all % bk == 0
    if causal_blocks:
        assert bq % bk == 0 and t_all == tq
    else:
        assert tq == bq
    qw = LANES if diff else 2 * LANES
    n_pairs = A_HEADS // 2
    in_specs, args = [], []
    if diff:
        in_specs += [pl.BlockSpec(lamp.shape, lambda i, p, j: (0, 0)),
                     pl.BlockSpec(g2.shape, lambda i, p, j: (0, 0))]
        args += [lamp, g2]
    in_specs.append(pl.BlockSpec((1, bq, qw), lambda i, p, j: (i, j, p)))
    if diff:
        in_specs += [pl.BlockSpec((1, LANES, t_all), lambda i, p, j: (i + all_batch0, p, 0)),
                     pl.BlockSpec((1, LANES, t_all), lambda i, p, j: (i + all_batch0, p, 0)),
                     pl.BlockSpec((1, LANES, bq), lambda i, p, j: (i + new_batch0, p, j)),
                     pl.BlockSpec((1, LANES, bq), lambda i, p, j: (i + new_batch0, p, j))]
    else:
        in_specs += [pl.BlockSpec((1, t_all, qw), lambda i, p, j: (i, 0, p)),
                     pl.BlockSpec((1, t_all, LANES), lambda i, p, j: (i, 0, p)),
                     pl.BlockSpec((1, bq, qw), lambda i, p, j: (i, j, p)),
                     pl.BlockSpec((1, bq, LANES), lambda i, p, j: (i, j, p))]
    args += [q, k_all, v_all, k_new, v_new]
    for bias in (bias_sub, bias_diag):
        if bias is not None:
            if bias.shape[0] == 1:
                in_specs.append(pl.BlockSpec(bias.shape, lambda i, p, j: (0, 0, 0)))
            else:
                in_specs.append(pl.BlockSpec((2,) + bias.shape[1:], lambda i, p, j: (p, 0, 0)))
            args.append(bias)
    if diff:
        in_specs.append(pl.BlockSpec((2, 8, LANES), lambda i, p, j: (p, 0, 0)))
        args.append(bias_max)
    n_var = 4 if diff else 2
    kern = functools.partial(
        _attn_kernel, diff=diff, sub_bias=bias_sub is not None, diag_bias=bias_diag is not None,
        bq=bq, bk=bk, n_all=None if causal_blocks else t_all // bk, lam_init=lam_init)
    return pl.pallas_call(
        kern,
        grid=(b, n_pairs, tq // bq),
        in_specs=in_specs,
        out_specs=pl.BlockSpec((1, bq, LANES), lambda i, p, j: (i, j, p)),
        out_shape=jax.ShapeDtypeStruct((b, tq, n_pairs * LANES), BF16),
        scratch_shapes=[pltpu.VMEM((2 * LANES, t_all) if diff else (t_all, 2 * LANES), BF16),
                        pltpu.VMEM((n_var * 8, LANES), F32),
                        pltpu.VMEM((n_var * bq, 2 * LANES), F32)]
        + ([pltpu.VMEM((LANES, t_all), BF16)] if diff else []),
        compiler_params=_cparams(("arbitrary", "arbitrary", "arbitrary")),
        name="diff_attn" if diff else "mla_attn",
    )(*args)


def _postffn_kernel(a_ref, b_ref, c_ref, x_ref, mod_ref, wo_ref, gffn_ref, wg_ref, wv_ref, cw_ref,
                    cb_ref, wd_ref, hist_ref, gfin_ref, y_ref, conv_ref, carry_ref, act_ref, *, tm, final):
    t = pl.program_id(1)
    ffn = wg_ref.shape[1]
    cat = jnp.concatenate([a_ref[0], b_ref[0], c_ref[0]], axis=-1)
    mix = jnp.dot(cat, wo_ref[...], preferred_element_type=F32)
    gt1 = mod_ref[0, 2:3, :]
    sh2 = mod_ref[0, 3:4, :]
    sc2 = mod_ref[0, 4:5, :]
    gt2 = mod_ref[0, 5:6, :]
    x1 = x_ref[0] + gt1 * mix
    h2 = ((_rms(x1) * gffn_ref[...]) * (1.0 + sc2) + sh2).astype(BF16)

    @pl.when(t == 0)
    def _():
        carry_ref[...] = hist_ref[0]

    for c0 in range(0, ffn, FFN_CHUNK):
        sl = slice(c0, min(c0 + FFN_CHUNK, ffn))
        g = jnp.dot(h2, wg_ref[:, sl], preferred_element_type=F32)
        val = jnp.dot(h2, wv_ref[:, sl], preferred_element_type=F32)
        ext = jnp.concatenate([carry_ref[:, sl], g], axis=0)
        tail = g[tm - 8:, :]
        carry_ref[:, sl] = tail
        conv_ref[0, :, sl] = tail
        gc = (cw_ref[0:1, sl] * pltpu.roll(ext, 2, 0)[8:] + cw_ref[1:2, sl] * pltpu.roll(ext, 1, 0)[8:]
              + cw_ref[2:3, sl] * g + cb_ref[:, sl])
        act_ref[:, sl] = (gc * jax.nn.sigmoid(gc) * val).astype(BF16)
    f = jnp.dot(act_ref[...], wd_ref[...], preferred_element_type=F32)
    x2 = x1 + gt2 * f
    if final:
        x2 = _rms(x2) * gfin_ref[...]
    y_ref[0] = x2


def _postffn(a, bo, c, x, mod, wo, gffn, wg, wv, cw, cb, wd, hist8, gfin, final):
    b, t, d = x.shape
    f = wg.shape[1]
    tm = min(512, t)
    assert t % tm == 0 and tm >= 8
    tile = lambda n: pl.BlockSpec((1, tm, n), lambda i, j: (i, j, 0))

    def resident(shape):
        return pl.BlockSpec(shape, lambda i, j: (0,) * len(shape), pipeline_mode=pl.Buffered(1))

    return pl.pallas_call(
        functools.partial(_postffn_kernel, tm=tm, final=final),
        grid=(b, t // tm),
        in_specs=[tile(A_WIDTH), tile(B_WIDTH), tile(C_WIDTH), tile(d),
                  pl.BlockSpec((1, 6, d), lambda i, j: (i, 0, 0)),
                  resident(wo.shape), resident((1, d)), resident(wg.shape), resident(wv.shape),
                  resident((CONV_W, f)), resident((1, f)), resident(wd.shape),
                  pl.BlockSpec((1, 8, f), lambda i, j: (i, 0, 0)),
                  resident((1, d))],
        out_specs=[tile(d), pl.BlockSpec((1, 8, f), lambda i, j: (i, 0, 0))],
        out_shape=[jax.ShapeDtypeStruct((b, t, d), F32), jax.ShapeDtypeStruct((b, 8, f), F32)],
        scratch_shapes=[pltpu.VMEM((8, f), F32), pltpu.VMEM((tm, f), BF16)],
        compiler_params=_cparams(("arbitrary", "arbitrary")),
        name="postmix_ffn",
    )(a, bo, c, x, mod, wo, gffn, wg, wv, cw, cb, wd, hist8, gfin)


def _rot_cols(w):
    half = C_ROPE // 2
    return jnp.concatenate([-w[..., half:], w[..., :half]], axis=-1)


def _layer_weights(l, w_in, w_q_up, w_kv_up, pool_w, w_out, w_up):
    d = w_in.shape[1]
    wkr = w_in[l][:, OFF_CKR:]
    zpad = jnp.zeros((d, LANES - C_ROPE), F32)
    w1 = jnp.concatenate([w_in[l][:, :A_WIDTH], w_in[l][:, OFF_B:OFF_CKR], wkr, zpad, _rot_cols(wkr), zpad],
                         axis=1).astype(BF16)
    wkvt = w_in[l][:, A_WIDTH:OFF_B].T.astype(BF16)

    wq = w_q_up[l].reshape(C_Q_RANK, C_HEADS, C_NOPE + C_ROPE)
    nope, rope = wq[..., :C_NOPE], wq[..., C_NOPE:]
    zq = jnp.zeros((C_Q_RANK, C_HEADS, LANES - C_NOPE - C_ROPE), F32)
    main = jnp.concatenate([rope, nope, zq], axis=-1).reshape(C_Q_RANK, C_HEADS * LANES)
    rot = jnp.concatenate([_rot_cols(rope), jnp.zeros((C_Q_RANK, C_HEADS, LANES - C_ROPE), F32)],
                          axis=-1).reshape(C_Q_RANK, C_HEADS * LANES)
    wq2 = jnp.concatenate([main, rot], axis=1).astype(BF16)

    wkv = w_kv_up[l].reshape(C_KV_RANK, C_HEADS, C_NOPE + C_V)
    zk = jnp.zeros((C_KV_RANK, C_HEADS, C_ROPE), F32)
    kpart = jnp.concatenate([zk, wkv[..., :C_NOPE], zk], axis=-1).reshape(C_KV_RANK, C_HEADS * LANES)
    vpart = wkv[..., C_NOPE:].reshape(C_KV_RANK, C_WIDTH)
    wkv2 = jnp.concatenate([kpart, vpart], axis=1).astype(BF16)

    wp = jnp.zeros((B_WIDTH, B_WIDTH), F32)
    for g in range(B_GROUPS):
        wp = wp.at[g * B_CH:(g + 1) * B_CH, g * B_CH:(g + 1) * B_CH].set(pool_w[l, g])
    ffn = w_up.shape[2] // 2
    return dict(w1=w1, wkvt=wkvt, wq=wq2, wkv=wkv2, wp=wp.astype(BF16), wo=w_out[l].astype(BF16),
                wg=w_up[l][:, :ffn].astype(BF16), wv=w_up[l][:, ffn:].astype(BF16))


def _rope_tables(pos):
    half = C_ROPE // 2
    inv = 1.0 / (ROPE_BASE ** (jnp.arange(half, dtype=jnp.float32) / half))
    ang = pos.astype(jnp.float32)[:, None] * inv[None, :]
    cos, sin = jnp.cos(ang), jnp.sin(ang)
    n = pos.shape[0]
    z = jnp.zeros((n, LANES - C_ROPE), F32)
    cs = jnp.concatenate([cos, cos, z], axis=1)
    sn = jnp.concatenate([sin, sin, z], axis=1)
    csq = jnp.concatenate([cos, cos, jnp.ones((n, C_NOPE), F32), jnp.zeros((n, LANES - C_ROPE - C_NOPE), F32)], axis=1)
    return cs, sn, csq


def kernel(x_prompt, x_sample, c_prompt, c_sample, cache_a_k, cache_a_v, cache_c_latent, cache_c_krope,
           state_b_pool, state_ffn_conv, w_ada, b_ada, g_mix, w_in, lam_q1, lam_k1, lam_q2, lam_k2,
           a_subln_g, rel_bias, pool_w, pool_scale, c_q_norm_g, w_q_up, c_kv_norm_g, w_kv_up, w_out,
           g_ffn, w_up, conv_w, conv_b, w_down, g_final):
    depth = w_in.shape[0]
    bp, tp, d = x_prompt.shape
    bs, ts, _ = x_sample.shape
    past = cache_a_k.shape[2]
    ffn = conv_w.shape[2]
    blk = ATT_BLOCK

    mod_all = _ada(jnp.concatenate([c_prompt, c_sample], axis=0), w_ada, b_ada)
    mod_all = mod_all.reshape(depth, bp + bs, 6, d)

    r = np.arange(blk)
    rq = np.arange(min(CAUSAL_Q_BLOCK, tp))
    bkt, madd = _tile_bucket_mask(blk + rq, np.arange(blk + len(rq)))
    pb_near, bias_max = _bias_tiles(rel_bias, bkt, madd)
    mla_diag = _tile_bucket_mask(rq, rq)[1][None]
    rs = np.arange(ts)
    bkt, madd = _tile_bucket_mask(past + rs, past - blk + r)
    sb_sub, _ = _bias_tiles(rel_bias, bkt, madd)
    bkt, madd = _tile_bucket_mask(past + rs, past + rs)
    sb_diag, _ = _bias_tiles(rel_bias, bkt, madd)
    mla_sdiag = madd[None]

    tabs_p = _rope_tables(jnp.arange(tp, dtype=jnp.int32))
    tabs_s = _rope_tables(past + jnp.arange(ts, dtype=jnp.int32))
    cache_kt = jnp.transpose(cache_a_k, (0, 1, 3, 4, 2)).reshape(depth * bs, A_WIDTH, past)
    cache_vt = jnp.transpose(cache_a_v, (0, 1, 3, 4, 2)).reshape(depth * bs, A_WIDTH, past)
    zero_pool = jnp.zeros((bp, 16, B_WIDTH), F32)
    zero_conv = jnp.zeros((bp, 8, ffn), F32)

    hp, hs = x_prompt, x_sample
    new_p, new_s = [], []
    stack_p = stack_s = None
    for l in range(depth):
        w = _layer_weights(l, w_in, w_q_up, w_kv_up, pool_w, w_out, w_up)
        lam_init = 0.8 - 0.6 * math.exp(-0.3 * l)
        lamp = jnp.stack([lam_q1[l], lam_k1[l], lam_q2[l], lam_k2[l]], axis=0)
        g2 = jnp.concatenate([a_subln_g[l], a_subln_g[l]])[None]
        wd = w_down[l].astype(BF16)
        last = l == depth - 1

        def run(x, mod, tabs, pos0, hist16, hist8, pasts, prev):
            (aq, akt, avt, ub, bo, lat, kr, cq, ck, cv) = _premix(
                x, mod, g_mix[l][None], w["w1"], w["wkvt"], *tabs, c_q_norm_g[l][None], w["wq"],
                c_kv_norm_g[l][None], w["wkv"], w["wp"], pool_scale[l][None], hist16, pos0, prev)
            bsz = x.shape[0]
            kt_new = akt.reshape((-1,) + akt.shape[2:])
            vt_new = avt.reshape((-1,) + avt.shape[2:])
            new0 = kt_new.shape[0] - bsz
            if pasts is None:
                a_out = _attention(aq, kt_new, vt_new, kt_new, vt_new, pb_near, None, diff=True,
                                   causal_blocks=True, lamp=lamp, g2=g2, bias_max=bias_max, lam_init=lam_init,
                                   all_batch0=new0, new_batch0=new0)
                c_out = _attention(cq, ck, cv, ck, cv, None, mla_diag, diff=False, causal_blocks=True)
            else:
                plat, pkr = pasts
                a_out = _attention(aq, cache_kt, cache_vt, kt_new, vt_new, sb_sub, sb_diag, diff=True,
                                   causal_blocks=False, lamp=lamp, g2=g2, bias_max=bias_max, lam_init=lam_init,
                                   all_batch0=l * bs, new_batch0=new0)
                ckp, cvp = _kvpast(plat, pkr, w["wkv"])
                c_out = _attention(cq, ckp, cvp, ck, cv, None, mla_sdiag, diff=False, causal_blocks=False)
            y, conv8 = _postffn(a_out, bo, c_out, x, mod, w["wo"], g_ffn[l][None], w["wg"], w["wv"],
                                conv_w[l], conv_b[l][None], wd, hist8, g_final[None], last)
            tt = x.shape[1]
            return y, (akt, avt, lat), (kr, ub[:, tt - POOL_HIST:], conv8[:, 8 - (CONV_W - 1):])

        hp, stack_p, st = run(hp, mod_all[l, :bp], tabs_p, 0, zero_pool, zero_conv, None, stack_p)
        new_p.append(st)
        hist16 = jnp.pad(state_b_pool[l], ((0, 0), (1, 0), (0, 0)))
        hist8 = jnp.pad(state_ffn_conv[l], ((0, 0), (8 - (CONV_W - 1), 0), (0, 0)))
        hs, stack_s, st = run(hs, mod_all[l, bp:], tabs_s, past, hist16, hist8,
                              (cache_c_latent[l], cache_c_krope[l]), stack_s)
        new_s.append(st)

    def assemble(stack, small):
        akt, avt, lat = stack
        to_cache = lambda a: jnp.transpose(a.reshape(a.shape[:2] + (A_HEADS, A_V, a.shape[3])), (0, 1, 4, 2, 3))
        kr, pool, conv = [jnp.stack(z, axis=0) for z in zip(*small)]
        return to_cache(akt), to_cache(avt), lat, kr, pool, conv

    return (hp, hs, *assemble(stack_p, new_p), *assemble(stack_s, new_s))
```
